```python
import math
import jax, jax.numpy as jnp
from jax import lax
import numpy as np

D_MODEL = 1024
BATCH = 4
SEQ = 4096
DEPTH = 2

MEM_LEN = 256
D_FF = 11 * D_MODEL // 4
NORM_EPS = 1e-6
FFN_HALF = 0.5
CHUNK = 64
N_BRANCH = 4
BRANCH_WIDTH = D_MODEL // 2

MLSTM_HEADS = 4
MLSTM_HEAD_DIM = BRANCH_WIDTH // MLSTM_HEADS
MLSTM_CONV = 4

S5_GROUP = 16
S5_GROUPS = BRANCH_WIDTH // S5_GROUP
S5_STATE = 64
S5_MIN_NEG = 1e-4

GLA_HEADS = 4
GLA_KEY_WIDTH = BRANCH_WIDTH // 2
GLA_HEAD_K = GLA_KEY_WIDTH // GLA_HEADS
GLA_HEAD_V = BRANCH_WIDTH // GLA_HEADS
GLA_GATE_RANK = 16
GLA_GATE_TAU = 16.0

RWKV_HEAD = 64
RWKV_HEADS = BRANCH_WIDTH // RWKV_HEAD
RWKV_DECAY_RANK = 64
RWKV_ICLR_RANK = 64
RWKV_GATE_RANK = 128
RWKV_GN_EPS = 64e-5
RWKV_WIDTHS = (BRANCH_WIDTH, BRANCH_WIDTH, BRANCH_WIDTH, RWKV_DECAY_RANK, RWKV_ICLR_RANK, RWKV_GATE_RANK)
RWKV_COLS = sum(RWKV_WIDTHS)
RWKV_SPLIT_POINTS = tuple(int(s) for s in np.cumsum(RWKV_WIDTHS)[:-1])

XATTN_HEADS = 4
XATTN_HEAD_DIM = D_MODEL // XATTN_HEADS

IN_WIDTHS = (
    BRANCH_WIDTH, BRANCH_WIDTH, MLSTM_HEADS, MLSTM_HEADS,
    BRANCH_WIDTH,
    GLA_KEY_WIDTH, GLA_KEY_WIDTH, BRANCH_WIDTH, BRANCH_WIDTH,
    GLA_GATE_RANK,
    RWKV_COLS,
    N_BRANCH * D_MODEL,
)
N_IN = sum(IN_WIDTHS)
IN_SPLIT_POINTS = tuple(int(s) for s in np.cumsum(IN_WIDTHS)[:-1])

kernel_name = "hybrid_parallel_gated_mixer_trunk"


def rms_norm(x, gain):
    x32 = x.astype(jnp.float32)
    y = x32 * lax.rsqrt(jnp.mean(x32 * x32, axis=-1, keepdims=True) + NORM_EPS)
    return (y * gain.astype(jnp.float32)).astype(x.dtype)


def head_rms_norm(x, gain, n_heads):
    shp = x.shape
    xh = x.astype(jnp.float32).reshape(shp[:-1] + (n_heads, -1))
    y = xh * lax.rsqrt(jnp.mean(xh * xh, axis=-1, keepdims=True) + NORM_EPS)
    return y.reshape(shp) * gain.astype(jnp.float32)


def head_group_norm(x, gain, n_heads, eps):
    shp = x.shape
    xh = x.astype(jnp.float32).reshape(shp[:-1] + (n_heads, -1))
    mu = jnp.mean(xh, axis=-1, keepdims=True)
    xc = xh - mu
    y = xc * lax.rsqrt(jnp.mean(xc * xc, axis=-1, keepdims=True) + eps)
    return y.reshape(shp) * gain.astype(jnp.float32)


def swiglu_ffn(x, w_gate, w_up, w_down):
    return (jax.nn.silu(x @ w_gate) * (x @ w_up)) @ w_down


def causal_depthwise_conv(x, w):
    k_len = w.shape[0]
    t = x.shape[1]
    xp = jnp.pad(x, ((0, 0), (k_len - 1, 0), (0, 0)))
    return sum(xp[:, j:j + t] * w[j] for j in range(k_len))


def token_shift(x):
    return jnp.pad(x, ((0, 0), (1, 0), (0, 0)))[:, :-1, :]


def to_chunks4(x):
    b, t, h, e = x.shape
    return x.reshape(b, t // CHUNK, CHUNK, h, e).transpose(1, 0, 3, 2, 4)


def to_chunks3(x):
    b, t, h = x.shape
    return x.reshape(b, t // CHUNK, CHUNK, h).transpose(1, 0, 3, 2)


def from_chunks4(y):
    nc, b, h, c, e = y.shape
    return y.transpose(1, 0, 3, 2, 4).reshape(b, nc * c, h, e)


def mlstm_chunkwise(q, k, v, log_i, log_f):
    bsz, _, n_h, e = q.shape
    mask = jnp.tril(jnp.ones((CHUNK, CHUNK), dtype=bool))

    def step(carry, xs):
        c_st, n_st, m_st = carry
        qc, kc, vc, ic, fc = xs
        b = jnp.cumsum(fc, axis=-1)
        d = jnp.where(mask, b[..., :, None] - b[..., None, :] + ic[..., None, :], -jnp.inf)
        m_inter = b + m_st[..., None]
        m_row = jnp.maximum(m_inter, jnp.max(d, axis=-1))
        w = jnp.exp(d - m_row[..., None]) * jnp.einsum('bhte,bhse->bhts', qc, kc)
        inter = jnp.exp(m_inter - m_row)
        num = inter[..., None] * jnp.einsum('bhte,bhev->bhtv', qc, c_st) + jnp.einsum('bhts,bhsv->bhtv', w, vc)
        den = inter * jnp.einsum('bhte,bhe->bht', qc, n_st) + jnp.sum(w, axis=-1)
        h = num / jnp.maximum(jnp.abs(den), jnp.exp(-m_row))[..., None]
        b_last = b[..., -1]
        m_new = jnp.maximum(b_last + m_st, jnp.max(b_last[..., None] - b + ic, axis=-1))
        decay = jnp.exp(b_last + m_st - m_new)
        wk = jnp.exp(b_last[..., None] - b + ic - m_new[..., None])
        c_new = decay[..., None, None] * c_st + jnp.einsum('bhs,bhse,bhsv->bhev', wk, kc, vc)
        n_new = decay[..., None] * n_st + jnp.einsum('bhs,bhse->bhe', wk, kc)
        return (c_new, n_new, m_new), h

    init = (jnp.zeros((bsz, n_h, e, e), jnp.float32),
            jnp.zeros((bsz, n_h, e), jnp.float32),
            jnp.zeros((bsz, n_h), jnp.float32))
    _, h = lax.scan(step, init, (to_chunks4(q), to_chunks4(k), to_chunks4(v),
                                 to_chunks3(log_i), to_chunks3(log_f)))
    return from_chunks4(h)


def gla_chunkwise(q, k, v, log_a):
    bsz, _, n_h, dk = q.shape
    dv = v.shape[-1]
    mask = jnp.tril(jnp.ones((CHUNK, CHUNK), dtype=bool))[..., None]

    def step(s_st, xs):
        qc, kc, vc, ac = xs
        b = jnp.cumsum(ac, axis=2)
        inter = jnp.einsum('bhtk,bhkv->bhtv', qc * jnp.exp(b), s_st)
        diff = jnp.where(mask, b[:, :, :, None, :] - b[:, :, None, :, :], -jnp.inf)
        attn = jnp.einsum('bhtk,bhsk,bhtsk->bhts', qc, kc, jnp.exp(diff))
        o = inter + jnp.einsum('bhts,bhsv->bhtv', attn, vc)
        b_last = b[:, :, -1:, :]
        s_new = (jnp.exp(b_last[:, :, 0, :])[..., None] * s_st
                 + jnp.einsum('bhsk,bhsv->bhkv', kc * jnp.exp(b_last - b), vc))
        return s_new, o

    init = jnp.zeros((bsz, n_h, dk, dv), jnp.float32)
    _, o = lax.scan(step, init, (to_chunks4(q), to_chunks4(k), to_chunks4(v), to_chunks4(log_a)))
    return from_chunks4(o)


def rwkv7_scan(r, w, k, v, a, b):
    bsz, _, n_h, n = r.shape

    def step(s, xs):
        rt, wt, kt, vt, at, bt = xs
        sa = jnp.einsum('bhij,bhj->bhi', s, at)
        s = s * wt[:, :, None, :] + sa[..., None] * bt[:, :, None, :] + vt[..., None] * kt[:, :, None, :]
        return s, jnp.einsum('bhij,bhj->bhi', s, rt)

    xs = tuple(jnp.moveaxis(z, 1, 0) for z in (r, w, k, v, a, b))
    _, y = lax.scan(step, jnp.zeros((bsz, n_h, n, n), jnp.float32), xs)
    return jnp.moveaxis(y, 0, 1)


def complex_affine_combine(e1, e2):
    a1r, a1i, b1r, b1i = e1
    a2r, a2i, b2r, b2i = e2
    return (a2r * a1r - a2i * a1i,
            a2r * a1i + a2i * a1r,
            a2r * b1r - a2i * b1i + b2r,
            a2r * b1i + a2i * b1r + b2i)


def mlstm_branch(u, o_pre, i_pre, f_pre, conv_w, wq, wk, wv, b_i, b_f, norm_g, proj):
    bsz, t, _ = u.shape
    f32 = jnp.float32
    uc = jax.nn.silu(causal_depthwise_conv(u, conv_w))
    uc_h = uc.reshape(bsz, t, MLSTM_HEADS, MLSTM_HEAD_DIM)
    u_h = u.reshape(bsz, t, MLSTM_HEADS, MLSTM_HEAD_DIM)
    q = jnp.einsum('bthe,hef->bthf', uc_h, wq).astype(f32)
    k = jnp.einsum('bthe,hef->bthf', uc_h, wk).astype(f32) * MLSTM_HEAD_DIM ** -0.5
    v = jnp.einsum('bthe,hef->bthf', u_h, wv).astype(f32)
    log_i = (i_pre + b_i).astype(f32)
    log_f = jax.nn.log_sigmoid((f_pre + b_f).astype(f32))
    h = mlstm_chunkwise(q, k, v, log_i, log_f).reshape(bsz, t, BRANCH_WIDTH)
    h = head_rms_norm(h, norm_g, MLSTM_HEADS) * jax.nn.sigmoid(o_pre.astype(f32))
    return h.astype(u.dtype) @ proj


def s5_branch(u, a_re, a_im, log_step, b_re, b_im, c_re, c_im, d_skip, glu_w1, glu_w2):
    bsz, t, _ = u.shape
    f32 = jnp.float32
    u32 = u.astype(f32)
    xg = u32.reshape(bsz, t, S5_GROUPS, S5_GROUP)
    step = jnp.exp(log_step.astype(f32))[:, None]
    lam_re = jnp.minimum(a_re.astype(f32), -S5_MIN_NEG)
    lam_im = a_im.astype(f32)
    mag = jnp.exp(lam_re * step)
    bar_re = mag * jnp.cos(lam_im * step)
    bar_im = mag * jnp.sin(lam_im * step)
    denom = lam_re * lam_re + lam_im * lam_im
    coef_re = ((bar_re - 1.0) * lam_re + bar_im * lam_im) / denom
    coef_im = (bar_im * lam_re - (bar_re - 1.0) * lam_im) / denom
    br, bi = b_re.astype(f32), b_im.astype(f32)
    bb_re = coef_re[..., None] * br - coef_im[..., None] * bi
    bb_im = coef_re[..., None] * bi + coef_im[..., None] * br
    bu_re = jnp.einsum('btgc,gpc->btgp', xg, bb_re)
    bu_im = jnp.einsum('btgc,gpc->btgp', xg, bb_im)
    lr = jnp.broadcast_to(bar_re, bu_re.shape)
    li = jnp.broadcast_to(bar_im, bu_im.shape)
    _, _, s_re, s_im = lax.associative_scan(complex_affine_combine, (lr, li, bu_re, bu_im), axis=1)
    y = (jnp.einsum('btgp,gcp->btgc', s_re, c_re.astype(f32))
         - jnp.einsum('btgp,gcp->btgc', s_im, c_im.astype(f32)))
    y = y.reshape(bsz, t, BRANCH_WIDTH) + d_skip.astype(f32) * u32
    z = jax.nn.gelu(y).astype(u.dtype)
    return (z @ glu_w1) * jax.nn.sigmoid(z @ glu_w2)


def gla_branch(q, k, v, g, a_low, a_up, a_bias, norm_g, proj):
    bsz, t, _ = q.shape
    f32 = jnp.float32
    qh = q.astype(f32).reshape(bsz, t, GLA_HEADS, GLA_HEAD_K) * GLA_HEAD_K ** -0.5
    kh = k.astype(f32).reshape(bsz, t, GLA_HEADS, GLA_HEAD_K)
    vh = v.astype(f32).reshape(bsz, t, GLA_HEADS, GLA_HEAD_V)
    log_a = jax.nn.log_sigmoid((a_low @ a_up + a_bias).astype(f32)) / GLA_GATE_TAU
    log_a = log_a.reshape(bsz, t, GLA_HEADS, GLA_HEAD_K)
    o = gla_chunkwise(qh, kh, vh, log_a).reshape(bsz, t, BRANCH_WIDTH)
    o = head_rms_norm(o, norm_g, GLA_HEADS) * jax.nn.silu(g.astype(f32))
    return o.astype(q.dtype) @ proj


def rwkv7_branch(z, mu, w0, w_up, a0, a_up, g_up, k_k, k_a, r_k, norm_g, proj):
    bsz, t, _ = z.shape
    f32 = jnp.float32
    z = z + mu * (token_shift(z) - z)
    r, k, v, xw, xa, xg = jnp.split(z, RWKV_SPLIT_POINTS, axis=-1)
    w_log = -jax.nn.softplus(-(w0 + jnp.tanh(xw) @ w_up).astype(f32)) - 0.5
    decay = jnp.exp(-jnp.exp(w_log))
    a = jax.nn.sigmoid((a0 + xa @ a_up).astype(f32))
    g = jax.nn.sigmoid(xg) @ g_up

    def heads(y):
        return y.astype(f32).reshape(bsz, t, RWKV_HEADS, RWKV_HEAD)

    kk = heads(k * k_k)
    kk = kk / jnp.maximum(jnp.linalg.norm(kk, axis=-1, keepdims=True), 1e-12)
    k_rep = heads(k.astype(f32) * (1.0 + (a - 1.0) * k_a.astype(f32)))
    rh, vh, ah = heads(r), heads(v), heads(a)
    y = rwkv7_scan(rh, heads(decay), k_rep, vh, -kk, kk * ah)
    y = head_group_norm(y.reshape(bsz, t, BRANCH_WIDTH), norm_g, RWKV_HEADS, RWKV_GN_EPS)
    bonus = jnp.sum(rh * k_rep * r_k.astype(f32), axis=-1, keepdims=True) * vh
    y = (y + bonus.reshape(bsz, t, BRANCH_WIDTH)) * g.astype(f32)
    return y.astype(z.dtype) @ proj


def memory_cross_attention(h, mem_n, wq, wk, wv, wo):
    bsz, t, _ = h.shape
    m_len = mem_n.shape[1]
    q = (h @ wq).reshape(bsz, t, XATTN_HEADS, XATTN_HEAD_DIM)
    k = (mem_n @ wk).reshape(bsz, m_len, XATTN_HEADS, XATTN_HEAD_DIM)
    v = (mem_n @ wv).reshape(bsz, m_len, XATTN_HEADS, XATTN_HEAD_DIM)
    s = jnp.einsum('bthd,bmhd->bhtm', q, k).astype(jnp.float32) * XATTN_HEAD_DIM ** -0.5
    p = jax.nn.softmax(s, axis=-1).astype(h.dtype)
    o = jnp.einsum('bhtm,bmhd->bthd', p, v).reshape(bsz, t, D_MODEL)
    return o @ wo


def setup_inputs(seed: int = 0) -> dict:
    key = jax.random.key(seed)
    ks = iter(jax.random.split(key, 80))
    f32 = jnp.float32
    L, D, W, F = DEPTH, D_MODEL, BRANCH_WIDTH, D_FF

    def nrm(shape, scale):
        return jax.random.normal(next(ks), shape, f32) * scale

    def gain(shape):
        return 1.0 + nrm(shape, 0.02)

    def unif(shape, lo, hi):
        return jax.random.uniform(next(ks), shape, f32, lo, hi)

    p = {}
    p["x"] = nrm((BATCH, SEQ, D), 1.0)
    p["mem"] = nrm((BATCH, MEM_LEN, D), 1.0)
    p["ffn1_norm"] = gain((L, D))
    p["ffn1_w_gate"] = nrm((L, D, F), D ** -0.5)
    p["ffn1_w_up"] = nrm((L, D, F), D ** -0.5)
    p["ffn1_w_down"] = nrm((L, F, D), F ** -0.5)
    p["mix_norm"] = gain((L, D))
    p["w_in"] = nrm((L, D, N_IN), D ** -0.5)
    p["gate_bias"] = nrm((L, N_BRANCH * D), 0.1)
    p["mlstm_conv"] = nrm((L, MLSTM_CONV, W), MLSTM_CONV ** -0.5)
    p["mlstm_wq"] = nrm((L, MLSTM_HEADS, MLSTM_HEAD_DIM, MLSTM_HEAD_DIM), MLSTM_HEAD_DIM ** -0.5)
    p["mlstm_wk"] = nrm((L, MLSTM_HEADS, MLSTM_HEAD_DIM, MLSTM_HEAD_DIM), MLSTM_HEAD_DIM ** -0.5)
    p["mlstm_wv"] = nrm((L, MLSTM_HEADS, MLSTM_HEAD_DIM, MLSTM_HEAD_DIM), MLSTM_HEAD_DIM ** -0.5)
    p["mlstm_b_i"] = nrm((L, MLSTM_HEADS), 0.1)
    p["mlstm_b_f"] = jnp.linspace(3.0, 6.0, MLSTM_HEADS, dtype=f32)[None, :] + nrm((L, MLSTM_HEADS), 0.1)
    p["mlstm_norm"] = gain((L, W))
    p["mlstm_proj"] = nrm((L, W, D), W ** -0.5)
    p["s5_a_re"] = -0.5 + nrm((L, S5_GROUPS, S5_STATE), 0.01)
    p["s5_a_im"] = (jnp.broadcast_to(math.pi * jnp.arange(S5_STATE, dtype=f32), (L, S5_GROUPS, S5_STATE))
                    + nrm((L, S5_GROUPS, S5_STATE), 0.01))
    p["s5_log_step"] = unif((L, S5_GROUPS), math.log(1e-3), math.log(1e-1))
    p["s5_b_re"] = nrm((L, S5_GROUPS, S5_STATE, S5_GROUP), (2 * S5_GROUP) ** -0.5)
    p["s5_b_im"] = nrm((L, S5_GROUPS, S5_STATE, S5_GROUP), (2 * S5_GROUP) ** -0.5)
    p["s5_c_re"] = nrm((L, S5_GROUPS, S5_GROUP, S5_STATE), S5_STATE ** -0.5)
    p["s5_c_im"] = nrm((L, S5_GROUPS, S5_GROUP, S5_STATE), S5_STATE ** -0.5)
    p["s5_d"] = nrm((L, W), 1.0)
    p["s5_glu_w1"] = nrm((L, W, D), W ** -0.5)
    p["s5_glu_w2"] = nrm((L, W, D), W ** -0.5)
    p["gla_a_up"] = nrm((L, GLA_GATE_RANK, GLA_KEY_WIDTH), GLA_GATE_RANK ** -0.5)
    p["gla_a_bias"] = nrm((L, GLA_KEY_WIDTH), 0.1)
    p["gla_norm"] = gain((L, W))
    p["gla_proj"] = nrm((L, W, D), W ** -0.5)
    p["rwkv_mu"] = unif((L, RWKV_COLS), 0.0, 1.0)
    p["rwkv_w0"] = jnp.linspace(-6.5, -1.5, W, dtype=f32)[None, :] + nrm((L, W), 0.1)
    p["rwkv_w_up"] = nrm((L, RWKV_DECAY_RANK, W), 0.1)
    p["rwkv_a0"] = nrm((L, W), 0.1)
    p["rwkv_a_up"] = nrm((L, RWKV_ICLR_RANK, W), 0.1)
    p["rwkv_g_up"] = nrm((L, RWKV_GATE_RANK, W), RWKV_GATE_RANK ** -0.5)
    p["rwkv_k_k"] = 0.85 + nrm((L, W), 0.02)
    p["rwkv_k_a"] = 1.0 + nrm((L, W), 0.02)
    p["rwkv_r_k"] = nrm((L, RWKV_HEADS, RWKV_HEAD), 0.1)
    p["rwkv_norm"] = gain((L, W))
    p["rwkv_proj"] = nrm((L, W, D), W ** -0.5)
    p["w_out"] = nrm((L, D, D), D ** -0.5)
    p["xattn_norm"] = gain((L, D))
    p["mem_norm"] = gain((L, D))
    p["xattn_wq"] = nrm((L, D, D), D ** -0.5)
    p["xattn_wk"] = nrm((L, D, D), D ** -0.5)
    p["xattn_wv"] = nrm((L, D, D), D ** -0.5)
    p["xattn_wo"] = nrm((L, D, D), D ** -0.5)
    p["ffn2_norm"] = gain((L, D))
    p["ffn2_w_gate"] = nrm((L, D, F), D ** -0.5)
    p["ffn2_w_up"] = nrm((L, D, F), D ** -0.5)
    p["ffn2_w_down"] = nrm((L, F, D), F ** -0.5)
    p["final_norm"] = gain((D,))
    return p


def reference(x, mem,
              ffn1_norm, ffn1_w_gate, ffn1_w_up, ffn1_w_down,
              mix_norm, w_in, gate_bias,
              mlstm_conv, mlstm_wq, mlstm_wk, mlstm_wv, mlstm_b_i, mlstm_b_f, mlstm_norm, mlstm_proj,
              s5_a_re, s5_a_im, s5_log_step, s5_b_re, s5_b_im, s5_c_re, s5_c_im, s5_d, s5_glu_w1, s5_glu_w2,
              gla_a_up, gla_a_bias, gla_norm, gla_proj,
              rwkv_mu, rwkv_w0, rwkv_w_up, rwkv_a0, rwkv_a_up, rwkv_g_up, rwkv_k_k, rwkv_k_a, rwkv_r_k,
              rwkv_norm, rwkv_proj,
              w_out,
              xattn_norm, mem_norm, xattn_wq, xattn_wk, xattn_wv, xattn_wo,
              ffn2_norm, ffn2_w_gate, ffn2_w_up, ffn2_w_down,
              final_norm):
    bsz, t, _ = x.shape
    dt = x.dtype
    for l in range(DEPTH):
        h = rms_norm(x, ffn1_norm[l])
        x = x + FFN_HALF * swiglu_ffn(h, ffn1_w_gate[l], ffn1_w_up[l], ffn1_w_down[l])

        h = rms_norm(x, mix_norm[l])
        z = h @ w_in[l]
        (a_u, a_o, a_i, a_f, b_u, c_q, c_k, c_v, c_g, c_a, d_z, gate_pre) = jnp.split(
            z, IN_SPLIT_POINTS, axis=-1)
        y_a = mlstm_branch(a_u, a_o, a_i, a_f, mlstm_conv[l], mlstm_wq[l], mlstm_wk[l], mlstm_wv[l],
                           mlstm_b_i[l], mlstm_b_f[l], mlstm_norm[l], mlstm_proj[l])
        y_b = s5_branch(b_u, s5_a_re[l], s5_a_im[l], s5_log_step[l], s5_b_re[l], s5_b_im[l],
                        s5_c_re[l], s5_c_im[l], s5_d[l], s5_glu_w1[l], s5_glu_w2[l])
        y_c = gla_branch(c_q, c_k, c_v, c_g, c_a, gla_a_up[l], gla_a_bias[l], gla_norm[l], gla_proj[l])
        y_d = rwkv7_branch(d_z, rwkv_mu[l], rwkv_w0[l], rwkv_w_up[l], rwkv_a0[l], rwkv_a_up[l],
                           rwkv_g_up[l], rwkv_k_k[l], rwkv_k_a[l], rwkv_r_k[l], rwkv_norm[l], rwkv_proj[l])
        gates = jax.nn.sigmoid((gate_pre + gate_bias[l]).astype(jnp.float32)).astype(dt)
        gates = gates.reshape(bsz, t, N_BRANCH, D_MODEL)
        merged = (gates[:, :, 0] * y_a + gates[:, :, 1] * y_b
                  + gates[:, :, 2] * y_c + gates[:, :, 3] * y_d)
        x = x + merged @ w_out[l]

        h = rms_norm(x, xattn_norm[l])
        m = rms_norm(mem, mem_norm[l])
        x = x + memory_cross_attention(h, m, xattn_wq[l], xattn_wk[l], xattn_wv[l], xattn_wo[l])

        h = rms_norm(x, ffn2_norm[l])
        x = x + FFN_HALF * swiglu_ffn(h, ffn2_w_gate[l], ffn2_w_up[l], ffn2_w_down[l])
    return rms_norm(x, final_norm)
```

```python
import functools
import math

import jax
import jax.numpy as jnp
import numpy as np
from jax import lax
from jax.experimental import pallas as pl
from jax.experimental.pallas import tpu as pltpu

F32 = jnp.float32
BF16 = jnp.bfloat16

D_MODEL = 1024
D_FF = 11 * D_MODEL // 4
NORM_EPS = 1e-6
FFN_HALF = 0.5
N_BRANCH = 4
WIDTH = D_MODEL // 2

MLSTM_HEADS = 4
MLSTM_HEAD_DIM = WIDTH // MLSTM_HEADS
MLSTM_CONV = 4

S5_GROUP = 16
S5_GROUPS = WIDTH // S5_GROUP
S5_STATE = 64
S5_MIN_NEG = 1e-4
S5_BLOCKS = 4
S5_BLOCK_STATES = (S5_GROUPS // S5_BLOCKS) * S5_STATE

GLA_HEADS = 4
GLA_KEY_WIDTH = WIDTH // 2
GLA_HEAD_K = GLA_KEY_WIDTH // GLA_HEADS
GLA_HEAD_V = WIDTH // GLA_HEADS
GLA_GATE_RANK = 16
GLA_GATE_TAU = 16.0
GLA_CHUNK = 64

RWKV_HEAD = 64
RWKV_HEADS = WIDTH // RWKV_HEAD
RWKV_DECAY_RANK = 64
RWKV_ICLR_RANK = 64
RWKV_GATE_RANK = 128
RWKV_TAIL = RWKV_DECAY_RANK + RWKV_ICLR_RANK + RWKV_GATE_RANK
RWKV_GN_EPS = 64e-5
RWKV_CHUNK = 64
RWKV_GROUP_LANES = 256
RWKV_GROUP_HEADS = RWKV_GROUP_LANES // RWKV_HEAD

XATTN_HEADS = 4
XATTN_HEAD_DIM = D_MODEL // XATTN_HEADS

ZC_GATE = 0
ZC_AU = 4096
ZC_AO = 4608
ZC_BU = 5120
ZC_CV = 5632
ZC_CG = 6144
ZC_DR = 6656
ZC_DK = 7168
ZC_DV = 7680
ZC_CQ = 8192
ZC_CK = 8448
ZC_DT = 8704
ZC_SM = 8960
Z_COLS = 9216
SMALL = 128

TIME_BLOCK = 256
VMEM_LIMIT = 56 * 1024 * 1024

_NN = (((1,), (0,)), ((), ()))
_NT = (((1,), (1,)), ((), ()))


def _split(x, n):
    parts = []
    rest = x
    for i in range(n):
        p = rest.astype(BF16)
        parts.append(p)
        if i + 1 < n:
            rest = rest - p.astype(F32)
    return parts


def _mm(a, b, dn=_NN, pa=1, pb=1):
    ap, bp = _split(a, pa), _split(b, pb)
    order = max(pa, pb)
    out = None
    for i, x in enumerate(ap):
        for j, y in enumerate(bp):
            if i + j < order:
                t = lax.dot_general(x, y, dn, preferred_element_type=F32)
                out = t if out is None else out + t
    return out


def _rms(x, g):
    return x * lax.rsqrt(jnp.mean(x * x, axis=-1, keepdims=True) + NORM_EPS) * g


def _sigmoid(x):
    return 1.0 / (1.0 + jnp.exp(-x))


def _log_sigmoid(x):
    return jnp.minimum(x, 0.0) - jnp.log(1.0 + jnp.exp(-jnp.abs(x)))


def _iota(shape, dim):
    return lax.broadcasted_iota(jnp.int32, shape, dim)


def _tri(n, block, upper=False, strict=False):
    r, c = _iota((n, n), 0), _iota((n, n), 1)
    same = (r // block) == (c // block)
    if upper:
        keep = (r < c) if strict else (r <= c)
    else:
        keep = (r > c) if strict else (r >= c)
    return jnp.where(same & keep, 1.0, 0.0).astype(BF16)


def _shift_rows(prev_ref, x, k):
    ext = jnp.concatenate([prev_ref[...], x], axis=0)
    return pltpu.roll(ext, k, 0)[prev_ref.shape[0]:]


def _ffn_kernel(x_ref, g_ref, wg_ref, wu_ref, wd_ref, fg_ref, o_ref, xn_ref, acc_ref, *, final):
    j = pl.program_id(1)

    @pl.when(j == 0)
    def _():
        xn_ref[...] = _rms(x_ref[...], g_ref[...]).astype(BF16)
        acc_ref[...] = jnp.zeros_like(acc_ref)

    xn = xn_ref[...]
    gate = jnp.dot(xn, wg_ref[...], preferred_element_type=F32)
    up = jnp.dot(xn, wu_ref[...], preferred_element_type=F32)
    h = (gate * _sigmoid(gate) * up).astype(BF16)
    acc_ref[...] += jnp.dot(h, wd_ref[...], preferred_element_type=F32)

    @pl.when(j == pl.num_programs(1) - 1)
    def _():
        y = x_ref[...] + FFN_HALF * acc_ref[...]
        if final:
            y = _rms(y, fg_ref[...])
        o_ref[...] = y


def _ffn(x, gain, wg, wu, wd, final_gain, *, final, tm=1024, tf=256):
    n, d = x.shape
    f = wg.shape[1]
    return pl.pallas_call(
        functools.partial(_ffn_kernel, final=final),
        grid=(n // tm, f // tf),
        in_specs=[
            pl.BlockSpec((tm, d), lambda i, j: (i, 0)),
            pl.BlockSpec((1, d), lambda i, j: (0, 0)),
            pl.BlockSpec((d, tf), lambda i, j: (0, j)),
            pl.BlockSpec((d, tf), lambda i, j: (0, j)),
            pl.BlockSpec((tf, d), lambda i, j: (j, 0)),
            pl.BlockSpec((1, d), lambda i, j: (0, 0)),
        ],
        out_specs=pl.BlockSpec((tm, d), lambda i, j: (i, 0)),
        out_shape=jax.ShapeDtypeStruct((n, d), F32),
        scratch_shapes=[pltpu.VMEM((tm, d), BF16), pltpu.VMEM((tm, d), F32)],
        compiler_params=pltpu.CompilerParams(
            dimension_semantics=("parallel", "arbitrary"), vmem_limit_bytes=VMEM_LIMIT),
        name="ffn",
    )(x, gain, wg, wu, wd, final_gain)


def _norm_matmul_kernel(x_ref, g_ref, w_ref, o_ref, xn_ref):
    @pl.when(pl.program_id(1) == 0)
    def _():
        xn_ref[...] = _rms(x_ref[...], g_ref[...]).astype(BF16)

    o_ref[...] = jnp.dot(xn_ref[...], w_ref[...], preferred_element_type=F32)


def _norm_matmul(x, gain, w, *, tm, tn, name):
    n, d = x.shape
    m = w.shape[1]
    return pl.pallas_call(
        _norm_matmul_kernel,
        grid=(n // tm, m // tn),
        in_specs=[
            pl.BlockSpec((tm, d), lambda i, j: (i, 0)),
            pl.BlockSpec((1, d), lambda i, j: (0, 0)),
            pl.BlockSpec((d, tn), lambda i, j: (0, j)),
        ],
        out_specs=pl.BlockSpec((tm, tn), lambda i, j: (i, j)),
        out_shape=jax.ShapeDtypeStruct((n, m), F32),
        scratch_shapes=[pltpu.VMEM((tm, d), BF16)],
        compiler_params=pltpu.CompilerParams(
            dimension_semantics=("parallel", "arbitrary"), vmem_limit_bytes=VMEM_LIMIT),
        name=name,
    )(x, gain, w)


def _mlstm_kernel(u_ref, o_ref, sm_ref, conv_ref, wq_ref, wk_ref, wv_ref, brow_ref, bcol_ref, ng_ref,
                  out_ref, uprev_ref, c_ref, n_ref, m_ref):
    lb = u_ref.shape[0]
    e = MLSTM_HEAD_DIM

    @pl.when(pl.program_id(1) == 0)
    def _():
        uprev_ref[...] = jnp.zeros_like(uprev_ref)
        c_ref[...] = jnp.zeros_like(c_ref)
        n_ref[...] = jnp.zeros_like(n_ref)
        m_ref[...] = jnp.zeros_like(m_ref)

    u = u_ref[...]
    conv = None
    for j in range(MLSTM_CONV):
        k = MLSTM_CONV - 1 - j
        term = conv_ref[j:j + 1, :] * (_shift_rows(uprev_ref, u, k) if k else u)
        conv = term if conv is None else conv + term
    uprev_ref[...] = u[lb - uprev_ref.shape[0]:]
    uc = conv * _sigmoid(conv)

    sm = sm_ref[...]
    pre = sm + brow_ref[...]
    log_f = _log_sigmoid(pre)
    pre_t = sm.T[0:8, :] + bcol_ref[0:8, :]
    log_f_t = _log_sigmoid(pre_t)
    b_cols = _mm(_tri(lb, lb), log_f, pb=3)
    b_rows = _mm(log_f_t, _tri(lb, lb, upper=True), pa=3)

    causal = _iota((lb, lb), 0) >= _iota((lb, lb), 1)
    for h in range(MLSTM_HEADS):
        sl = slice(h * e, (h + 1) * e)
        q = _mm(uc[:, sl], wq_ref[h])
        k = _mm(uc[:, sl], wk_ref[h]) * e ** -0.5
        v = _mm(u[:, sl], wv_ref[h])
        bc = b_cols[:, MLSTM_HEADS + h:MLSTM_HEADS + h + 1]
        br = b_rows[MLSTM_HEADS + h:MLSTM_HEADS + h + 1, :]
        i_row = pre_t[h:h + 1, :]
        i_col = pre[:, h:h + 1]
        c_st = c_ref[h]
        n_st = n_ref[h:h + 1, :]
        m_st = m_ref[h:h + 1, 0:1]

        dmat = jnp.where(causal, bc - br + i_row, -jnp.inf)
        m_inter = bc + m_st
        m_row = jnp.maximum(m_inter, jnp.max(dmat, axis=-1, keepdims=True))
        w = jnp.exp(dmat - m_row) * _mm(q, k, _NT)
        inter = jnp.exp(m_inter - m_row)
        num = inter * _mm(q, c_st) + _mm(w, v)
        den = inter * jnp.sum(q * n_st, axis=-1, keepdims=True) + jnp.sum(w, axis=-1, keepdims=True)
        hh = num / jnp.maximum(jnp.abs(den), jnp.exp(-m_row))

        b_last = bc[lb - 1:lb, :]
        e_col = b_last - bc + i_col
        m_new = jnp.maximum(b_last + m_st, jnp.max(e_col, axis=0, keepdims=True))
        decay = jnp.exp(b_last + m_st - m_new)
        wk = jnp.exp(e_col - m_new) * k
        c_ref[h] = decay * c_st + _mm(wk.T, v)
        n_ref[h:h + 1, :] = decay * n_st + jnp.sum(wk, axis=0, keepdims=True)
        m_ref[h:h + 1, :] = jnp.broadcast_to(m_new, (1, m_ref.shape[1]))

        hn = hh * lax.rsqrt(jnp.mean(hh * hh, axis=-1, keepdims=True) + NORM_EPS) * ng_ref[:, sl]
        out_ref[:, sl] = hn * _sigmoid(o_ref[:, sl])


def _mlstm(z, bsz, conv_w, wq, wk, wv, b_row, b_col, norm_g):
    n = z.shape[0]
    lb = TIME_BLOCK
    nblk = n // bsz // lb
    row = lambda b, c: b * nblk + c
    full = lambda *shape: pl.BlockSpec(shape, lambda b, c: (0,) * len(shape))
    return pl.pallas_call(
        _mlstm_kernel,
        grid=(bsz, nblk),
        in_specs=[
            pl.BlockSpec((lb, WIDTH), lambda b, c: (row(b, c), ZC_AU // WIDTH)),
            pl.BlockSpec((lb, WIDTH), lambda b, c: (row(b, c), ZC_AO // WIDTH)),
            pl.BlockSpec((lb, SMALL), lambda b, c: (row(b, c), ZC_SM // SMALL)),
            full(MLSTM_CONV, WIDTH),
            full(MLSTM_HEADS, MLSTM_HEAD_DIM, MLSTM_HEAD_DIM),
            full(MLSTM_HEADS, MLSTM_HEAD_DIM, MLSTM_HEAD_DIM),
            full(MLSTM_HEADS, MLSTM_HEAD_DIM, MLSTM_HEAD_DIM),
            full(1, SMALL),
            full(SMALL, 1),
            full(1, WIDTH),
        ],
        out_specs=pl.BlockSpec((lb, WIDTH), lambda b, c: (row(b, c), 0)),
        out_shape=jax.ShapeDtypeStruct((n, WIDTH), F32),
        scratch_shapes=[
            pltpu.VMEM((8, WIDTH), F32),
            pltpu.VMEM((MLSTM_HEADS, MLSTM_HEAD_DIM, MLSTM_HEAD_DIM), F32),
            pltpu.VMEM((8, MLSTM_HEAD_DIM), F32),
            pltpu.VMEM((8, MLSTM_HEAD_DIM), F32),
        ],
        compiler_params=pltpu.CompilerParams(
            dimension_semantics=("parallel", "arbitrary"), vmem_limit_bytes=VMEM_LIMIT),
        name="mlstm",
    )(z, z, z, conv_w, wq, wk, wv, b_row, b_col, norm_g)


def _s5_kernel(u_ref, bm_ref, cm_ref, pre_ref, pim_ref, d_ref, out_ref, cre_ref, cim_ref):
    lb = u_ref.shape[0]
    ns = S5_BLOCK_STATES
    cw = WIDTH // S5_BLOCKS

    @pl.when(pl.program_id(1) == 0)
    def _():
        cre_ref[...] = jnp.zeros_like(cre_ref)
        cim_ref[...] = jnp.zeros_like(cim_ref)

    u = u_ref[...]
    rows = _iota((lb, ns), 0)
    for j in range(S5_BLOCKS):
        cs = slice(j * ns, (j + 1) * ns)
        bu = _mm(u[:, j * cw:(j + 1) * cw], bm_ref[j], pa=2, pb=2)
        sr, si = bu[:, :ns], bu[:, ns:]
        d = 1
        while d < lb:
            lr, li = pre_ref[d - 1:d, cs], pim_ref[d - 1:d, cs]
            keep = rows >= d
            shr = jnp.where(keep, pltpu.roll(sr, d, 0), 0.0)
            shi = jnp.where(keep, pltpu.roll(si, d, 0), 0.0)
            sr, si = sr + lr * shr - li * shi, si + lr * shi + li * shr
            d *= 2
        pr, pi = pre_ref[:, cs], pim_ref[:, cs]
        cr, ci = cre_ref[0:1, cs], cim_ref[0:1, cs]
        sr = sr + pr * cr - pi * ci
        si = si + pr * ci + pi * cr
        cre_ref[0:1, cs] = sr[lb - 1:lb, :]
        cim_ref[0:1, cs] = si[lb - 1:lb, :]
        y = _mm(jnp.concatenate([sr, si], axis=1), cm_ref[j], pa=2, pb=2)
        ucols = slice(j * cw, (j + 1) * cw)
        y = y + d_ref[:, ucols] * u[:, ucols]
        out_ref[:, ucols] = 0.5 * y * (1.0 + jnp.tanh(math.sqrt(2.0 / math.pi) * (y + 0.044715 * y * y * y)))


def _s5(z, bsz, b_mat, c_mat, pow_re, pow_im, d_skip):
    n = z.shape[0]
    lb = TIME_BLOCK
    nblk = n // bsz // lb
    row = lambda b, c: b * nblk + c
    full = lambda *shape: pl.BlockSpec(shape, lambda b, c: (0,) * len(shape))
    nst = S5_GROUPS * S5_STATE
    return pl.pallas_call(
        _s5_kernel,
        grid=(bsz, nblk),
        in_specs=[
            pl.BlockSpec((lb, WIDTH), lambda b, c: (row(b, c), ZC_BU // WIDTH)),
            full(S5_BLOCKS, WIDTH // S5_BLOCKS, 2 * S5_BLOCK_STATES),
            full(S5_BLOCKS, 2 * S5_BLOCK_STATES, WIDTH // S5_BLOCKS),
            full(lb, nst),
            full(lb, nst),
            full(1, WIDTH),
        ],
        out_specs=pl.BlockSpec((lb, WIDTH), lambda b, c: (row(b, c), 0)),
        out_shape=jax.ShapeDtypeStruct((n, WIDTH), F32),
        scratch_shapes=[pltpu.VMEM((8, nst), F32), pltpu.VMEM((8, nst), F32)],
        compiler_params=pltpu.CompilerParams(
            dimension_semantics=("parallel", "arbitrary"), vmem_limit_bytes=VMEM_LIMIT),
        name="s5",
    )(z, b_mat, c_mat, pow_re, pow_im, d_skip)


def _gla_kernel(q_ref, k_ref, v_ref, g_ref, sm_ref, aup_ref, ab_ref, ng_ref, out_ref, st_ref):
    lb = q_ref.shape[0]
    ch = GLA_CHUNK
    kw = GLA_KEY_WIDTH

    @pl.when(pl.program_id(1) == 0)
    def _():
        st_ref[...] = jnp.zeros_like(st_ref)

    x = _mm(sm_ref[...], aup_ref[...], pa=2, pb=2) + ab_ref[...]
    log_a = _log_sigmoid(x) * (1.0 / GLA_GATE_TAU)
    b_all = _mm(_tri(lb, ch), log_a, pb=3)
    lane_head = _iota((1, kw), 1) // GLA_HEAD_K
    masks = [jnp.where(lane_head == h, 1.0, 0.0) for h in range(GLA_HEADS)]
    causal = _iota((ch, ch), 0) >= _iota((ch, ch), 1)
    mid = ch // 2 - 1

    for c in range(lb // ch):
        rs = slice(c * ch, (c + 1) * ch)
        b = b_all[rs]
        b_mid, b_last = b[mid:mid + 1, :], b[ch - 1:ch, :]
        q = q_ref[rs, :] * GLA_HEAD_K ** -0.5
        k = k_ref[rs, :]
        q_in = q * jnp.exp(b)
        q_at = q * jnp.exp(b - b_mid)
        k_at = (k * jnp.exp(b_mid - b)).astype(BF16)
        k_st = k * jnp.exp(b_last - b)
        dec = jnp.exp(b_last)
        for h in range(GLA_HEADS):
            vs = slice(h * GLA_HEAD_V, (h + 1) * GLA_HEAD_V)
            v = v_ref[rs, vs]
            st = st_ref[h]
            attn = jnp.where(causal, _mm(q_at * masks[h], k_at, _NT), 0.0)
            o = _mm(q_in * masks[h], st, _NT) + _mm(attn, v)
            st_ref[h] = dec * st + _mm(v.T, k_st * masks[h])
            on = o * lax.rsqrt(jnp.mean(o * o, axis=-1, keepdims=True) + NORM_EPS) * ng_ref[:, vs]
            g = g_ref[rs, vs]
            out_ref[rs, vs] = on * (g * _sigmoid(g))


def _gla(z, bsz, a_up, a_bias, norm_g):
    n = z.shape[0]
    lb = TIME_BLOCK
    nblk = n // bsz // lb
    row = lambda b, c: b * nblk + c
    full = lambda *shape: pl.BlockSpec(shape, lambda b, c: (0,) * len(shape))
    return pl.pallas_call(
        _gla_kernel,
        grid=(bsz, nblk),
        in_specs=[
            pl.BlockSpec((lb, GLA_KEY_WIDTH), lambda b, c: (row(b, c), ZC_CQ // GLA_KEY_WIDTH)),
            pl.BlockSpec((lb, GLA_KEY_WIDTH), lambda b, c: (row(b, c), ZC_CK // GLA_KEY_WIDTH)),
            pl.BlockSpec((lb, WIDTH), lambda b, c: (row(b, c), ZC_CV // WIDTH)),
            pl.BlockSpec((lb, WIDTH), lambda b, c: (row(b, c), ZC_CG // WIDTH)),
            pl.BlockSpec((lb, SMALL), lambda b, c: (row(b, c), ZC_SM // SMALL)),
            full(SMALL, GLA_KEY_WIDTH),
            full(1, GLA_KEY_WIDTH),
            full(1, WIDTH),
        ],
        out_specs=pl.BlockSpec((lb, WIDTH), lambda b, c: (row(b, c), 0)),
        out_shape=jax.ShapeDtypeStruct((n, WIDTH), F32),
        scratch_shapes=[pltpu.VMEM((GLA_HEADS, GLA_HEAD_V, GLA_KEY_WIDTH), F32)],
        compiler_params=pltpu.CompilerParams(
            dimension_semantics=("parallel", "arbitrary"), vmem_limit_bytes=VMEM_LIMIT),
        name="gla",
    )(z, z, z, z, z, a_up, a_bias, norm_g)


def _rwkv_kernel(r_ref, k_ref, v_ref, t_ref, mur_ref, muk_ref, muv_ref, mut_ref,
                 w0_ref, wup_ref, a0_ref, aup_ref, gup_ref, kk_ref, ka_ref, rk_ref, ng_ref,
                 out_ref, pr_ref, pk_ref, pv_ref, pt_ref, st_ref):
    lb = r_ref.shape[0]
    ch = RWKV_CHUNK
    gl = RWKV_GROUP_LANES
    gh = RWKV_GROUP_HEADS
    rows_s = gh * ch

    @pl.when(pl.program_id(1) == 0)
    def _():
        for ref in (pr_ref, pk_ref, pv_ref, pt_ref, st_ref):
            ref[...] = jnp.zeros_like(ref)

    def shift_mix(x_ref, prev_ref, mu_ref):
        x = x_ref[...]
        xs = _shift_rows(prev_ref, x, 1)
        prev_ref[...] = x[lb - prev_ref.shape[0]:]
        return x + mu_ref[...] * (xs - x)

    r = shift_mix(r_ref, pr_ref, mur_ref)
    k = shift_mix(k_ref, pk_ref, muk_ref)
    v = shift_mix(v_ref, pv_ref, muv_ref)
    tl = shift_mix(t_ref, pt_ref, mut_ref)

    ones_bd = _tri(WIDTH, RWKV_HEAD) + _tri(WIDTH, RWKV_HEAD, upper=True, strict=True)
    w_log = _log_sigmoid(w0_ref[...] + _mm(jnp.tanh(tl), wup_ref[...])) - 0.5
    log_w = -jnp.exp(w_log)
    a = _sigmoid(a0_ref[...] + _mm(tl, aup_ref[...]))
    g = _mm(_sigmoid(tl), gup_ref[...])
    kk = k * kk_ref[...]
    kk = kk / jnp.maximum(jnp.sqrt(_mm(kk * kk, ones_bd, pa=3)), 1e-12)
    k_rep = k * (1.0 + (a - 1.0) * ka_ref[...])
    av = -kk
    bv = kk * a
    bonus = _mm(r * k_rep * rk_ref[...], ones_bd, pa=3) * v
    gc_all = _mm(_tri(lb, ch), log_w, pb=3)

    lane_head = _iota((1, gl), 1) // RWKV_HEAD
    masks = [jnp.where(lane_head == h, 1.0, 0.0) for h in range(gh)]

    def stack(x):
        return jnp.concatenate([x * m for m in masks], axis=0)

    t_row = _iota((rows_s, rows_s), 0) % ch
    t_col = _iota((rows_s, rows_s), 1) % ch
    strict = t_row > t_col
    incl = t_row >= t_col
    eye = jnp.where(_iota((rows_s, rows_s), 0) == _iota((rows_s, rows_s), 1), 1.0, 0.0)

    for gi in range(WIDTH // gl):
        ls = slice(gi * gl, (gi + 1) * gl)
        for c in range(lb // ch):
            rs = slice(c * ch, (c + 1) * ch)
            gc = gc_all[rs, ls]
            lw = log_w[rs, ls]
            g_last = gc[ch - 1:ch, :]
            e_inc = jnp.exp(gc)
            e_inv = jnp.exp(-gc)
            e_end = jnp.exp(g_last - gc)
            r_t = stack(r[rs, ls] * e_inc)
            a_t = stack(av[rs, ls] * jnp.exp(gc - lw))
            b_t = stack(bv[rs, ls] * e_inv)
            k_t = stack(k_rep[rs, ls] * e_inv)
            b_e = stack(bv[rs, ls] * e_end)
            k_e = stack(k_rep[rs, ls] * e_end)
            v_s = stack(v[rs, ls])
            st = st_ref[gi]

            ar = jnp.concatenate([a_t, r_t], axis=0)
            bk = jnp.concatenate([b_t, k_t], axis=0)
            mm = _mm(ar, bk, _NT, pa=2, pb=2)
            a_ab = jnp.where(strict, mm[:rows_s, :rows_s], 0.0)
            a_ak = jnp.where(strict, mm[:rows_s, rows_s:], 0.0)
            a_rb = jnp.where(incl, mm[rows_s:, :rows_s], 0.0)
            a_rk = jnp.where(incl, mm[rows_s:, rows_s:], 0.0)
            ah = _mm(ar, st, _NT, pa=2, pb=2)
            x = ah[:rows_s] + _mm(a_ak, v_s, pa=2, pb=2)
            tinv = eye + a_ab
            p = a_ab
            step = 2
            while step < ch:
                p = _mm(p, p, pa=2, pb=2)
                tinv = tinv + _mm(tinv, p, pa=2, pb=2)
                step *= 2
            u_s = _mm(tinv, x, pa=2, pb=2)
            y_s = ah[rows_s:] + _mm(a_rb, u_s, pa=2, pb=2) + _mm(a_rk, v_s, pa=2, pb=2)
            y = y_s[0:ch]
            for h in range(1, gh):
                y = y + y_s[h * ch:(h + 1) * ch]
            out_ref[rs, ls] = y
            st_ref[gi] = (jnp.exp(g_last) * st + _mm(u_s.T, b_e, pa=2, pb=2)
                          + _mm(v_s.T, k_e, pa=2, pb=2))

    y = out_ref[...]
    inv_n = 1.0 / RWKV_HEAD
    mu = _mm(y, ones_bd, pa=3) * inv_n
    yc = y - mu
    var = _mm(yc * yc, ones_bd, pa=3) * inv_n
    yn = yc * lax.rsqrt(var + RWKV_GN_EPS) * ng_ref[...]
    out_ref[...] = (yn + bonus) * g


def _rwkv(z, bsz, mu_r, mu_k, mu_v, mu_t, w0, w_up, a0, a_up, g_up, k_k, k_a, r_k, norm_g):
    n = z.shape[0]
    lb = TIME_BLOCK
    nblk = n // bsz // lb
    row = lambda b, c: b * nblk + c
    full = lambda *shape: pl.BlockSpec(shape, lambda b, c: (0,) * len(shape))
    vec = full(1, WIDTH)
    return pl.pallas_call(
        _rwkv_kernel,
        grid=(bsz, nblk),
        in_specs=[
            pl.BlockSpec((lb, WIDTH), lambda b, c: (row(b, c), ZC_DR // WIDTH)),
            pl.BlockSpec((lb, WIDTH), lambda b, c: (row(b, c), ZC_DK // WIDTH)),
            pl.BlockSpec((lb, WIDTH), lambda b, c: (row(b, c), ZC_DV // WIDTH)),
            pl.BlockSpec((lb, RWKV_TAIL), lambda b, c: (row(b, c), ZC_DT // RWKV_TAIL)),
            vec, vec, vec, full(1, RWKV_TAIL),
            vec, full(RWKV_TAIL, WIDTH), vec, full(RWKV_TAIL, WIDTH), full(RWKV_TAIL, WIDTH),
            vec, vec, vec, vec,
        ],
        out_specs=pl.BlockSpec((lb, WIDTH), lambda b, c: (row(b, c), 0)),
        out_shape=jax.ShapeDtypeStruct((n, WIDTH), F32),
        scratch_shapes=[
            pltpu.VMEM((8, WIDTH), F32), pltpu.VMEM((8, WIDTH), F32), pltpu.VMEM((8, WIDTH), F32),
            pltpu.VMEM((8, RWKV_TAIL), F32),
            pltpu.VMEM((WIDTH // RWKV_GROUP_LANES, RWKV_GROUP_LANES, RWKV_GROUP_LANES), F32),
        ],
        compiler_params=pltpu.CompilerParams(
            dimension_semantics=("parallel", "arbitrary"), vmem_limit_bytes=VMEM_LIMIT),
        name="rwkv",
    )(z, z, z, z, mu_r, mu_k, mu_v, mu_t, w0, w_up, a0, a_up, g_up, k_k, k_a, r_k, norm_g)


def _merge_kernel(x_ref, ya_ref, zb_ref, yc_ref, yd_ref, g0_ref, g1_ref, g2_ref, g3_ref, gb_ref,
                  pa_ref, w1_ref, w2_ref, pc_ref, pd_ref, wo_ref, out_ref):
    d = D_MODEL

    def gate(g_ref, i):
        return _sigmoid(g_ref[...] + gb_ref[:, i * d:(i + 1) * d])

    zb = zb_ref[...]
    merged = gate(g0_ref, 0) * _mm(ya_ref[...], pa_ref[...])
    merged = merged + gate(g1_ref, 1) * (_mm(zb, w1_ref[...]) * _sigmoid(_mm(zb, w2_ref[...])))
    merged = merged + gate(g2_ref, 2) * _mm(yc_ref[...], pc_ref[...])
    merged = merged + gate(g3_ref, 3) * _mm(yd_ref[...], pd_ref[...])
    out_ref[...] = x_ref[...] + _mm(merged, wo_ref[...])


def _merge(x, z, ya, zb, yc, yd, gate_bias, pa, w1, w2, pc, pd, wo, *, tm=256):
    n, d = x.shape
    full = lambda *shape: pl.BlockSpec(shape, lambda i: (0,) * len(shape))
    tile = lambda w: pl.BlockSpec((tm, w), lambda i: (i, 0))
    gate = lambda b: pl.BlockSpec((tm, d), lambda i: (i, ZC_GATE // d + b))
    return pl.pallas_call(
        _merge_kernel,
        grid=(n // tm,),
        in_specs=[tile(d), tile(WIDTH), tile(WIDTH), tile(WIDTH), tile(WIDTH),
                  gate(0), gate(1), gate(2), gate(3), full(1, N_BRANCH * d),
                  full(WIDTH, d), full(WIDTH, d), full(WIDTH, d), full(WIDTH, d), full(WIDTH, d),
                  full(d, d)],
        out_specs=tile(d),
        out_shape=jax.ShapeDtypeStruct((n, d), F32),
        compiler_params=pltpu.CompilerParams(
            dimension_semantics=("parallel",), vmem_limit_bytes=VMEM_LIMIT),
        name="merge",
    )(x, ya, zb, yc, yd, z, z, z, z, gate_bias, pa, w1, w2, pc, pd, wo)


def _xattn_kernel(x_ref, g_ref, wq_ref, kv_ref, wo_ref, out_ref):
    d = D_MODEL
    hd = XATTN_HEAD_DIM
    x = x_ref[...]
    q = _mm(_rms(x, g_ref[...]), wq_ref[...])
    kv = kv_ref[...]
    outs = []
    for h in range(XATTN_HEADS):
        sl = slice(h * hd, (h + 1) * hd)
        s = _mm(q[:, sl], kv[:, sl], _NT) * hd ** -0.5
        s = s - jnp.max(s, axis=-1, keepdims=True)
        p = jnp.exp(s)
        p = p / jnp.sum(p, axis=-1, keepdims=True)
        outs.append(_mm(p, kv[:, d + h * hd:d + (h + 1) * hd]))
    o = jnp.concatenate(outs, axis=1)
    out_ref[...] = x + _mm(o, wo_ref[...])


def _xattn(x, bsz, gain, wq, kv, wo, *, tm=512):
    n, d = x.shape
    nblk = n // bsz // tm
    m_len = kv.shape[0] // bsz
    return pl.pallas_call(
        _xattn_kernel,
        grid=(bsz, nblk),
        in_specs=[
            pl.BlockSpec((tm, d), lambda b, c: (b * nblk + c, 0)),
            pl.BlockSpec((1, d), lambda b, c: (0, 0)),
            pl.BlockSpec((d, d), lambda b, c: (0, 0)),
            pl.BlockSpec((m_len, 2 * d), lambda b, c: (b, 0)),
            pl.BlockSpec((d, d), lambda b, c: (0, 0)),
        ],
        out_specs=pl.BlockSpec((tm, d), lambda b, c: (b * nblk + c, 0)),
        out_shape=jax.ShapeDtypeStruct((n, d), F32),
        compiler_params=pltpu.CompilerParams(
            dimension_semantics=("parallel", "parallel"), vmem_limit_bytes=VMEM_LIMIT),
        name="xattn",
    )(x, gain, wq, kv, wo)


def _regroup_w_in(w):
    o = np.cumsum((0, WIDTH, WIDTH, MLSTM_HEADS, MLSTM_HEADS, WIDTH, GLA_KEY_WIDTH, GLA_KEY_WIDTH, WIDTH, WIDTH,
                   GLA_GATE_RANK, 3 * WIDTH + RWKV_TAIL, N_BRANCH * D_MODEL))
    a_u, a_o, a_i, a_f, b_u, c_q, c_k, c_v, c_g, c_a, d_z, gate = (w[:, o[i]:o[i + 1]] for i in range(12))
    d_r, d_k, d_v, d_t = (d_z[:, :WIDTH], d_z[:, WIDTH:2 * WIDTH], d_z[:, 2 * WIDTH:3 * WIDTH], d_z[:, 3 * WIDTH:])
    small = jnp.concatenate([a_i, a_f, c_a], axis=1)
    cols = jnp.concatenate([gate, a_u, a_o, b_u, c_v, c_g, d_r, d_k, d_v, c_q, c_k, d_t, small], axis=1)
    return jnp.pad(cols, ((0, 0), (0, Z_COLS - cols.shape[1]))).astype(BF16)


def _s5_discretise(a_re, a_im, log_step, b_re, b_im, c_re, c_im, lb):
    step = jnp.exp(log_step)[:, None]
    lam_re = jnp.minimum(a_re, -S5_MIN_NEG)
    lam_im = a_im
    mag = jnp.exp(lam_re * step)
    bar_re = mag * jnp.cos(lam_im * step)
    bar_im = mag * jnp.sin(lam_im * step)
    denom = lam_re * lam_re + lam_im * lam_im
    coef_re = ((bar_re - 1.0) * lam_re + bar_im * lam_im) / denom
    coef_im = (bar_im * lam_re - (bar_re - 1.0) * lam_im) / denom
    bb_re = coef_re[..., None] * b_re - coef_im[..., None] * b_im
    bb_im = coef_re[..., None] * b_im + coef_im[..., None] * b_re
    gpb = S5_GROUPS // S5_BLOCKS
    eye = jnp.eye(gpb, dtype=F32)

    def in_block(bb):
        bb = bb.reshape(S5_BLOCKS, gpb, S5_STATE, S5_GROUP)
        return jnp.einsum('jgpc,gh->jgchp', bb, eye).reshape(S5_BLOCKS, gpb * S5_GROUP, gpb * S5_STATE)

    def out_block(cc):
        cc = cc.reshape(S5_BLOCKS, gpb, S5_GROUP, S5_STATE)
        return jnp.einsum('jgcp,gh->jgphc', cc, eye).reshape(S5_BLOCKS, gpb * S5_STATE, gpb * S5_GROUP)

    b_mat = jnp.concatenate([in_block(bb_re), in_block(bb_im)], axis=2)
    c_mat = jnp.concatenate([out_block(c_re), -out_block(c_im)], axis=1)
    t = jnp.arange(1, lb + 1, dtype=F32)[:, None, None]
    mag_t = jnp.exp(lam_re * step * t)
    ang_t = lam_im * step * t
    pow_re = (mag_t * jnp.cos(ang_t)).reshape(lb, S5_GROUPS * S5_STATE)
    pow_im = (mag_t * jnp.sin(ang_t)).reshape(lb, S5_GROUPS * S5_STATE)
    return b_mat, c_mat, pow_re, pow_im


def _pad_rows(w, start, total):
    return jnp.pad(w, ((start, total - start - w.shape[0]), (0, 0)))


def kernel(x, mem, ffn1_norm, ffn1_w_gate, ffn1_w_up, ffn1_w_down, mix_norm, w_in, gate_bias, mlstm_conv, mlstm_wq, mlstm_wk, mlstm_wv, mlstm_b_i, mlstm_b_f, mlstm_norm, mlstm_proj, s5_a_re, s5_a_im, s5_log_step, s5_b_re, s5_b_im, s5_c_re, s5_c_im, s5_d, s5_glu_w1, s5_glu_w2, gla_a_up, gla_a_bias, gla_norm, gla_proj, rwkv_mu, rwkv_w0, rwkv_w_up, rwkv_a0, rwkv_a_up, rwkv_g_up, rwkv_k_k, rwkv_k_a, rwkv_r_k, rwkv_norm, rwkv_proj, w_out, xattn_norm, mem_norm, xattn_wq, xattn_wk, xattn_wv, xattn_wo, ffn2_norm, ffn2_w_gate, ffn2_w_up, ffn2_w_down, final_norm):
    bsz, t, d = x.shape
    depth = w_in.shape[0]
    assert d == D_MODEL and t % TIME_BLOCK == 0
    xs = x.reshape(bsz * t, d)
    mems = mem.reshape(bsz * mem.shape[1], d)
    bf = lambda w: w.astype(BF16)
    rowv = lambda v: v.reshape(1, -1)
    fin = rowv(final_norm)

    for l in range(depth):
        xs = _ffn(xs, rowv(ffn1_norm[l]), bf(ffn1_w_gate[l]), bf(ffn1_w_up[l]), bf(ffn1_w_down[l]), fin,
                  final=False)

        z = _norm_matmul(xs, rowv(mix_norm[l]), _regroup_w_in(w_in[l]), tm=1024, tn=512, name="in_proj")

        gate_b = jnp.pad(jnp.concatenate([mlstm_b_i[l], mlstm_b_f[l]]), (0, SMALL - 2 * MLSTM_HEADS))
        y_a = _mlstm(z, bsz, mlstm_conv[l], bf(mlstm_wq[l]), bf(mlstm_wk[l]), bf(mlstm_wv[l]),
                     gate_b.reshape(1, SMALL), gate_b.reshape(SMALL, 1), rowv(mlstm_norm[l]))

        b_mat, c_mat, pow_re, pow_im = _s5_discretise(
            s5_a_re[l], s5_a_im[l], s5_log_step[l], s5_b_re[l], s5_b_im[l], s5_c_re[l], s5_c_im[l], TIME_BLOCK)
        z_b = _s5(z, bsz, b_mat, c_mat, pow_re, pow_im, rowv(s5_d[l]))

        y_c = _gla(z, bsz, _pad_rows(gla_a_up[l], 2 * MLSTM_HEADS, SMALL), rowv(gla_a_bias[l]), rowv(gla_norm[l]))

        mu = rwkv_mu[l]
        y_d = _rwkv(z, bsz, rowv(mu[:WIDTH]), rowv(mu[WIDTH:2 * WIDTH]), rowv(mu[2 * WIDTH:3 * WIDTH]),
                    rowv(mu[3 * WIDTH:]), rowv(rwkv_w0[l]), bf(_pad_rows(rwkv_w_up[l], 0, RWKV_TAIL)),
                    rowv(rwkv_a0[l]), bf(_pad_rows(rwkv_a_up[l], RWKV_DECAY_RANK, RWKV_TAIL)),
                    bf(_pad_rows(rwkv_g_up[l], RWKV_DECAY_RANK + RWKV_ICLR_RANK, RWKV_TAIL)),
                    rowv(rwkv_k_k[l]), rowv(rwkv_k_a[l]), rowv(rwkv_r_k[l]), rowv(rwkv_norm[l]))

        xs = _merge(xs, z, y_a, z_b, y_c, y_d, rowv(gate_bias[l]), bf(mlstm_proj[l]), bf(s5_glu_w1[l]),
                    bf(s5_glu_w2[l]), bf(gla_proj[l]), bf(rwkv_proj[l]), bf(w_out[l]))

        kv = _norm_matmul(mems, rowv(mem_norm[l]), bf(jnp.concatenate([xattn_wk[l], xattn_wv[l]], axis=1)),
                          tm=mems.shape[0], tn=512, name="mem_kv")
        xs = _xattn(xs, bsz, rowv(xattn_norm[l]), bf(xattn_wq[l]), kv, bf(xattn_wo[l]))

        xs = _ffn(xs, rowv(ffn2_norm[l]), bf(ffn2_w_gate[l]), bf(ffn2_w_up[l]), bf(ffn2_w_down[l]), fin,
                  final=(l == depth - 1))
    return xs.reshape(bsz, t, d)
```

```python
import functools
import math

import jax
import jax.numpy as jnp
import numpy as np
from jax import lax
from jax.experimental import pallas as pl
from jax.experimental.pallas import tpu as pltpu

F32 = jnp.float32
BF16 = jnp.bfloat16

D_MODEL = 1024
D_FF = 11 * D_MODEL // 4
NORM_EPS = 1e-6
FFN_HALF = 0.5
N_BRANCH = 4
WIDTH = D_MODEL // 2

MLSTM_HEADS = 4
MLSTM_HEAD_DIM = WIDTH // MLSTM_HEADS
MLSTM_CONV = 4

S5_GROUP = 16
S5_GROUPS = WIDTH // S5_GROUP
S5_STATE = 64
S5_MIN_NEG = 1e-4
S5_HALF = WIDTH // 2
S5_BLOCK_GROUPS = 8
S5_BLOCK_CH = S5_BLOCK_GROUPS * S5_GROUP
S5_BLOCK_STATES = S5_BLOCK_GROUPS * S5_STATE
S5_STEPS = 64
SUBLANES = 8

GLA_HEADS = 4
GLA_KEY_WIDTH = WIDTH // 2
GLA_HEAD_K = GLA_KEY_WIDTH // GLA_HEADS
GLA_HEAD_V = WIDTH // GLA_HEADS
GLA_GATE_RANK = 16
GLA_GATE_TAU = 16.0
GLA_CHUNK = 64

RWKV_HEAD = 64
RWKV_HEADS = WIDTH // RWKV_HEAD
RWKV_DECAY_RANK = 64
RWKV_ICLR_RANK = 64
RWKV_GATE_RANK = 128
RWKV_TAIL = RWKV_DECAY_RANK + RWKV_ICLR_RANK + RWKV_GATE_RANK
RWKV_GN_EPS = 64e-5
RWKV_CHUNK = 64
RWKV_GROUP_LANES = 256
RWKV_GROUP_HEADS = RWKV_GROUP_LANES // RWKV_HEAD

XATTN_HEADS = 4
XATTN_HEAD_DIM = D_MODEL // XATTN_HEADS

ZC_GATE = 0
ZC_AU = 4096
ZC_AO = 4608
ZC_BU = 5120
ZC_CV = 5632
ZC_CG = 6144
ZC_DR = 6656
ZC_DK = 7168
ZC_DV = 7680
ZC_CQ = 8192
ZC_CK = 8448
ZC_DT = 8704
ZC_SM = 8960
Z_COLS = 9216
SMALL = 128

TIME_BLOCK = 256
VMEM_LIMIT = 56 * 1024 * 1024

_NN = (((1,), (0,)), ((), ()))
_NT = (((1,), (1,)), ((), ()))


def _split(x, n):
    parts = []
    rest = x
    for i in range(n):
        p = rest.astype(BF16)
        parts.append(p)
        if i + 1 < n:
            rest = rest - p.astype(F32)
    return parts


def _mm(a, b, dn=_NN, pa=1, pb=1):
    ap, bp = _split(a, pa), _split(b, pb)
    order = max(pa, pb)
    out = None
    for i, x in enumerate(ap):
        for j, y in enumerate(bp):
            if i + j < order:
                t = lax.dot_general(x, y, dn, preferred_element_type=F32)
                out = t if out is None else out + t
    return out


def _rms(x, g):
    return x * lax.rsqrt(jnp.mean(x * x, axis=-1, keepdims=True) + NORM_EPS) * g


def _sigmoid(x):
    return 1.0 / (1.0 + jnp.exp(-x))


def _log_sigmoid(x):
    return jnp.minimum(x, 0.0) - jnp.log(1.0 + jnp.exp(-jnp.abs(x)))


def _iota(shape, dim):
    return lax.broadcasted_iota(jnp.int32, shape, dim)


def _tri(n, block, upper=False, strict=False):
    r, c = _iota((n, n), 0), _iota((n, n), 1)
    same = (r // block) == (c // block)
    if upper:
        keep = (r < c) if strict else (r <= c)
    else:
        keep = (r > c) if strict else (r >= c)
    return jnp.where(same & keep, 1.0, 0.0).astype(BF16)


def _shift_rows(prev_ref, x, k):
    ext = jnp.concatenate([prev_ref[...], x], axis=0)
    return pltpu.roll(ext, k, 0)[prev_ref.shape[0]:]


def _layer_spec(layer, *shape):
    return pl.BlockSpec((None,) + shape, lambda *_: (layer,) + (0,) * len(shape))


def _ffn_kernel(x_ref, g_ref, wg_ref, wu_ref, wd_ref, fg_ref, o_ref, xn_ref, acc_ref, *, final):
    j = pl.program_id(1)

    @pl.when(j == 0)
    def _():
        xn_ref[...] = _rms(x_ref[...], g_ref[...]).astype(BF16)
        acc_ref[...] = jnp.zeros_like(acc_ref)

    xn = xn_ref[...]
    gate = _mm(xn, wg_ref[...])
    up = _mm(xn, wu_ref[...])
    acc_ref[...] += _mm(gate * _sigmoid(gate) * up, wd_ref[...])

    @pl.when(j == pl.num_programs(1) - 1)
    def _():
        y = x_ref[...] + FFN_HALF * acc_ref[...]
        if final:
            y = _rms(y, fg_ref[...])
        o_ref[...] = y


def _ffn(x, gain, wg, wu, wd, final_gain, *, layer, final, tm=1024, tf=256):
    n, d = x.shape
    f = wg.shape[2]
    return pl.pallas_call(
        functools.partial(_ffn_kernel, final=final),
        grid=(n // tm, f // tf),
        in_specs=[
            pl.BlockSpec((tm, d), lambda i, j: (i, 0)),
            _layer_spec(layer, 1, d),
            pl.BlockSpec((None, d, tf), lambda i, j: (layer, 0, j)),
            pl.BlockSpec((None, d, tf), lambda i, j: (layer, 0, j)),
            pl.BlockSpec((None, tf, d), lambda i, j: (layer, j, 0)),
            pl.BlockSpec((1, d), lambda i, j: (0, 0)),
        ],
        out_specs=pl.BlockSpec((tm, d), lambda i, j: (i, 0)),
        out_shape=jax.ShapeDtypeStruct((n, d), F32),
        scratch_shapes=[pltpu.VMEM((tm, d), BF16), pltpu.VMEM((tm, d), F32)],
        compiler_params=pltpu.CompilerParams(
            dimension_semantics=("parallel", "arbitrary"), vmem_limit_bytes=VMEM_LIMIT),
        name="ffn",
    )(x, gain, wg, wu, wd, final_gain)


def _norm_matmul_kernel(x_ref, g_ref, w_ref, o_ref, xn_ref):
    @pl.when(pl.program_id(1) == 0)
    def _():
        xn_ref[...] = _rms(x_ref[...], g_ref[...]).astype(BF16)

    o_ref[...] = _mm(xn_ref[...], w_ref[...])


def _norm_matmul(x, gain, w, *, layer, tm, tn, name):
    n, d = x.shape
    m = w.shape[2]
    return pl.pallas_call(
        _norm_matmul_kernel,
        grid=(n // tm, m // tn),
        in_specs=[
            pl.BlockSpec((tm, d), lambda i, j: (i, 0)),
            _layer_spec(layer, 1, d),
            pl.BlockSpec((None, d, tn), lambda i, j: (layer, 0, j)),
        ],
        out_specs=pl.BlockSpec((tm, tn), lambda i, j: (i, j)),
        out_shape=jax.ShapeDtypeStruct((n, m), F32),
        scratch_shapes=[pltpu.VMEM((tm, d), BF16)],
        compiler_params=pltpu.CompilerParams(
            dimension_semantics=("parallel", "arbitrary"), vmem_limit_bytes=VMEM_LIMIT),
        name=name,
    )(x, gain, w)


def _mlstm_kernel(u_ref, o_ref, sm_ref, conv_ref, wq_ref, wk_ref, wv_ref, brow_ref, bcol_ref, ng_ref,
                  out_ref, uprev_ref, c_ref, n_ref, m_ref):
    lb = u_ref.shape[0]
    e = MLSTM_HEAD_DIM

    @pl.when(pl.program_id(1) == 0)
    def _():
        uprev_ref[...] = jnp.zeros_like(uprev_ref)
        c_ref[...] = jnp.zeros_like(c_ref)
        n_ref[...] = jnp.zeros_like(n_ref)
        m_ref[...] = jnp.zeros_like(m_ref)

    u = u_ref[...]
    conv = None
    for j in range(MLSTM_CONV):
        k = MLSTM_CONV - 1 - j
        term = conv_ref[j:j + 1, :] * (_shift_rows(uprev_ref, u, k) if k else u)
        conv = term if conv is None else conv + term
    uprev_ref[...] = u[lb - uprev_ref.shape[0]:]
    uc = conv * _sigmoid(conv)

    sm = sm_ref[...]
    pre = sm + brow_ref[...]
    log_f = _log_sigmoid(pre)
    pre_t = sm.T[0:8, :] + bcol_ref[0:8, :]
    log_f_t = _log_sigmoid(pre_t)
    b_cols = _mm(_tri(lb, lb), log_f, pb=3)
    b_rows = _mm(log_f_t, _tri(lb, lb, upper=True), pa=3)

    causal = _iota((lb, lb), 0) >= _iota((lb, lb), 1)
    for h in range(MLSTM_HEADS):
        sl = slice(h * e, (h + 1) * e)
        q = _mm(uc[:, sl], wq_ref[h])
        k = _mm(uc[:, sl], wk_ref[h]) * e ** -0.5
        v = _mm(u[:, sl], wv_ref[h])
        bc = b_cols[:, MLSTM_HEADS + h:MLSTM_HEADS + h + 1]
        br = b_rows[MLSTM_HEADS + h:MLSTM_HEADS + h + 1, :]
        i_row = pre_t[h:h + 1, :]
        i_col = pre[:, h:h + 1]
        c_st = c_ref[h]
        n_st = n_ref[h:h + 1, :]
        m_st = m_ref[h:h + 1, 0:1]

        dmat = jnp.where(causal, bc - br + i_row, -jnp.inf)
        m_inter = bc + m_st
        m_row = jnp.maximum(m_inter, jnp.max(dmat, axis=-1, keepdims=True))
        w = jnp.exp(dmat - m_row) * _mm(q, k, _NT)
        inter = jnp.exp(m_inter - m_row)
        num = inter * _mm(q, c_st) + _mm(w, v)
        den = inter * jnp.sum(q * n_st, axis=-1, keepdims=True) + jnp.sum(w, axis=-1, keepdims=True)
        hh = num / jnp.maximum(jnp.abs(den), jnp.exp(-m_row))

        b_last = bc[lb - 1:lb, :]
        e_col = b_last - bc + i_col
        m_new = jnp.maximum(b_last + m_st, jnp.max(e_col, axis=0, keepdims=True))
        decay = jnp.exp(b_last + m_st - m_new)
        wk = jnp.exp(e_col - m_new) * k
        c_ref[h] = decay * c_st + _mm(wk.T, v)
        n_ref[h:h + 1, :] = decay * n_st + jnp.sum(wk, axis=0, keepdims=True)
        m_ref[h:h + 1, :] = jnp.broadcast_to(m_new, (1, m_ref.shape[1]))

        hn = hh * lax.rsqrt(jnp.mean(hh * hh, axis=-1, keepdims=True) + NORM_EPS) * ng_ref[:, sl]
        out_ref[:, sl] = hn * _sigmoid(o_ref[:, sl])


def _mlstm(z, bsz, layer, conv_w, wq, wk, wv, b_row, b_col, norm_g):
    n = z.shape[0]
    lb = TIME_BLOCK
    nblk = n // bsz // lb
    row = lambda b, c: b * nblk + c
    hd = MLSTM_HEAD_DIM
    return pl.pallas_call(
        _mlstm_kernel,
        grid=(bsz, nblk),
        in_specs=[
            pl.BlockSpec((lb, WIDTH), lambda b, c: (row(b, c), ZC_AU // WIDTH)),
            pl.BlockSpec((lb, WIDTH), lambda b, c: (row(b, c), ZC_AO // WIDTH)),
            pl.BlockSpec((lb, SMALL), lambda b, c: (row(b, c), ZC_SM // SMALL)),
            _layer_spec(layer, MLSTM_CONV, WIDTH),
            _layer_spec(layer, MLSTM_HEADS, hd, hd),
            _layer_spec(layer, MLSTM_HEADS, hd, hd),
            _layer_spec(layer, MLSTM_HEADS, hd, hd),
            _layer_spec(layer, 1, SMALL),
            _layer_spec(layer, SMALL, 1),
            _layer_spec(layer, 1, WIDTH),
        ],
        out_specs=pl.BlockSpec((lb, WIDTH), lambda b, c: (row(b, c), 0)),
        out_shape=jax.ShapeDtypeStruct((n, WIDTH), F32),
        scratch_shapes=[
            pltpu.VMEM((8, WIDTH), F32),
            pltpu.VMEM((MLSTM_HEADS, hd, hd), F32),
            pltpu.VMEM((8, hd), F32),
            pltpu.VMEM((8, hd), F32),
        ],
        compiler_params=pltpu.CompilerParams(
            dimension_semantics=("parallel", "arbitrary"), vmem_limit_bytes=VMEM_LIMIT),
        name="mlstm",
    )(z, z, z, conv_w, wq, wk, wv, b_row, b_col, norm_g)


def _s5_kernel(u_ref, win_ref, wout_ref, lre_ref, lim_ref, d_ref, out_ref, x_ref, c_ref, *, bsz):
    rows = u_ref.shape[0]
    ns = S5_BLOCK_STATES
    cw = S5_BLOCK_CH
    nblk = S5_HALF // cw

    @pl.when(pl.program_id(0) == 0)
    def _():
        c_ref[...] = jnp.zeros_like(c_ref)

    u = u_ref[...]
    upper = (_iota((rows, 1), 0) % SUBLANES) >= bsz
    for j in range(nblk):
        uj = u[:, j * cw:(j + 1) * cw]
        uext = jnp.concatenate([jnp.where(upper, 0.0, uj), jnp.where(upper, uj, 0.0)], axis=1)
        x_ref[:, j * 2 * ns:(j + 1) * 2 * ns] = _mm(uext, win_ref[j])

    lre, lim = lre_ref[...], lim_ref[...]

    def step(t, carry):
        r0 = pl.multiple_of(t * SUBLANES, SUBLANES)
        new = []
        for j in range(nblk):
            sr, si = carry[2 * j], carry[2 * j + 1]
            lr, li = lre[:, j * ns:(j + 1) * ns], lim[:, j * ns:(j + 1) * ns]
            re_cols = slice(j * 2 * ns, j * 2 * ns + ns)
            im_cols = slice(j * 2 * ns + ns, (j + 1) * 2 * ns)
            nr = lr * sr - li * si + x_ref[pl.ds(r0, SUBLANES), re_cols]
            ni = lr * si + li * sr + x_ref[pl.ds(r0, SUBLANES), im_cols]
            x_ref[pl.ds(r0, SUBLANES), re_cols] = nr
            x_ref[pl.ds(r0, SUBLANES), im_cols] = ni
            new += [nr, ni]
        return tuple(new)

    init = tuple(c_ref[:, k * ns:(k + 1) * ns] for k in range(2 * nblk))
    last = lax.fori_loop(0, rows // SUBLANES, step, init, unroll=8)
    for k in range(2 * nblk):
        c_ref[:, k * ns:(k + 1) * ns] = last[k]

    for j in range(nblk):
        res = _mm(x_ref[:, j * 2 * ns:(j + 1) * 2 * ns], wout_ref[j])
        cols = slice(j * cw, (j + 1) * cw)
        y = jnp.where(upper, res[:, cw:], res[:, :cw]) + d_ref[:, cols] * u[:, cols]
        out_ref[:, cols] = 0.5 * y * (1.0 + jnp.tanh(math.sqrt(2.0 / math.pi) * (y + 0.044715 * y * y * y)))


def _s5(u_tm, bsz, layer, w_in, w_out, lam_re, lam_im, d_tab):
    n_rows = u_tm.shape[0]
    rows = S5_STEPS * SUBLANES
    nblk = S5_HALF // S5_BLOCK_CH
    nst = nblk * S5_BLOCK_STATES
    return pl.pallas_call(
        functools.partial(_s5_kernel, bsz=bsz),
        grid=(n_rows // rows,),
        in_specs=[
            pl.BlockSpec((rows, S5_HALF), lambda i: (i, 0)),
            _layer_spec(layer, nblk, 2 * S5_BLOCK_CH, 2 * S5_BLOCK_STATES),
            _layer_spec(layer, nblk, 2 * S5_BLOCK_STATES, 2 * S5_BLOCK_CH),
            _layer_spec(layer, SUBLANES, nst),
            _layer_spec(layer, SUBLANES, nst),
            _layer_spec(layer, rows, S5_HALF),
        ],
        out_specs=pl.BlockSpec((rows, S5_HALF), lambda i: (i, 0)),
        out_shape=jax.ShapeDtypeStruct((n_rows, S5_HALF), F32),
        scratch_shapes=[pltpu.VMEM((rows, 2 * nst), F32), pltpu.VMEM((SUBLANES, 2 * nst), F32)],
        compiler_params=pltpu.CompilerParams(
            dimension_semantics=("arbitrary",), vmem_limit_bytes=VMEM_LIMIT),
        name="s5",
    )(u_tm, w_in, w_out, lam_re, lam_im, d_tab)


def _gla_kernel(q_ref, k_ref, v_ref, g_ref, sm_ref, aup_ref, ab_ref, ng_ref, out_ref, st_ref):
    lb = q_ref.shape[0]
    ch = GLA_CHUNK
    kw = GLA_KEY_WIDTH

    @pl.when(pl.program_id(1) == 0)
    def _():
        st_ref[...] = jnp.zeros_like(st_ref)

    x = _mm(sm_ref[...], aup_ref[...], pa=2, pb=2) + ab_ref[...]
    log_a = _log_sigmoid(x) * (1.0 / GLA_GATE_TAU)
    b_all = _mm(_tri(lb, ch), log_a, pb=3)
    lane_head = _iota((1, kw), 1) // GLA_HEAD_K
    masks = [jnp.where(lane_head == h, 1.0, 0.0) for h in range(GLA_HEADS)]
    causal = _iota((ch, ch), 0) >= _iota((ch, ch), 1)
    mid = ch // 2 - 1

    for c in range(lb // ch):
        rs = slice(c * ch, (c + 1) * ch)
        b = b_all[rs]
        b_mid, b_last = b[mid:mid + 1, :], b[ch - 1:ch, :]
        q = q_ref[rs, :] * GLA_HEAD_K ** -0.5
        k = k_ref[rs, :]
        q_in = q * jnp.exp(b)
        q_at = q * jnp.exp(b - b_mid)
        k_at = (k * jnp.exp(b_mid - b)).astype(BF16)
        k_st = k * jnp.exp(b_last - b)
        dec = jnp.exp(b_last)
        for h in range(GLA_HEADS):
            vs = slice(h * GLA_HEAD_V, (h + 1) * GLA_HEAD_V)
            v = v_ref[rs, vs]
            st = st_ref[h]
            attn = jnp.where(causal, _mm(q_at * masks[h], k_at, _NT), 0.0)
            o = _mm(q_in * masks[h], st, _NT) + _mm(attn, v)
            st_ref[h] = dec * st + _mm(v.T, k_st * masks[h])
            on = o * lax.rsqrt(jnp.mean(o * o, axis=-1, keepdims=True) + NORM_EPS) * ng_ref[:, vs]
            g = g_ref[rs, vs]
            out_ref[rs, vs] = on * (g * _sigmoid(g))


def _gla(z, bsz, layer, a_up, a_bias, norm_g):
    n = z.shape[0]
    lb = TIME_BLOCK
    nblk = n // bsz // lb
    row = lambda b, c: b * nblk + c
    return pl.pallas_call(
        _gla_kernel,
        grid=(bsz, nblk),
        in_specs=[
            pl.BlockSpec((lb, GLA_KEY_WIDTH), lambda b, c: (row(b, c), ZC_CQ // GLA_KEY_WIDTH)),
            pl.BlockSpec((lb, GLA_KEY_WIDTH), lambda b, c: (row(b, c), ZC_CK // GLA_KEY_WIDTH)),
            pl.BlockSpec((lb, WIDTH), lambda b, c: (row(b, c), ZC_CV // WIDTH)),
            pl.BlockSpec((lb, WIDTH), lambda b, c: (row(b, c), ZC_CG // WIDTH)),
            pl.BlockSpec((lb, SMALL), lambda b, c: (row(b, c), ZC_SM // SMALL)),
            _layer_spec(layer, SMALL, GLA_KEY_WIDTH),
            _layer_spec(layer, 1, GLA_KEY_WIDTH),
            _layer_spec(layer, 1, WIDTH),
        ],
        out_specs=pl.BlockSpec((lb, WIDTH), lambda b, c: (row(b, c), 0)),
        out_shape=jax.ShapeDtypeStruct((n, WIDTH), F32),
        scratch_shapes=[pltpu.VMEM((GLA_HEADS, GLA_HEAD_V, GLA_KEY_WIDTH), F32)],
        compiler_params=pltpu.CompilerParams(
            dimension_semantics=("parallel", "arbitrary"), vmem_limit_bytes=VMEM_LIMIT),
        name="gla",
    )(z, z, z, z, z, a_up, a_bias, norm_g)


def _rwkv_kernel(r_ref, k_ref, v_ref, t_ref, mu_ref, mut_ref,
                 w0_ref, wup_ref, a0_ref, aup_ref, gup_ref, kk_ref, ka_ref, rk_ref, ng_ref,
                 out_ref, pr_ref, pk_ref, pv_ref, pt_ref, st_ref):
    lb = r_ref.shape[0]
    ch = RWKV_CHUNK
    gl = RWKV_GROUP_LANES
    gh = RWKV_GROUP_HEADS
    rows_s = gh * ch

    @pl.when(pl.program_id(1) == 0)
    def _():
        for ref in (pr_ref, pk_ref, pv_ref, pt_ref, st_ref):
            ref[...] = jnp.zeros_like(ref)

    def shift_mix(x_ref, prev_ref, mu):
        x = x_ref[...]
        xs = _shift_rows(prev_ref, x, 1)
        prev_ref[...] = x[lb - prev_ref.shape[0]:]
        return x + mu * (xs - x)

    r = shift_mix(r_ref, pr_ref, mu_ref[0:1, :])
    k = shift_mix(k_ref, pk_ref, mu_ref[1:2, :])
    v = shift_mix(v_ref, pv_ref, mu_ref[2:3, :])
    tl = shift_mix(t_ref, pt_ref, mut_ref[...])

    ones_bd = _tri(WIDTH, RWKV_HEAD) + _tri(WIDTH, RWKV_HEAD, upper=True, strict=True)
    w_log = _log_sigmoid(w0_ref[...] + _mm(jnp.tanh(tl), wup_ref[...])) - 0.5
    log_w = -jnp.exp(w_log)
    a = _sigmoid(a0_ref[...] + _mm(tl, aup_ref[...]))
    g = _mm(_sigmoid(tl), gup_ref[...])
    kk = k * kk_ref[...]
    kk = kk / jnp.maximum(jnp.sqrt(_mm(kk * kk, ones_bd, pa=3)), 1e-12)
    k_rep = k * (1.0 + (a - 1.0) * ka_ref[...])
    av = -kk
    bv = kk * a
    bonus = _mm(r * k_rep * rk_ref[...], ones_bd, pa=3) * v
    gc_all = _mm(_tri(lb, ch), log_w, pb=3)

    lane_head = _iota((1, gl), 1) // RWKV_HEAD
    masks = [jnp.where(lane_head == h, 1.0, 0.0).astype(BF16) for h in range(gh)]

    def stack(x):
        xb = x.astype(BF16)
        return jnp.concatenate([xb * m for m in masks], axis=0)

    t_row = _iota((rows_s, rows_s), 0) % ch
    t_col = _iota((rows_s, rows_s), 1) % ch
    strict = t_row > t_col
    incl = t_row >= t_col
    eye = jnp.where(_iota((rows_s, rows_s), 0) == _iota((rows_s, rows_s), 1), 1.0, 0.0)

    pairs = [(gi, c) for gi in range(WIDTH // gl) for c in range(lb // ch)]
    pre = {}
    for gi, c in pairs:
        ls = slice(gi * gl, (gi + 1) * gl)
        rs = slice(c * ch, (c + 1) * ch)
        gc = gc_all[rs, ls]
        g_last = gc[ch - 1:ch, :]
        e_inv = jnp.exp(-gc)
        e_end = jnp.exp(g_last - gc)
        r_t = stack(r[rs, ls] * jnp.exp(gc))
        a_t = stack(av[rs, ls] * jnp.exp(gc - log_w[rs, ls]))
        b_t = stack(bv[rs, ls] * e_inv)
        k_t = stack(k_rep[rs, ls] * e_inv)
        mm = _mm(jnp.concatenate([a_t, r_t], axis=0), jnp.concatenate([b_t, k_t], axis=0), _NT)
        pre[gi, c] = dict(
            r_t=r_t, a_t=a_t, v_s=stack(v[rs, ls]), dec=jnp.exp(g_last),
            b_e=stack(bv[rs, ls] * e_end), k_e=stack(k_rep[rs, ls] * e_end),
            a_ab=jnp.where(strict, mm[:rows_s, :rows_s], 0.0),
            a_ak=jnp.where(strict, mm[:rows_s, rows_s:], 0.0).astype(BF16),
            a_rb=jnp.where(incl, mm[rows_s:, :rows_s], 0.0).astype(BF16),
            a_rk=jnp.where(incl, mm[rows_s:, rows_s:], 0.0).astype(BF16))
    tinv = {key: eye + d["a_ab"] for key, d in pre.items()}
    power = {key: d["a_ab"].astype(BF16) for key, d in pre.items()}
    step = 2
    while step < ch:
        for key in pairs:
            power[key] = _mm(power[key], power[key]).astype(BF16)
        for key in pairs:
            tinv[key] = tinv[key] + _mm(tinv[key], power[key])
        step *= 2

    def finish_phase1(key):
        d = pre[key]
        tb = tinv[key].astype(BF16)
        d["a_hat"] = _mm(tb, d["a_t"]).astype(BF16)
        d["u0"] = _mm(tb, _mm(d["a_ak"], d["v_s"]))
        d["y0"] = _mm(d["a_rk"], d["v_s"])
        d["s0"] = _mm(d["v_s"].astype(F32).T, d["k_e"])

    groups = range(WIDTH // gl)
    state = [st_ref[gi] for gi in groups]

    def solve_u(gi, c):
        d = pre[gi, c]
        d["st"] = state[gi].astype(BF16)
        return _mm(d["a_hat"], d["st"], _NT) + d["u0"]

    def emit_and_advance(gi, c, u_s):
        d = pre[gi, c]
        y_s = _mm(d["r_t"], d["st"], _NT) + _mm(d["a_rb"], u_s) + d["y0"]
        y = y_s[0:ch]
        for h in range(1, gh):
            y = y + y_s[h * ch:(h + 1) * ch]
        out_ref[c * ch:(c + 1) * ch, gi * gl:(gi + 1) * gl] = y
        state[gi] = d["dec"] * state[gi] + _mm(u_s.T, d["b_e"]) + d["s0"]

    n_chunks = lb // ch
    for c in range(n_chunks + 1):
        u_prev = {}
        for gi in groups:
            if c < n_chunks:
                finish_phase1((gi, c))
            if c > 0:
                u_prev[gi] = solve_u(gi, c - 1)
        if c > 0:
            for gi in groups:
                emit_and_advance(gi, c - 1, u_prev[gi])
    for gi in groups:
        st_ref[gi] = state[gi]

    y = out_ref[...]
    inv_n = 1.0 / RWKV_HEAD
    mu = _mm(y, ones_bd, pa=3) * inv_n
    yc = y - mu
    var = _mm(yc * yc, ones_bd, pa=3) * inv_n
    yn = yc * lax.rsqrt(var + RWKV_GN_EPS) * ng_ref[...]
    out_ref[...] = (yn + bonus) * g


def _rwkv(z, bsz, layer, mu_rkv, mu_t, w0, w_up, a0, a_up, g_up, k_k, k_a, r_k, norm_g):
    n = z.shape[0]
    lb = TIME_BLOCK
    nblk = n // bsz // lb
    row = lambda b, c: b * nblk + c
    vec = _layer_spec(layer, 1, WIDTH)
    mat = _layer_spec(layer, RWKV_TAIL, WIDTH)
    return pl.pallas_call(
        _rwkv_kernel,
        grid=(bsz, nblk),
        in_specs=[
            pl.BlockSpec((lb, WIDTH), lambda b, c: (row(b, c), ZC_DR // WIDTH)),
            pl.BlockSpec((lb, WIDTH), lambda b, c: (row(b, c), ZC_DK // WIDTH)),
            pl.BlockSpec((lb, WIDTH), lambda b, c: (row(b, c), ZC_DV // WIDTH)),
            pl.BlockSpec((lb, RWKV_TAIL), lambda b, c: (row(b, c), ZC_DT // RWKV_TAIL)),
            _layer_spec(layer, 3, WIDTH), _layer_spec(layer, 1, RWKV_TAIL),
            vec, mat, vec, mat, mat, vec, vec, vec, vec,
        ],
        out_specs=pl.BlockSpec((lb, WIDTH), lambda b, c: (row(b, c), 0)),
        out_shape=jax.ShapeDtypeStruct((n, WIDTH), F32),
        scratch_shapes=[
            pltpu.VMEM((8, WIDTH), F32), pltpu.VMEM((8, WIDTH), F32), pltpu.VMEM((8, WIDTH), F32),
            pltpu.VMEM((8, RWKV_TAIL), F32),
            pltpu.VMEM((WIDTH // RWKV_GROUP_LANES, RWKV_GROUP_LANES, RWKV_GROUP_LANES), F32),
        ],
        compiler_params=pltpu.CompilerParams(
            dimension_semantics=("parallel", "arbitrary"), vmem_limit_bytes=VMEM_LIMIT),
        name="rwkv",
    )(z, z, z, z, mu_rkv, mu_t, w0, w_up, a0, a_up, g_up, k_k, k_a, r_k, norm_g)


def _merge_kernel(x_ref, ya_ref, zb_ref, yc_ref, yd_ref, g0_ref, g1_ref, g2_ref, g3_ref, gb_ref,
                  pa_ref, w1_ref, w2_ref, pc_ref, pd_ref, wo_ref, out_ref):
    d = D_MODEL

    def gate(g_ref, i):
        return _sigmoid(g_ref[...] + gb_ref[:, i * d:(i + 1) * d])

    zb = zb_ref[...]
    merged = gate(g0_ref, 0) * _mm(ya_ref[...], pa_ref[...])
    merged = merged + gate(g1_ref, 1) * (_mm(zb, w1_ref[...]) * _sigmoid(_mm(zb, w2_ref[...])))
    merged = merged + gate(g2_ref, 2) * _mm(yc_ref[...], pc_ref[...])
    merged = merged + gate(g3_ref, 3) * _mm(yd_ref[...], pd_ref[...])
    out_ref[...] = x_ref[...] + _mm(merged, wo_ref[...])


def _merge(x, z, ya, zb, yc, yd, layer, gate_bias, pa, w1, w2, pc, pd, wo, *, tm=256):
    n, d = x.shape
    tile = lambda w: pl.BlockSpec((tm, w), lambda i: (i, 0))
    gate = lambda b: pl.BlockSpec((tm, d), lambda i: (i, ZC_GATE // d + b))
    proj = _layer_spec(layer, WIDTH, d)
    return pl.pallas_call(
        _merge_kernel,
        grid=(n // tm,),
        in_specs=[tile(d), tile(WIDTH), tile(WIDTH), tile(WIDTH), tile(WIDTH),
                  gate(0), gate(1), gate(2), gate(3), _layer_spec(layer, 1, N_BRANCH * d),
                  proj, proj, proj, proj, proj, _layer_spec(layer, d, d)],
        out_specs=tile(d),
        out_shape=jax.ShapeDtypeStruct((n, d), F32),
        compiler_params=pltpu.CompilerParams(
            dimension_semantics=("parallel",), vmem_limit_bytes=VMEM_LIMIT),
        name="merge",
    )(x, ya, zb, yc, yd, z, z, z, z, gate_bias, pa, w1, w2, pc, pd, wo)


def _xattn_kernel(x_ref, g_ref, wq_ref, kv_ref, wo_ref, out_ref):
    d = D_MODEL
    hd = XATTN_HEAD_DIM
    x = x_ref[...]
    q = _mm(_rms(x, g_ref[...]), wq_ref[...])
    kv = kv_ref[...]
    outs = []
    for h in range(XATTN_HEADS):
        sl = slice(h * hd, (h + 1) * hd)
        s = _mm(q[:, sl], kv[:, sl], _NT) * hd ** -0.5
        s = s - jnp.max(s, axis=-1, keepdims=True)
        p = jnp.exp(s)
        p = p / jnp.sum(p, axis=-1, keepdims=True)
        outs.append(_mm(p, kv[:, d + h * hd:d + (h + 1) * hd]))
    o = jnp.concatenate(outs, axis=1)
    out_ref[...] = x + _mm(o, wo_ref[...])


def _xattn(x, bsz, layer, gain, wq, kv, wo, *, tm=512):
    n, d = x.shape
    nblk = n // bsz // tm
    m_len = kv.shape[0] // bsz
    return pl.pallas_call(
        _xattn_kernel,
        grid=(bsz, nblk),
        in_specs=[
            pl.BlockSpec((tm, d), lambda b, c: (b * nblk + c, 0)),
            _layer_spec(layer, 1, d),
            _layer_spec(layer, d, d),
            pl.BlockSpec((m_len, 2 * d), lambda b, c: (b, 0)),
            _layer_spec(layer, d, d),
        ],
        out_specs=pl.BlockSpec((tm, d), lambda b, c: (b * nblk + c, 0)),
        out_shape=jax.ShapeDtypeStruct((n, d), F32),
        compiler_params=pltpu.CompilerParams(
            dimension_semantics=("parallel", "parallel"), vmem_limit_bytes=VMEM_LIMIT),
        name="xattn",
    )(x, gain, wq, kv, wo)


def _regroup_w_in(w):
    o = np.cumsum((0, WIDTH, WIDTH, MLSTM_HEADS, MLSTM_HEADS, WIDTH, GLA_KEY_WIDTH, GLA_KEY_WIDTH, WIDTH, WIDTH,
                   GLA_GATE_RANK, 3 * WIDTH + RWKV_TAIL, N_BRANCH * D_MODEL))
    a_u, a_o, a_i, a_f, b_u, c_q, c_k, c_v, c_g, c_a, d_z, gate = (w[..., o[i]:o[i + 1]] for i in range(12))
    d_r, d_k, d_v, d_t = (d_z[..., :WIDTH], d_z[..., WIDTH:2 * WIDTH], d_z[..., 2 * WIDTH:3 * WIDTH],
                          d_z[..., 3 * WIDTH:])
    used = Z_COLS - 2 * SMALL + 2 * MLSTM_HEADS + GLA_GATE_RANK
    pad = jnp.zeros(w.shape[:-1] + (Z_COLS - used,), w.dtype)
    cols = jnp.concatenate([gate, a_u, a_o, b_u, c_v, c_g, d_r, d_k, d_v, c_q, c_k, d_t, a_i, a_f, c_a, pad], axis=-1)
    return cols.astype(BF16)


def _s5_tables(a_re, a_im, log_step, b_re, b_im, c_re, c_im, d_skip, bsz):
    depth = a_re.shape[0]
    step = jnp.exp(log_step)[..., None]
    lam_re = jnp.minimum(a_re, -S5_MIN_NEG)
    lam_im = a_im
    mag = jnp.exp(lam_re * step)
    bar_re = mag * jnp.cos(lam_im * step)
    bar_im = mag * jnp.sin(lam_im * step)
    denom = lam_re * lam_re + lam_im * lam_im
    coef_re = ((bar_re - 1.0) * lam_re + bar_im * lam_im) / denom
    coef_im = (bar_im * lam_re - (bar_re - 1.0) * lam_im) / denom
    bb_re = coef_re[..., None] * b_re - coef_im[..., None] * b_im
    bb_im = coef_re[..., None] * b_im + coef_im[..., None] * b_re
    nb = S5_HALF // S5_BLOCK_CH
    gpb = S5_BLOCK_GROUPS
    eye = jnp.eye(gpb, dtype=F32)

    def in_block(bb):
        bb = bb.reshape(depth, 2, nb, gpb, S5_STATE, S5_GROUP)
        return jnp.einsum('lhjgpc,gk->lhjgckp', bb, eye).reshape(depth, 2, nb, S5_BLOCK_CH, S5_BLOCK_STATES)

    def out_block(cc):
        cc = cc.reshape(depth, 2, nb, gpb, S5_GROUP, S5_STATE)
        return jnp.einsum('lhjgcp,gk->lhjgpkc', cc, eye).reshape(depth, 2, nb, S5_BLOCK_STATES, S5_BLOCK_CH)

    w_in = jnp.concatenate([in_block(bb_re), in_block(bb_im)], axis=-1)
    w_in = w_in.transpose(0, 2, 1, 3, 4).reshape(depth, nb, 2 * S5_BLOCK_CH, 2 * S5_BLOCK_STATES)
    w_out = jnp.concatenate([out_block(c_re), -out_block(c_im)], axis=-2)
    w_out = w_out.transpose(0, 2, 3, 1, 4).reshape(depth, nb, 2 * S5_BLOCK_STATES, 2 * S5_BLOCK_CH)

    def rows(t, width):
        t = t.reshape(depth, 2, 1, width)
        return jnp.broadcast_to(t, (depth, 2, bsz, width)).reshape(depth, 2 * bsz, width)

    nst = nb * S5_BLOCK_STATES
    lam_re_rows = rows(bar_re.reshape(depth, 2 * nst), nst)
    lam_im_rows = rows(bar_im.reshape(depth, 2 * nst), nst)
    d_tab = jnp.tile(rows(d_skip, S5_HALF), (1, S5_STEPS, 1))
    return w_in.astype(BF16), w_out.astype(BF16), lam_re_rows, lam_im_rows, d_tab


def _pad_rows(w, start, total):
    return jnp.pad(w, ((0, 0), (start, total - start - w.shape[1]), (0, 0)))


def kernel(x, mem, ffn1_norm, ffn1_w_gate, ffn1_w_up, ffn1_w_down, mix_norm, w_in, gate_bias, mlstm_conv, mlstm_wq, mlstm_wk, mlstm_wv, mlstm_b_i, mlstm_b_f, mlstm_norm, mlstm_proj, s5_a_re, s5_a_im, s5_log_step, s5_b_re, s5_b_im, s5_c_re, s5_c_im, s5_d, s5_glu_w1, s5_glu_w2, gla_a_up, gla_a_bias, gla_norm, gla_proj, rwkv_mu, rwkv_w0, rwkv_w_up, rwkv_a0, rwkv_a_up, rwkv_g_up, rwkv_k_k, rwkv_k_a, rwkv_r_k, rwkv_norm, rwkv_proj, w_out, xattn_norm, mem_norm, xattn_wq, xattn_wk, xattn_wv, xattn_wo, ffn2_norm, ffn2_w_gate, ffn2_w_up, ffn2_w_down, final_norm):
    bsz, t, d = x.shape
    depth = w_in.shape[0]
    assert d == D_MODEL and t % TIME_BLOCK == 0 and 2 * bsz == SUBLANES
    xs = x.reshape(bsz * t, d)
    mems = mem.reshape(bsz * mem.shape[1], d)
    bf = lambda w: w.astype(BF16)
    vec = lambda v: v.reshape(depth, 1, -1)
    fin = final_norm.reshape(1, d)

    w_in_r = _regroup_w_in(w_in)
    gate_b = jnp.pad(jnp.concatenate([mlstm_b_i, mlstm_b_f], axis=1), ((0, 0), (0, SMALL - 2 * MLSTM_HEADS)))
    s5_w_in, s5_w_out, s5_lre, s5_lim, s5_dtab = _s5_tables(
        s5_a_re, s5_a_im, s5_log_step, s5_b_re, s5_b_im, s5_c_re, s5_c_im, s5_d, bsz)
    gla_up = _pad_rows(gla_a_up, 2 * MLSTM_HEADS, SMALL)
    mu_rkv = rwkv_mu[:, :3 * WIDTH].reshape(depth, 3, WIDTH)
    mu_t = vec(rwkv_mu[:, 3 * WIDTH:])
    rw_wup = bf(_pad_rows(rwkv_w_up, 0, RWKV_TAIL))
    rw_aup = bf(_pad_rows(rwkv_a_up, RWKV_DECAY_RANK, RWKV_TAIL))
    rw_gup = bf(_pad_rows(rwkv_g_up, RWKV_DECAY_RANK + RWKV_ICLR_RANK, RWKV_TAIL))
    w_kv = bf(jnp.concatenate([xattn_wk, xattn_wv], axis=2))
    wq_a, wk_a, wv_a = bf(mlstm_wq), bf(mlstm_wk), bf(mlstm_wv)
    p_a, p_b1, p_b2, p_c, p_d, p_o = (bf(mlstm_proj), bf(s5_glu_w1), bf(s5_glu_w2), bf(gla_proj), bf(rwkv_proj),
                                      bf(w_out))
    x_wq, x_wo = bf(xattn_wq), bf(xattn_wo)

    for l in range(depth):
        xs = _ffn(xs, vec(ffn1_norm), ffn1_w_gate, ffn1_w_up, ffn1_w_down, fin, layer=l, final=False)

        z = _norm_matmul(xs, vec(mix_norm), w_in_r, layer=l, tm=1024, tn=512, name="in_proj")

        y_a = _mlstm(z, bsz, l, mlstm_conv, wq_a, wk_a, wv_a, vec(gate_b), gate_b.reshape(depth, SMALL, 1),
                     vec(mlstm_norm))

        u_tm = z[:, ZC_BU:ZC_BU + WIDTH].reshape(bsz, t, 2, S5_HALF).transpose(1, 2, 0, 3).reshape(-1, S5_HALF)
        z_tm = _s5(u_tm, bsz, l, s5_w_in, s5_w_out, s5_lre, s5_lim, s5_dtab)
        z_b = z_tm.reshape(t, 2, bsz, S5_HALF).transpose(2, 0, 1, 3).reshape(bsz * t, WIDTH)

        y_c = _gla(z, bsz, l, gla_up, vec(gla_a_bias), vec(gla_norm))

        y_d = _rwkv(z, bsz, l, mu_rkv, mu_t, vec(rwkv_w0), rw_wup, vec(rwkv_a0), rw_aup, rw_gup,
                    vec(rwkv_k_k), vec(rwkv_k_a), vec(rwkv_r_k), vec(rwkv_norm))

        xs = _merge(xs, z, y_a, z_b, y_c, y_d, l, vec(gate_bias), p_a, p_b1, p_b2, p_c, p_d, p_o)

        kv = _norm_matmul(mems, vec(mem_norm), w_kv, layer=l, tm=mems.shape[0], tn=512, name="mem_kv")
        xs = _xattn(xs, bsz, l, vec(xattn_norm), x_wq, kv, x_wo)

        xs = _ffn(xs, vec(ffn2_norm), ffn2_w_gate, ffn2_w_up, ffn2_w_down, fin, layer=l,
                  final=(l == depth - 1))
    return xs.reshape(bsz, t, d)
```

```python
import functools
import math

import jax
import jax.numpy as jnp
import numpy as np
from jax import lax
from jax.experimental import pallas as pl
from jax.experimental.pallas import tpu as pltpu

F32 = jnp.float32
BF16 = jnp.bfloat16

D_MODEL = 1024
D_FF = 11 * D_MODEL // 4
NORM_EPS = 1e-6
FFN_HALF = 0.5
N_BRANCH = 4
WIDTH = D_MODEL // 2

MLSTM_HEADS = 4
MLSTM_HEAD_DIM = WIDTH // MLSTM_HEADS
MLSTM_CONV = 4

S5_GROUP = 16
S5_GROUPS = WIDTH // S5_GROUP
S5_STATE = 64
S5_MIN_NEG = 1e-4
S5_HALF = WIDTH // 2
S5_BLOCK_GROUPS = 8
S5_BLOCK_CH = S5_BLOCK_GROUPS * S5_GROUP
S5_BLOCK_STATES = S5_BLOCK_GROUPS * S5_STATE
S5_STEPS = 64
SUBLANES = 8

GLA_HEADS = 4
GLA_KEY_WIDTH = WIDTH // 2
GLA_HEAD_K = GLA_KEY_WIDTH // GLA_HEADS
GLA_HEAD_V = WIDTH // GLA_HEADS
GLA_GATE_RANK = 16
GLA_GATE_TAU = 16.0
GLA_CHUNK = 64

RWKV_HEAD = 64
RWKV_HEADS = WIDTH // RWKV_HEAD
RWKV_DECAY_RANK = 64
RWKV_ICLR_RANK = 64
RWKV_GATE_RANK = 128
RWKV_TAIL = RWKV_DECAY_RANK + RWKV_ICLR_RANK + RWKV_GATE_RANK
RWKV_GN_EPS = 64e-5
RWKV_CHUNK = 64
RWKV_GROUP_LANES = 256
RWKV_GROUP_HEADS = RWKV_GROUP_LANES // RWKV_HEAD

XATTN_HEADS = 4
XATTN_HEAD_DIM = D_MODEL // XATTN_HEADS

ZC_GATE = 0
ZC_AU = 4096
ZC_AO = 4608
ZC_BU = 5120
ZC_CV = 5632
ZC_CG = 6144
ZC_DR = 6656
ZC_DK = 7168
ZC_DV = 7680
ZC_CQ = 8192
ZC_CK = 8448
ZC_DT = 8704
ZC_SM = 8960
Z_COLS = 9216
SMALL = 128

TIME_BLOCK = 256
VMEM_LIMIT = 56 * 1024 * 1024

_NN = (((1,), (0,)), ((), ()))
_NT = (((1,), (1,)), ((), ()))


def _split(x, n):
    parts = []
    rest = x
    for i in range(n):
        p = rest.astype(BF16)
        parts.append(p)
        if i + 1 < n:
            rest = rest - p.astype(F32)
    return parts


def _mm(a, b, dn=_NN, pa=1, pb=1):
    ap, bp = _split(a, pa), _split(b, pb)
    order = max(pa, pb)
    out = None
    for i, x in enumerate(ap):
        for j, y in enumerate(bp):
            if i + j < order:
                t = lax.dot_general(x, y, dn, preferred_element_type=F32)
                out = t if out is None else out + t
    return out


def _rms(x, g):
    return x * lax.rsqrt(jnp.mean(x * x, axis=-1, keepdims=True) + NORM_EPS) * g


def _sigmoid(x):
    return 1.0 / (1.0 + jnp.exp(-x))


def _log_sigmoid(x):
    return jnp.minimum(x, 0.0) - jnp.log(1.0 + jnp.exp(-jnp.abs(x)))


def _iota(shape, dim):
    return lax.broadcasted_iota(jnp.int32, shape, dim)


def _tri(n, block, upper=False, strict=False):
    r, c = _iota((n, n), 0), _iota((n, n), 1)
    same = (r // block) == (c // block)
    if upper:
        keep = (r < c) if strict else (r <= c)
    else:
        keep = (r > c) if strict else (r >= c)
    return jnp.where(same & keep, 1.0, 0.0).astype(BF16)


def _shift_rows(prev_ref, x, k):
    ext = jnp.concatenate([prev_ref[...], x], axis=0)
    return pltpu.roll(ext, k, 0)[prev_ref.shape[0]:]


def _layer_spec(layer, *shape):
    return pl.BlockSpec((None,) + shape, lambda *_: (layer,) + (0,) * len(shape))


def _ffn_kernel(x_ref, g_ref, wg_ref, wu_ref, wd_ref, fg_ref, o_ref, xn_ref, acc_ref, *, final):
    j = pl.program_id(1)

    @pl.when(j == 0)
    def _():
        xn_ref[...] = _rms(x_ref[...], g_ref[...]).astype(BF16)
        acc_ref[...] = jnp.zeros_like(acc_ref)

    xn = xn_ref[...]
    gate = _mm(xn, wg_ref[...])
    up = _mm(xn, wu_ref[...])
    acc_ref[...] += _mm(gate * _sigmoid(gate) * up, wd_ref[...])

    @pl.when(j == pl.num_programs(1) - 1)
    def _():
        y = x_ref[...] + FFN_HALF * acc_ref[...]
        if final:
            y = _rms(y, fg_ref[...])
        o_ref[...] = y


def _ffn(x, gain, wg, wu, wd, final_gain, *, layer, final, tm=1024, tf=256):
    n, d = x.shape
    f = wg.shape[2]
    return pl.pallas_call(
        functools.partial(_ffn_kernel, final=final),
        grid=(n // tm, f // tf),
        in_specs=[
            pl.BlockSpec((tm, d), lambda i, j: (i, 0)),
            _layer_spec(layer, 1, d),
            pl.BlockSpec((None, d, tf), lambda i, j: (layer, 0, j)),
            pl.BlockSpec((None, d, tf), lambda i, j: (layer, 0, j)),
            pl.BlockSpec((None, tf, d), lambda i, j: (layer, j, 0)),
            pl.BlockSpec((1, d), lambda i, j: (0, 0)),
        ],
        out_specs=pl.BlockSpec((tm, d), lambda i, j: (i, 0)),
        out_shape=jax.ShapeDtypeStruct((n, d), F32),
        scratch_shapes=[pltpu.VMEM((tm, d), BF16), pltpu.VMEM((tm, d), F32)],
        compiler_params=pltpu.CompilerParams(
            dimension_semantics=("parallel", "arbitrary"), vmem_limit_bytes=VMEM_LIMIT),
        name="ffn",
    )(x, gain, wg, wu, wd, final_gain)


def _norm_matmul_kernel(x_ref, g_ref, w_ref, o_ref, xn_ref):
    @pl.when(pl.program_id(1) == 0)
    def _():
        xn_ref[...] = _rms(x_ref[...], g_ref[...]).astype(BF16)

    o_ref[...] = _mm(xn_ref[...], w_ref[...])


def _norm_matmul(x, gain, w, *, layer, tm, tn, name):
    n, d = x.shape
    m = w.shape[2]
    return pl.pallas_call(
        _norm_matmul_kernel,
        grid=(n // tm, m // tn),
        in_specs=[
            pl.BlockSpec((tm, d), lambda i, j: (i, 0)),
            _layer_spec(layer, 1, d),
            pl.BlockSpec((None, d, tn), lambda i, j: (layer, 0, j)),
        ],
        out_specs=pl.BlockSpec((tm, tn), lambda i, j: (i, j)),
        out_shape=jax.ShapeDtypeStruct((n, m), F32),
        scratch_shapes=[pltpu.VMEM((tm, d), BF16)],
        compiler_params=pltpu.CompilerParams(
            dimension_semantics=("parallel", "arbitrary"), vmem_limit_bytes=VMEM_LIMIT),
        name=name,
    )(x, gain, w)


def _mlstm_kernel(u_ref, o_ref, sm_ref, conv_ref, wq_ref, wk_ref, wv_ref, brow_ref, bcol_ref, ng_ref,
                  out_ref, uprev_ref, c_ref, m_ref):
    lb = u_ref.shape[0]
    e = MLSTM_HEAD_DIM

    @pl.when(pl.program_id(1) == 0)
    def _():
        uprev_ref[...] = jnp.zeros_like(uprev_ref)
        c_ref[...] = jnp.zeros_like(c_ref)
        m_ref[...] = jnp.zeros_like(m_ref)

    u = u_ref[...]
    conv = None
    for j in range(MLSTM_CONV):
        k = MLSTM_CONV - 1 - j
        term = conv_ref[j:j + 1, :] * (_shift_rows(uprev_ref, u, k) if k else u)
        conv = term if conv is None else conv + term
    uprev_ref[...] = u[lb - uprev_ref.shape[0]:]
    uc = conv * _sigmoid(conv)

    sm = sm_ref[...]
    pre = sm + brow_ref[...]
    log_f = _log_sigmoid(pre)
    pre_t = sm.T[0:8, :] + bcol_ref[0:8, :]
    log_f_t = _log_sigmoid(pre_t)
    b_cols = _mm(_tri(lb, lb), log_f, pb=3)
    b_rows = _mm(log_f_t, _tri(lb, lb, upper=True), pa=3)

    causal = _iota((lb, lb), 0) >= _iota((lb, lb), 1)
    heads = range(MLSTM_HEADS)
    sls = [slice(h * e, (h + 1) * e) for h in heads]
    ones = jnp.ones((lb, e), BF16)
    q = [_mm(uc[:, sls[h]], wq_ref[h]).astype(BF16) for h in heads]
    k = [_mm(uc[:, sls[h]], wk_ref[h]) * e ** -0.5 for h in heads]
    v = [jnp.concatenate([_mm(u[:, sls[h]], wv_ref[h]).astype(BF16), ones], axis=1) for h in heads]
    bc = [b_cols[:, MLSTM_HEADS + h:MLSTM_HEADS + h + 1] for h in heads]
    m_st = [m_ref[h:h + 1, 0:1] for h in heads]
    dmat = [jnp.where(causal, bc[h] - b_rows[MLSTM_HEADS + h:MLSTM_HEADS + h + 1, :] + pre_t[h:h + 1, :], -jnp.inf)
            for h in heads]
    m_inter = [bc[h] + m_st[h] for h in heads]
    m_row = [jnp.maximum(m_inter[h], jnp.max(dmat[h], axis=-1, keepdims=True)) for h in heads]
    qk = [_mm(q[h], k[h], _NT) for h in heads]
    w = [jnp.exp(dmat[h] - m_row[h]) * qk[h] for h in heads]
    from_state = [_mm(q[h], c_ref[h]) for h in heads]
    from_chunk = [_mm(w[h], v[h]) for h in heads]
    for h in heads:
        tot = jnp.exp(m_inter[h] - m_row[h]) * from_state[h] + from_chunk[h]
        hh = tot[:, :e] / jnp.maximum(jnp.abs(tot[:, e:e + 1]), jnp.exp(-m_row[h]))
        hn = hh * lax.rsqrt(jnp.mean(hh * hh, axis=-1, keepdims=True) + NORM_EPS) * ng_ref[:, sls[h]]
        out_ref[:, sls[h]] = hn * _sigmoid(o_ref[:, sls[h]])
    for h in heads:
        b_last = bc[h][lb - 1:lb, :]
        e_col = b_last - bc[h] + pre[:, h:h + 1]
        m_new = jnp.maximum(b_last + m_st[h], jnp.max(e_col, axis=0, keepdims=True))
        decay = jnp.exp(b_last + m_st[h] - m_new)
        wk = jnp.exp(e_col - m_new) * k[h]
        c_ref[h] = decay * c_ref[h] + _mm(wk.T, v[h])
        m_ref[h:h + 1, :] = jnp.broadcast_to(m_new, (1, m_ref.shape[1]))


def _mlstm(z, bsz, layer, conv_w, wq, wk, wv, b_row, b_col, norm_g):
    n = z.shape[0]
    lb = TIME_BLOCK
    nblk = n // bsz // lb
    row = lambda b, c: b * nblk + c
    hd = MLSTM_HEAD_DIM
    return pl.pallas_call(
        _mlstm_kernel,
        grid=(bsz, nblk),
        in_specs=[
            pl.BlockSpec((lb, WIDTH), lambda b, c: (row(b, c), ZC_AU // WIDTH)),
            pl.BlockSpec((lb, WIDTH), lambda b, c: (row(b, c), ZC_AO // WIDTH)),
            pl.BlockSpec((lb, SMALL), lambda b, c: (row(b, c), ZC_SM // SMALL)),
            _layer_spec(layer, MLSTM_CONV, WIDTH),
            _layer_spec(layer, MLSTM_HEADS, hd, hd),
            _layer_spec(layer, MLSTM_HEADS, hd, hd),
            _layer_spec(layer, MLSTM_HEADS, hd, hd),
            _layer_spec(layer, 1, SMALL),
            _layer_spec(layer, SMALL, 1),
            _layer_spec(layer, 1, WIDTH),
        ],
        out_specs=pl.BlockSpec((lb, WIDTH), lambda b, c: (row(b, c), 0)),
        out_shape=jax.ShapeDtypeStruct((n, WIDTH), F32),
        scratch_shapes=[
            pltpu.VMEM((8, WIDTH), F32),
            pltpu.VMEM((MLSTM_HEADS, hd, 2 * hd), F32),
            pltpu.VMEM((8, hd), F32),
        ],
        compiler_params=pltpu.CompilerParams(
            dimension_semantics=("parallel", "arbitrary"), vmem_limit_bytes=VMEM_LIMIT),
        name="mlstm",
    )(z, z, z, conv_w, wq, wk, wv, b_row, b_col, norm_g)


def _s5_kernel(u_ref, pin_ref, pout_ref, win_ref, wout_ref, lre_ref, lim_ref, d_ref, out_ref, x_ref, c_ref, *, bsz):
    steps = u_ref.shape[1]
    rows = steps * SUBLANES
    ns = S5_BLOCK_STATES
    cw = S5_BLOCK_CH
    nblk = S5_HALF // cw

    @pl.when(pl.program_id(0) == 0)
    def _():
        c_ref[...] = jnp.zeros_like(c_ref)

    u_bt = u_ref[...].reshape(bsz * steps, WIDTH)
    u = (_mm(pin_ref[0], u_bt[:, :S5_HALF], pb=3)
         + _mm(pin_ref[1], u_bt[:, S5_HALF:], pb=3))
    upper = (_iota((rows, 1), 0) % SUBLANES) >= bsz
    for j in range(nblk):
        uj = u[:, j * cw:(j + 1) * cw]
        uext = jnp.concatenate([jnp.where(upper, 0.0, uj), jnp.where(upper, uj, 0.0)], axis=1)
        x_ref[:, j * 2 * ns:(j + 1) * 2 * ns] = _mm(uext, win_ref[j])

    lre, lim = lre_ref[...], lim_ref[...]

    def step(t, carry):
        r0 = pl.multiple_of(t * SUBLANES, SUBLANES)
        new = []
        for j in range(nblk):
            sr, si = carry[2 * j], carry[2 * j + 1]
            lr, li = lre[:, j * ns:(j + 1) * ns], lim[:, j * ns:(j + 1) * ns]
            re_cols = slice(j * 2 * ns, j * 2 * ns + ns)
            im_cols = slice(j * 2 * ns + ns, (j + 1) * 2 * ns)
            nr = lr * sr - li * si + x_ref[pl.ds(r0, SUBLANES), re_cols]
            ni = lr * si + li * sr + x_ref[pl.ds(r0, SUBLANES), im_cols]
            x_ref[pl.ds(r0, SUBLANES), re_cols] = nr
            x_ref[pl.ds(r0, SUBLANES), im_cols] = ni
            new += [nr, ni]
        return tuple(new)

    init = tuple(c_ref[:, k * ns:(k + 1) * ns] for k in range(2 * nblk))
    last = lax.fori_loop(0, steps, step, init, unroll=8)
    for k in range(2 * nblk):
        c_ref[:, k * ns:(k + 1) * ns] = last[k]

    ys = []
    for j in range(nblk):
        res = _mm(x_ref[:, j * 2 * ns:(j + 1) * 2 * ns], wout_ref[j])
        cols = slice(j * cw, (j + 1) * cw)
        y = jnp.where(upper, res[:, cw:], res[:, :cw]) + d_ref[:, cols] * u[:, cols]
        ys.append(0.5 * y * (1.0 + jnp.tanh(math.sqrt(2.0 / math.pi) * (y + 0.044715 * y * y * y))))
    y = jnp.concatenate(ys, axis=1).astype(BF16)
    out = jnp.concatenate([_mm(pout_ref[0], y), _mm(pout_ref[1], y)], axis=1)
    out_ref[...] = out.reshape(bsz, steps, WIDTH)


def _s5_permutations(bsz, steps):
    rows = steps * 2 * bsz
    r = np.arange(rows)
    p = np.zeros((2, rows, bsz * steps), np.float32)
    p[(r % (2 * bsz)) // bsz, r, (r % bsz) * steps + r // (2 * bsz)] = 1.0
    return jnp.asarray(p, BF16), jnp.asarray(p.transpose(0, 2, 1), BF16)


def _s5(z, bsz, layer, w_in, w_out, lam_re, lam_im, d_tab):
    t = z.shape[1]
    steps = S5_STEPS
    rows = steps * SUBLANES
    nblk = S5_HALF // S5_BLOCK_CH
    nst = nblk * S5_BLOCK_STATES
    p_in, p_out = _s5_permutations(bsz, steps)
    whole = lambda *shape: pl.BlockSpec(shape, lambda i: (0,) * len(shape))
    return pl.pallas_call(
        functools.partial(_s5_kernel, bsz=bsz),
        grid=(t // steps,),
        in_specs=[
            pl.BlockSpec((bsz, steps, WIDTH), lambda i: (0, i, ZC_BU // WIDTH)),
            whole(2, rows, bsz * steps),
            whole(2, bsz * steps, rows),
            _layer_spec(layer, nblk, 2 * S5_BLOCK_CH, 2 * S5_BLOCK_STATES),
            _layer_spec(layer, nblk, 2 * S5_BLOCK_STATES, 2 * S5_BLOCK_CH),
            _layer_spec(layer, SUBLANES, nst),
            _layer_spec(layer, SUBLANES, nst),
            _layer_spec(layer, rows, S5_HALF),
        ],
        out_specs=pl.BlockSpec((bsz, steps, WIDTH), lambda i: (0, i, 0)),
        out_shape=jax.ShapeDtypeStruct((bsz, t, WIDTH), F32),
        scratch_shapes=[pltpu.VMEM((rows, 2 * nst), F32), pltpu.VMEM((SUBLANES, 2 * nst), F32)],
        compiler_params=pltpu.CompilerParams(
            dimension_semantics=("arbitrary",), vmem_limit_bytes=VMEM_LIMIT),
        name="s5",
    )(z, p_in, p_out, w_in, w_out, lam_re, lam_im, d_tab)


def _gla_kernel(q_ref, k_ref, v_ref, g_ref, sm_ref, aup_ref, ab_ref, ng_ref, out_ref, st_ref):
    lb = q_ref.shape[0]
    ch = GLA_CHUNK
    kw = GLA_KEY_WIDTH

    @pl.when(pl.program_id(1) == 0)
    def _():
        st_ref[...] = jnp.zeros_like(st_ref)

    x = _mm(sm_ref[...], aup_ref[...], pa=2, pb=2) + ab_ref[...]
    log_a = _log_sigmoid(x) * (1.0 / GLA_GATE_TAU)
    b_all = _mm(_tri(lb, ch), log_a, pb=3)
    lane_head = _iota((1, kw), 1) // GLA_HEAD_K
    masks = [jnp.where(lane_head == h, 1.0, 0.0).astype(BF16) for h in range(GLA_HEADS)]
    causal = _iota((ch, ch), 0) >= _iota((ch, ch), 1)
    mid = ch // 2 - 1
    heads = range(GLA_HEADS)
    chunks = range(lb // ch)
    rss = [slice(c * ch, (c + 1) * ch) for c in chunks]
    vss = [slice(h * GLA_HEAD_V, (h + 1) * GLA_HEAD_V) for h in heads]

    q_in, q_at, k_at, k_st, dec = [], [], [], [], []
    for c in chunks:
        b = b_all[rss[c]]
        b_mid, b_last = b[mid:mid + 1, :], b[ch - 1:ch, :]
        q = q_ref[rss[c], :] * GLA_HEAD_K ** -0.5
        k = k_ref[rss[c], :]
        q_in.append((q * jnp.exp(b)).astype(BF16))
        q_at.append((q * jnp.exp(b - b_mid)).astype(BF16))
        k_at.append((k * jnp.exp(b_mid - b)).astype(BF16))
        k_st.append((k * jnp.exp(b_last - b)).astype(BF16))
        dec.append(jnp.exp(b_last))
    v = [[v_ref[rss[c], vss[h]] for h in heads] for c in chunks]
    attn = [[jnp.where(causal, _mm(q_at[c] * masks[h], k_at[c], _NT), 0.0) for h in heads] for c in chunks]
    grow = [[_mm(v[c][h].T, k_st[c] * masks[h]) for h in heads] for c in chunks]
    state = [st_ref[h] for h in heads]
    for c in chunks:
        for h in heads:
            o = _mm(q_in[c] * masks[h], state[h], _NT) + _mm(attn[c][h], v[c][h])
            state[h] = dec[c] * state[h] + grow[c][h]
            on = o * lax.rsqrt(jnp.mean(o * o, axis=-1, keepdims=True) + NORM_EPS) * ng_ref[:, vss[h]]
            g = g_ref[rss[c], vss[h]]
            out_ref[rss[c], vss[h]] = on * (g * _sigmoid(g))
    for h in heads:
        st_ref[h] = state[h]


def _gla(z, bsz, layer, a_up, a_bias, norm_g):
    n = z.shape[0]
    lb = TIME_BLOCK
    nblk = n // bsz // lb
    row = lambda b, c: b * nblk + c
    return pl.pallas_call(
        _gla_kernel,
        grid=(bsz, nblk),
        in_specs=[
            pl.BlockSpec((lb, GLA_KEY_WIDTH), lambda b, c: (row(b, c), ZC_CQ // GLA_KEY_WIDTH)),
            pl.BlockSpec((lb, GLA_KEY_WIDTH), lambda b, c: (row(b, c), ZC_CK // GLA_KEY_WIDTH)),
            pl.BlockSpec((lb, WIDTH), lambda b, c: (row(b, c), ZC_CV // WIDTH)),
            pl.BlockSpec((lb, WIDTH), lambda b, c: (row(b, c), ZC_CG // WIDTH)),
            pl.BlockSpec((lb, SMALL), lambda b, c: (row(b, c), ZC_SM // SMALL)),
            _layer_spec(layer, SMALL, GLA_KEY_WIDTH),
            _layer_spec(layer, 1, GLA_KEY_WIDTH),
            _layer_spec(layer, 1, WIDTH),
        ],
        out_specs=pl.BlockSpec((lb, WIDTH), lambda b, c: (row(b, c), 0)),
        out_shape=jax.ShapeDtypeStruct((n, WIDTH), F32),
        scratch_shapes=[pltpu.VMEM((GLA_HEADS, GLA_HEAD_V, GLA_KEY_WIDTH), F32)],
        compiler_params=pltpu.CompilerParams(
            dimension_semantics=("parallel", "arbitrary"), vmem_limit_bytes=VMEM_LIMIT),
        name="gla",
    )(z, z, z, z, z, a_up, a_bias, norm_g)


def _rwkv_kernel(r_ref, k_ref, v_ref, t_ref, mu_ref, mut_ref,
                 w0_ref, wup_ref, a0_ref, aup_ref, gup_ref, kk_ref, ka_ref, rk_ref, ng_ref,
                 out_ref, pr_ref, pk_ref, pv_ref, pt_ref, st_ref):
    lb = r_ref.shape[0]
    ch = RWKV_CHUNK
    gl = RWKV_GROUP_LANES
    gh = RWKV_GROUP_HEADS
    rows_s = gh * ch

    @pl.when(pl.program_id(1) == 0)
    def _():
        for ref in (pr_ref, pk_ref, pv_ref, pt_ref, st_ref):
            ref[...] = jnp.zeros_like(ref)

    def shift_mix(x_ref, prev_ref, mu):
        x = x_ref[...]
        xs = _shift_rows(prev_ref, x, 1)
        prev_ref[...] = x[lb - prev_ref.shape[0]:]
        return x + mu * (xs - x)

    r = shift_mix(r_ref, pr_ref, mu_ref[0:1, :])
    k = shift_mix(k_ref, pk_ref, mu_ref[1:2, :])
    v = shift_mix(v_ref, pv_ref, mu_ref[2:3, :])
    tl = shift_mix(t_ref, pt_ref, mut_ref[...])

    ones_bd = _tri(WIDTH, RWKV_HEAD) + _tri(WIDTH, RWKV_HEAD, upper=True, strict=True)
    w_log = _log_sigmoid(w0_ref[...] + _mm(jnp.tanh(tl), wup_ref[...])) - 0.5
    log_w = -jnp.exp(w_log)
    a = _sigmoid(a0_ref[...] + _mm(tl, aup_ref[...]))
    g = _mm(_sigmoid(tl), gup_ref[...])
    kk = k * kk_ref[...]
    kk = kk / jnp.maximum(jnp.sqrt(_mm(kk * kk, ones_bd, pa=3)), 1e-12)
    k_rep = k * (1.0 + (a - 1.0) * ka_ref[...])
    av = -kk
    bv = kk * a
    bonus = _mm(r * k_rep * rk_ref[...], ones_bd, pa=3) * v
    gc_all = _mm(_tri(lb, ch), log_w, pb=3)

    lane_head = _iota((1, gl), 1) // RWKV_HEAD
    masks = [jnp.where(lane_head == h, 1.0, 0.0).astype(BF16) for h in range(gh)]

    def stack(x):
        xb = x.astype(BF16)
        return jnp.concatenate([xb * m for m in masks], axis=0)

    t_row = _iota((rows_s, rows_s), 0) % ch
    t_col = _iota((rows_s, rows_s), 1) % ch
    strict = t_row > t_col
    incl = t_row >= t_col
    eye = jnp.where(_iota((rows_s, rows_s), 0) == _iota((rows_s, rows_s), 1), 1.0, 0.0)

    pairs = [(gi, c) for gi in range(WIDTH // gl) for c in range(lb // ch)]
    pre = {}
    for gi, c in pairs:
        ls = slice(gi * gl, (gi + 1) * gl)
        rs = slice(c * ch, (c + 1) * ch)
        gc = gc_all[rs, ls]
        g_last = gc[ch - 1:ch, :]
        e_inv = jnp.exp(-gc)
        e_end = jnp.exp(g_last - gc)
        r_t = stack(r[rs, ls] * jnp.exp(gc))
        a_t = stack(av[rs, ls] * jnp.exp(gc - log_w[rs, ls]))
        b_t = stack(bv[rs, ls] * e_inv)
        k_t = stack(k_rep[rs, ls] * e_inv)
        mm = _mm(jnp.concatenate([a_t, r_t], axis=0), jnp.concatenate([b_t, k_t], axis=0), _NT)
        pre[gi, c] = dict(
            r_t=r_t, a_t=a_t, v_s=stack(v[rs, ls]), dec=jnp.exp(g_last),
            b_e=stack(bv[rs, ls] * e_end), k_e=stack(k_rep[rs, ls] * e_end),
            a_ab=jnp.where(strict, mm[:rows_s, :rows_s], 0.0),
            a_ak=jnp.where(strict, mm[:rows_s, rows_s:], 0.0).astype(BF16),
            a_rb=jnp.where(incl, mm[rows_s:, :rows_s], 0.0).astype(BF16),
            a_rk=jnp.where(incl, mm[rows_s:, rows_s:], 0.0).astype(BF16))
    tinv = {key: eye + d["a_ab"] for key, d in pre.items()}
    power = {key: d["a_ab"].astype(BF16) for key, d in pre.items()}
    step = 2
    while step < ch:
        for key in pairs:
            power[key] = _mm(power[key], power[key]).astype(BF16)
        for key in pairs:
            tinv[key] = tinv[key] + _mm(tinv[key], power[key])
        step *= 2

    def finish_phase1(key):
        d = pre[key]
        tb = tinv[key].astype(BF16)
        d["a_hat"] = _mm(tb, d["a_t"]).astype(BF16)
        d["u0"] = _mm(tb, _mm(d["a_ak"], d["v_s"]))
        d["y0"] = _mm(d["a_rk"], d["v_s"])
        d["s0"] = _mm(d["v_s"].astype(F32).T, d["k_e"])

    groups = range(WIDTH // gl)
    state = [st_ref[gi] for gi in groups]

    def solve_u(gi, c):
        d = pre[gi, c]
        d["st"] = state[gi].astype(BF16)
        return _mm(d["a_hat"], d["st"], _NT) + d["u0"]

    def emit_and_advance(gi, c, u_s):
        d = pre[gi, c]
        y_s = _mm(d["r_t"], d["st"], _NT) + _mm(d["a_rb"], u_s) + d["y0"]
        y = y_s[0:ch]
        for h in range(1, gh):
            y = y + y_s[h * ch:(h + 1) * ch]
        out_ref[c * ch:(c + 1) * ch, gi * gl:(gi + 1) * gl] = y
        state[gi] = d["dec"] * state[gi] + _mm(u_s.T, d["b_e"]) + d["s0"]

    n_chunks = lb // ch
    for c in range(n_chunks + 1):
        u_prev = {}
        for gi in groups:
            if c < n_chunks:
                finish_phase1((gi, c))
            if c > 0:
                u_prev[gi] = solve_u(gi, c - 1)
        if c > 0:
            for gi in groups:
                emit_and_advance(gi, c - 1, u_prev[gi])
    for gi in groups:
        st_ref[gi] = state[gi]

    y = out_ref[...]
    inv_n = 1.0 / RWKV_HEAD
    mu = _mm(y, ones_bd, pa=3) * inv_n
    yc = y - mu
    var = _mm(yc * yc, ones_bd, pa=3) * inv_n
    yn = yc * lax.rsqrt(var + RWKV_GN_EPS) * ng_ref[...]
    out_ref[...] = (yn + bonus) * g


def _rwkv(z, bsz, layer, mu_rkv, mu_t, w0, w_up, a0, a_up, g_up, k_k, k_a, r_k, norm_g):
    n = z.shape[0]
    lb = TIME_BLOCK
    nblk = n // bsz // lb
    row = lambda b, c: b * nblk + c
    vec = _layer_spec(layer, 1, WIDTH)
    mat = _layer_spec(layer, RWKV_TAIL, WIDTH)
    return pl.pallas_call(
        _rwkv_kernel,
        grid=(bsz, nblk),
        in_specs=[
            pl.BlockSpec((lb, WIDTH), lambda b, c: (row(b, c), ZC_DR // WIDTH)),
            pl.BlockSpec((lb, WIDTH), lambda b, c: (row(b, c), ZC_DK // WIDTH)),
            pl.BlockSpec((lb, WIDTH), lambda b, c: (row(b, c), ZC_DV // WIDTH)),
            pl.BlockSpec((lb, RWKV_TAIL), lambda b, c: (row(b, c), ZC_DT // RWKV_TAIL)),
            _layer_spec(layer, 3, WIDTH), _layer_spec(layer, 1, RWKV_TAIL),
            vec, mat, vec, mat, mat, vec, vec, vec, vec,
        ],
        out_specs=pl.BlockSpec((lb, WIDTH), lambda b, c: (row(b, c), 0)),
        out_shape=jax.ShapeDtypeStruct((n, WIDTH), F32),
        scratch_shapes=[
            pltpu.VMEM((8, WIDTH), F32), pltpu.VMEM((8, WIDTH), F32), pltpu.VMEM((8, WIDTH), F32),
            pltpu.VMEM((8, RWKV_TAIL), F32),
            pltpu.VMEM((WIDTH // RWKV_GROUP_LANES, RWKV_GROUP_LANES, RWKV_GROUP_LANES), F32),
        ],
        compiler_params=pltpu.CompilerParams(
            dimension_semantics=("parallel", "arbitrary"), vmem_limit_bytes=VMEM_LIMIT),
        name="rwkv",
    )(z, z, z, z, mu_rkv, mu_t, w0, w_up, a0, a_up, g_up, k_k, k_a, r_k, norm_g)


def _merge_kernel(x_ref, ya_ref, zb_ref, yc_ref, yd_ref, g0_ref, g1_ref, g2_ref, g3_ref, gb_ref,
                  pa_ref, w1_ref, w2_ref, pc_ref, pd_ref, wo_ref, out_ref):
    d = D_MODEL

    def gate(g_ref, i):
        return _sigmoid(g_ref[...] + gb_ref[:, i * d:(i + 1) * d])

    zb = zb_ref[...]
    merged = gate(g0_ref, 0) * _mm(ya_ref[...], pa_ref[...])
    merged = merged + gate(g1_ref, 1) * (_mm(zb, w1_ref[...]) * _sigmoid(_mm(zb, w2_ref[...])))
    merged = merged + gate(g2_ref, 2) * _mm(yc_ref[...], pc_ref[...])
    merged = merged + gate(g3_ref, 3) * _mm(yd_ref[...], pd_ref[...])
    out_ref[...] = x_ref[...] + _mm(merged, wo_ref[...])


def _merge(x, z, ya, zb, yc, yd, layer, gate_bias, pa, w1, w2, pc, pd, wo, *, tm=256):
    n, d = x.shape
    tile = lambda w: pl.BlockSpec((tm, w), lambda i: (i, 0))
    gate = lambda b: pl.BlockSpec((tm, d), lambda i: (i, ZC_GATE // d + b))
    proj = _layer_spec(layer, WIDTH, d)
    return pl.pallas_call(
        _merge_kernel,
        grid=(n // tm,),
        in_specs=[tile(d), tile(WIDTH), tile(WIDTH), tile(WIDTH), tile(WIDTH),
                  gate(0), gate(1), gate(2), gate(3), _layer_spec(layer, 1, N_BRANCH * d),
                  proj, proj, proj, proj, proj, _layer_spec(layer, d, d)],
        out_specs=tile(d),
        out_shape=jax.ShapeDtypeStruct((n, d), F32),
        compiler_params=pltpu.CompilerParams(
            dimension_semantics=("parallel",), vmem_limit_bytes=VMEM_LIMIT),
        name="merge",
    )(x, ya, zb, yc, yd, z, z, z, z, gate_bias, pa, w1, w2, pc, pd, wo)


def _xattn_kernel(x_ref, g_ref, wq_ref, kv_ref, wo_ref, out_ref):
    d = D_MODEL
    hd = XATTN_HEAD_DIM
    x = x_ref[...]
    q = _mm(_rms(x, g_ref[...]), wq_ref[...])
    kv = kv_ref[...]
    outs = []
    for h in range(XATTN_HEADS):
        sl = slice(h * hd, (h + 1) * hd)
        s = _mm(q[:, sl], kv[:, sl], _NT) * hd ** -0.5
        s = s - jnp.max(s, axis=-1, keepdims=True)
        p = jnp.exp(s)
        p = p / jnp.sum(p, axis=-1, keepdims=True)
        outs.append(_mm(p, kv[:, d + h * hd:d + (h + 1) * hd]))
    o = jnp.concatenate(outs, axis=1)
    out_ref[...] = x + _mm(o, wo_ref[...])


def _xattn(x, bsz, layer, gain, wq, kv, wo, *, tm=512):
    n, d = x.shape
    nblk = n // bsz // tm
    m_len = kv.shape[0] // bsz
    return pl.pallas_call(
        _xattn_kernel,
        grid=(bsz, nblk),
        in_specs=[
            pl.BlockSpec((tm, d), lambda b, c: (b * nblk + c, 0)),
            _layer_spec(layer, 1, d),
            _layer_spec(layer, d, d),
            pl.BlockSpec((m_len, 2 * d), lambda b, c: (b, 0)),
            _layer_spec(layer, d, d),
        ],
        out_specs=pl.BlockSpec((tm, d), lambda b, c: (b * nblk + c, 0)),
        out_shape=jax.ShapeDtypeStruct((n, d), F32),
        compiler_params=pltpu.CompilerParams(
            dimension_semantics=("parallel", "parallel"), vmem_limit_bytes=VMEM_LIMIT),
        name="xattn",
    )(x, gain, wq, kv, wo)


def _regroup_w_in(w):
    o = np.cumsum((0, WIDTH, WIDTH, MLSTM_HEADS, MLSTM_HEADS, WIDTH, GLA_KEY_WIDTH, GLA_KEY_WIDTH, WIDTH, WIDTH,
                   GLA_GATE_RANK, 3 * WIDTH + RWKV_TAIL, N_BRANCH * D_MODEL))
    a_u, a_o, a_i, a_f, b_u, c_q, c_k, c_v, c_g, c_a, d_z, gate = (w[..., o[i]:o[i + 1]] for i in range(12))
    d_r, d_k, d_v, d_t = (d_z[..., :WIDTH], d_z[..., WIDTH:2 * WIDTH], d_z[..., 2 * WIDTH:3 * WIDTH],
                          d_z[..., 3 * WIDTH:])
    used = Z_COLS - 2 * SMALL + 2 * MLSTM_HEADS + GLA_GATE_RANK
    pad = jnp.zeros(w.shape[:-1] + (Z_COLS - used,), w.dtype)
    cols = jnp.concatenate([gate, a_u, a_o, b_u, c_v, c_g, d_r, d_k, d_v, c_q, c_k, d_t, a_i, a_f, c_a, pad], axis=-1)
    return cols.astype(BF16)


def _s5_tables(a_re, a_im, log_step, b_re, b_im, c_re, c_im, d_skip, bsz):
    depth = a_re.shape[0]
    step = jnp.exp(log_step)[..., None]
    lam_re = jnp.minimum(a_re, -S5_MIN_NEG)
    lam_im = a_im
    mag = jnp.exp(lam_re * step)
    bar_re = mag * jnp.cos(lam_im * step)
    bar_im = mag * jnp.sin(lam_im * step)
    denom = lam_re * lam_re + lam_im * lam_im
    coef_re = ((bar_re - 1.0) * lam_re + bar_im * lam_im) / denom
    coef_im = (bar_im * lam_re - (bar_re - 1.0) * lam_im) / denom
    bb_re = coef_re[..., None] * b_re - coef_im[..., None] * b_im
    bb_im = coef_re[..., None] * b_im + coef_im[..., None] * b_re
    nb = S5_HALF // S5_BLOCK_CH
    gpb = S5_BLOCK_GROUPS
    eye = jnp.eye(gpb, dtype=F32)

    def in_block(bb):
        bb = bb.reshape(depth, 2, nb, gpb, S5_STATE, S5_GROUP)
        return jnp.einsum('lhjgpc,gk->lhjgckp', bb, eye).reshape(depth, 2, nb, S5_BLOCK_CH, S5_BLOCK_STATES)

    def out_block(cc):
        cc = cc.reshape(depth, 2, nb, gpb, S5_GROUP, S5_STATE)
        return jnp.einsum('lhjgcp,gk->lhjgpkc', cc, eye).reshape(depth, 2, nb, S5_BLOCK_STATES, S5_BLOCK_CH)

    w_in = jnp.concatenate([in_block(bb_re), in_block(bb_im)], axis=-1)
    w_in = w_in.transpose(0, 2, 1, 3, 4).reshape(depth, nb, 2 * S5_BLOCK_CH, 2 * S5_BLOCK_STATES)
    w_out = jnp.concatenate([out_block(c_re), -out_block(c_im)], axis=-2)
    w_out = w_out.transpose(0, 2, 3, 1, 4).reshape(depth, nb, 2 * S5_BLOCK_STATES, 2 * S5_BLOCK_CH)

    def rows(t, width):
        t = t.reshape(depth, 2, 1, width)
        return jnp.broadcast_to(t, (depth, 2, bsz, width)).reshape(depth, 2 * bsz, width)

    nst = nb * S5_BLOCK_STATES
    lam_re_rows = rows(bar_re.reshape(depth, 2 * nst), nst)
    lam_im_rows = rows(bar_im.reshape(depth, 2 * nst), nst)
    d_tab = jnp.tile(rows(d_skip, S5_HALF), (1, S5_STEPS, 1))
    return w_in.astype(BF16), w_out.astype(BF16), lam_re_rows, lam_im_rows, d_tab


def _pad_rows(w, start, total):
    return jnp.pad(w, ((0, 0), (start, total - start - w.shape[1]), (0, 0)))


def kernel(x, mem, ffn1_norm, ffn1_w_gate, ffn1_w_up, ffn1_w_down, mix_norm, w_in, gate_bias, mlstm_conv, mlstm_wq, mlstm_wk, mlstm_wv, mlstm_b_i, mlstm_b_f, mlstm_norm, mlstm_proj, s5_a_re, s5_a_im, s5_log_step, s5_b_re, s5_b_im, s5_c_re, s5_c_im, s5_d, s5_glu_w1, s5_glu_w2, gla_a_up, gla_a_bias, gla_norm, gla_proj, rwkv_mu, rwkv_w0, rwkv_w_up, rwkv_a0, rwkv_a_up, rwkv_g_up, rwkv_k_k, rwkv_k_a, rwkv_r_k, rwkv_norm, rwkv_proj, w_out, xattn_norm, mem_norm, xattn_wq, xattn_wk, xattn_wv, xattn_wo, ffn2_norm, ffn2_w_gate, ffn2_w_up, ffn2_w_down, final_norm):
    bsz, t, d = x.shape
    depth = w_in.shape[0]
    assert d == D_MODEL and t % TIME_BLOCK == 0 and 2 * bsz == SUBLANES
    xs = x.reshape(bsz * t, d)
    mems = mem.reshape(bsz * mem.shape[1], d)
    bf = lambda w: w.astype(BF16)
    vec = lambda v: v.reshape(depth, 1, -1)
    fin = final_norm.reshape(1, d)

    w_in_r = _regroup_w_in(w_in)
    gate_b = jnp.pad(jnp.concatenate([mlstm_b_i, mlstm_b_f], axis=1), ((0, 0), (0, SMALL - 2 * MLSTM_HEADS)))
    s5_w_in, s5_w_out, s5_lre, s5_lim, s5_dtab = _s5_tables(
        s5_a_re, s5_a_im, s5_log_step, s5_b_re, s5_b_im, s5_c_re, s5_c_im, s5_d, bsz)
    gla_up = _pad_rows(gla_a_up, 2 * MLSTM_HEADS, SMALL)
    mu_rkv = rwkv_mu[:, :3 * WIDTH].reshape(depth, 3, WIDTH)
    mu_t = vec(rwkv_mu[:, 3 * WIDTH:])
    rw_wup = bf(_pad_rows(rwkv_w_up, 0, RWKV_TAIL))
    rw_aup = bf(_pad_rows(rwkv_a_up, RWKV_DECAY_RANK, RWKV_TAIL))
    rw_gup = bf(_pad_rows(rwkv_g_up, RWKV_DECAY_RANK + RWKV_ICLR_RANK, RWKV_TAIL))
    w_kv = bf(jnp.concatenate([xattn_wk, xattn_wv], axis=2))
    wq_a, wk_a, wv_a = bf(mlstm_wq), bf(mlstm_wk), bf(mlstm_wv)
    p_a, p_b1, p_b2, p_c, p_d, p_o = (bf(mlstm_proj), bf(s5_glu_w1), bf(s5_glu_w2), bf(gla_proj), bf(rwkv_proj),
                                      bf(w_out))
    x_wq, x_wo = bf(xattn_wq), bf(xattn_wo)

    for l in range(depth):
        xs = _ffn(xs, vec(ffn1_norm), ffn1_w_gate, ffn1_w_up, ffn1_w_down, fin, layer=l, final=False)

        z = _norm_matmul(xs, vec(mix_norm), w_in_r, layer=l, tm=1024, tn=1536, name="in_proj")

        y_a = _mlstm(z, bsz, l, mlstm_conv, wq_a, wk_a, wv_a, vec(gate_b), gate_b.reshape(depth, SMALL, 1),
                     vec(mlstm_norm))

        z_b = _s5(z.reshape(bsz, t, Z_COLS), bsz, l, s5_w_in, s5_w_out, s5_lre, s5_lim, s5_dtab)
        z_b = z_b.reshape(bsz * t, WIDTH)

        y_c = _gla(z, bsz, l, gla_up, vec(gla_a_bias), vec(gla_norm))

        y_d = _rwkv(z, bsz, l, mu_rkv, mu_t, vec(rwkv_w0), rw_wup, vec(rwkv_a0), rw_aup, rw_gup,
                    vec(rwkv_k_k), vec(rwkv_k_a), vec(rwkv_r_k), vec(rwkv_norm))

        xs = _merge(xs, z, y_a, z_b, y_c, y_d, l, vec(gate_bias), p_a, p_b1, p_b2, p_c, p_d, p_o)

        kv = _norm_matmul(mems, vec(mem_norm), w_kv, layer=l, tm=mems.shape[0], tn=512, name="mem_kv")
        xs = _xattn(xs, bsz, l, vec(xattn_norm), x_wq, kv, x_wo)

        xs = _ffn(xs, vec(ffn2_norm), ffn2_w_gate, ffn2_w_up, ffn2_w_down, fin, layer=l,
                  final=(l == depth - 1))
    return xs.reshape(bsz, t, d)
```

```python
import functools
import math

import jax
import jax.numpy as jnp
import numpy as np
from jax import lax
from jax.experimental import pallas as pl
from jax.experimental.pallas import tpu as pltpu

F32 = jnp.float32
BF16 = jnp.bfloat16

D_MODEL = 1024
D_FF = 11 * D_MODEL // 4
NORM_EPS = 1e-6
FFN_HALF = 0.5
N_BRANCH = 4
WIDTH = D_MODEL // 2

MLSTM_HEADS = 4
MLSTM_HEAD_DIM = WIDTH // MLSTM_HEADS
MLSTM_CONV = 4

S5_GROUP = 16
S5_GROUPS = WIDTH // S5_GROUP
S5_STATE = 64
S5_MIN_NEG = 1e-4
S5_HALF = WIDTH // 2
S5_BLOCK_GROUPS = 8
S5_BLOCK_CH = S5_BLOCK_GROUPS * S5_GROUP
S5_BLOCK_STATES = S5_BLOCK_GROUPS * S5_STATE
S5_STEPS = 64
SUBLANES = 8

GLA_HEADS = 4
GLA_KEY_WIDTH = WIDTH // 2
GLA_HEAD_K = GLA_KEY_WIDTH // GLA_HEADS
GLA_HEAD_V = WIDTH // GLA_HEADS
GLA_GATE_RANK = 16
GLA_GATE_TAU = 16.0
GLA_CHUNK = 64

RWKV_HEAD = 64
RWKV_HEADS = WIDTH // RWKV_HEAD
RWKV_DECAY_RANK = 64
RWKV_ICLR_RANK = 64
RWKV_GATE_RANK = 128
RWKV_TAIL = RWKV_DECAY_RANK + RWKV_ICLR_RANK + RWKV_GATE_RANK
RWKV_GN_EPS = 64e-5
RWKV_CHUNK = 64
RWKV_GROUP_LANES = 256
RWKV_GROUP_HEADS = RWKV_GROUP_LANES // RWKV_HEAD

XATTN_HEADS = 4
XATTN_HEAD_DIM = D_MODEL // XATTN_HEADS

ZC_GATE = 0
ZC_AU = 4096
ZC_AO = 4608
ZC_BU = 5120
ZC_CV = 5632
ZC_CG = 6144
ZC_DR = 6656
ZC_DK = 7168
ZC_DV = 7680
ZC_CQ = 8192
ZC_CK = 8448
ZC_DT = 8704
ZC_SM = 8960
Z_COLS = 9216
SMALL = 128

TIME_BLOCK = 256
VMEM_LIMIT = 56 * 1024 * 1024

_NN = (((1,), (0,)), ((), ()))
_NT = (((1,), (1,)), ((), ()))


def _split(x, n):
    parts = []
    rest = x
    for i in range(n):
        p = rest.astype(BF16)
        parts.append(p)
        if i + 1 < n:
            rest = rest - p.astype(F32)
    return parts


def _mm(a, b, dn=_NN, pa=1, pb=1):
    ap, bp = _split(a, pa), _split(b, pb)
    order = max(pa, pb)
    out = None
    for i, x in enumerate(ap):
        for j, y in enumerate(bp):
            if i + j < order:
                t = lax.dot_general(x, y, dn, preferred_element_type=F32)
                out = t if out is None else out + t
    return out


def _rms(x, g):
    return x * lax.rsqrt(jnp.mean(x * x, axis=-1, keepdims=True) + NORM_EPS) * g


def _sigmoid(x):
    return 1.0 / (1.0 + jnp.exp(-x))


def _log_sigmoid(x):
    return jnp.minimum(x, 0.0) - jnp.log(1.0 + jnp.exp(-jnp.abs(x)))


def _iota(shape, dim):
    return lax.broadcasted_iota(jnp.int32, shape, dim)


def _tri(n, block, upper=False, strict=False):
    r, c = _iota((n, n), 0), _iota((n, n), 1)
    same = (r // block) == (c // block)
    if upper:
        keep = (r < c) if strict else (r <= c)
    else:
        keep = (r > c) if strict else (r >= c)
    return jnp.where(same & keep, 1.0, 0.0).astype(BF16)


def _shift_rows(prev_ref, x, k):
    ext = jnp.concatenate([prev_ref[...], x], axis=0)
    return pltpu.roll(ext, k, 0)[prev_ref.shape[0]:]


def _layer_spec(layer, *shape):
    return pl.BlockSpec((None,) + shape, lambda *_: (layer,) + (0,) * len(shape))


def _ffn_kernel(x_ref, g_ref, wg_ref, wu_ref, wd_ref, fg_ref, o_ref, *, final, tf):
    x = x_ref[...]
    xn = _rms(x, g_ref[...]).astype(BF16)
    acc = None
    for j in range(wg_ref.shape[1] // tf):
        cols = slice(j * tf, (j + 1) * tf)
        gate = _mm(xn, wg_ref[:, cols])
        up = _mm(xn, wu_ref[:, cols])
        part = _mm(gate * _sigmoid(gate) * up, wd_ref[cols, :])
        acc = part if acc is None else acc + part
    y = x + FFN_HALF * acc
    if final:
        y = _rms(y, fg_ref[...])
    o_ref[...] = y


def _ffn(x, gain, wg, wu, wd, final_gain, *, layer, final, tm=512, tf=256):
    n, d = x.shape
    f = wg.shape[2]
    resident = lambda *shape: pl.BlockSpec((None,) + shape, lambda i: (layer, 0, 0), pipeline_mode=pl.Buffered(1))
    return pl.pallas_call(
        functools.partial(_ffn_kernel, final=final, tf=tf),
        grid=(n // tm,),
        in_specs=[
            pl.BlockSpec((tm, d), lambda i: (i, 0)),
            _layer_spec(layer, 1, d),
            resident(d, f), resident(d, f), resident(f, d),
            pl.BlockSpec((1, d), lambda i: (0, 0)),
        ],
        out_specs=pl.BlockSpec((tm, d), lambda i: (i, 0)),
        out_shape=jax.ShapeDtypeStruct((n, d), F32),
        compiler_params=pltpu.CompilerParams(
            dimension_semantics=("parallel",), vmem_limit_bytes=VMEM_LIMIT),
        name="ffn",
    )(x, gain, wg, wu, wd, final_gain)


def _norm_matmul_kernel(x_ref, g_ref, w_ref, o_ref, xn_ref):
    @pl.when(pl.program_id(1) == 0)
    def _():
        xn_ref[...] = _rms(x_ref[...], g_ref[...]).astype(BF16)

    o_ref[...] = _mm(xn_ref[...], w_ref[...]).astype(o_ref.dtype)


def _norm_matmul(x, gain, w, *, layer, tm, tn, name, out_dtype):
    n, d = x.shape
    m = w.shape[2]
    return pl.pallas_call(
        _norm_matmul_kernel,
        grid=(n // tm, m // tn),
        in_specs=[
            pl.BlockSpec((tm, d), lambda i, j: (i, 0)),
            _layer_spec(layer, 1, d),
            pl.BlockSpec((None, d, tn), lambda i, j: (layer, 0, j)),
        ],
        out_specs=pl.BlockSpec((tm, tn), lambda i, j: (i, j)),
        out_shape=jax.ShapeDtypeStruct((n, m), out_dtype),
        scratch_shapes=[pltpu.VMEM((tm, d), BF16)],
        compiler_params=pltpu.CompilerParams(
            dimension_semantics=("parallel", "arbitrary"), vmem_limit_bytes=VMEM_LIMIT),
        name=name,
    )(x, gain, w)


def _mlstm_kernel(u_ref, o_ref, sm_ref, conv_ref, wq_ref, wk_ref, wv_ref, brow_ref, bcol_ref, ng_ref,
                  out_ref, uprev_ref, c_ref, m_ref):
    lb = u_ref.shape[0]
    e = MLSTM_HEAD_DIM

    @pl.when(pl.program_id(1) == 0)
    def _():
        uprev_ref[...] = jnp.zeros_like(uprev_ref)
        c_ref[...] = jnp.zeros_like(c_ref)
        m_ref[...] = jnp.zeros_like(m_ref)

    u = u_ref[...].astype(F32)
    conv = None
    for j in range(MLSTM_CONV):
        k = MLSTM_CONV - 1 - j
        term = conv_ref[j:j + 1, :] * (_shift_rows(uprev_ref, u, k) if k else u)
        conv = term if conv is None else conv + term
    uprev_ref[...] = u[lb - uprev_ref.shape[0]:]
    uc = conv * _sigmoid(conv)

    sm = sm_ref[...].astype(F32)
    pre = sm + brow_ref[...]
    log_f = _log_sigmoid(pre)
    pre_t = sm.T[0:8, :] + bcol_ref[0:8, :]
    log_f_t = _log_sigmoid(pre_t)
    b_cols = _mm(_tri(lb, lb), log_f, pb=3)
    b_rows = _mm(log_f_t, _tri(lb, lb, upper=True), pa=3)

    causal = _iota((lb, lb), 0) >= _iota((lb, lb), 1)
    heads = range(MLSTM_HEADS)
    sls = [slice(h * e, (h + 1) * e) for h in heads]
    ones = jnp.ones((lb, e), BF16)
    q = [_mm(uc[:, sls[h]], wq_ref[h]).astype(BF16) for h in heads]
    k = [_mm(uc[:, sls[h]], wk_ref[h]) * e ** -0.5 for h in heads]
    v = [jnp.concatenate([_mm(u[:, sls[h]], wv_ref[h]).astype(BF16), ones], axis=1) for h in heads]
    bc = [b_cols[:, MLSTM_HEADS + h:MLSTM_HEADS + h + 1] for h in heads]
    m_st = [m_ref[h:h + 1, 0:1] for h in heads]
    dmat = [jnp.where(causal, bc[h] - b_rows[MLSTM_HEADS + h:MLSTM_HEADS + h + 1, :] + pre_t[h:h + 1, :], -jnp.inf)
            for h in heads]
    m_inter = [bc[h] + m_st[h] for h in heads]
    m_row = [jnp.maximum(m_inter[h], jnp.max(dmat[h], axis=-1, keepdims=True)) for h in heads]
    qk = [_mm(q[h], k[h], _NT) for h in heads]
    w = [jnp.exp(dmat[h] - m_row[h]) * qk[h] for h in heads]
    from_state = [_mm(q[h], c_ref[h]) for h in heads]
    from_chunk = [_mm(w[h], v[h]) for h in heads]
    for h in heads:
        tot = jnp.exp(m_inter[h] - m_row[h]) * from_state[h] + from_chunk[h]
        hh = tot[:, :e] / jnp.maximum(jnp.abs(tot[:, e:e + 1]), jnp.exp(-m_row[h]))
        hn = hh * lax.rsqrt(jnp.mean(hh * hh, axis=-1, keepdims=True) + NORM_EPS) * ng_ref[:, sls[h]]
        out_ref[:, sls[h]] = (hn * _sigmoid(o_ref[:, sls[h]].astype(F32))).astype(out_ref.dtype)
    for h in heads:
        b_last = bc[h][lb - 1:lb, :]
        e_col = b_last - bc[h] + pre[:, h:h + 1]
        m_new = jnp.maximum(b_last + m_st[h], jnp.max(e_col, axis=0, keepdims=True))
        decay = jnp.exp(b_last + m_st[h] - m_new)
        wk = jnp.exp(e_col - m_new) * k[h]
        c_ref[h] = decay * c_ref[h] + _mm(wk.T, v[h])
        m_ref[h:h + 1, :] = jnp.broadcast_to(m_new, (1, m_ref.shape[1]))


def _mlstm(z, bsz, layer, conv_w, wq, wk, wv, b_row, b_col, norm_g):
    n = z.shape[0]
    lb = TIME_BLOCK
    nblk = n // bsz // lb
    row = lambda b, c: b * nblk + c
    hd = MLSTM_HEAD_DIM
    return pl.pallas_call(
        _mlstm_kernel,
        grid=(bsz, nblk),
        in_specs=[
            pl.BlockSpec((lb, WIDTH), lambda b, c: (row(b, c), ZC_AU // WIDTH)),
            pl.BlockSpec((lb, WIDTH), lambda b, c: (row(b, c), ZC_AO // WIDTH)),
            pl.BlockSpec((lb, SMALL), lambda b, c: (row(b, c), ZC_SM // SMALL)),
            _layer_spec(layer, MLSTM_CONV, WIDTH),
            _layer_spec(layer, MLSTM_HEADS, hd, hd),
            _layer_spec(layer, MLSTM_HEADS, hd, hd),
            _layer_spec(layer, MLSTM_HEADS, hd, hd),
            _layer_spec(layer, 1, SMALL),
            _layer_spec(layer, SMALL, 1),
            _layer_spec(layer, 1, WIDTH),
        ],
        out_specs=pl.BlockSpec((lb, WIDTH), lambda b, c: (row(b, c), 0)),
        out_shape=jax.ShapeDtypeStruct((n, WIDTH), BF16),
        scratch_shapes=[
            pltpu.VMEM((8, WIDTH), F32),
            pltpu.VMEM((MLSTM_HEADS, hd, 2 * hd), F32),
            pltpu.VMEM((8, hd), F32),
        ],
        compiler_params=pltpu.CompilerParams(
            dimension_semantics=("parallel", "arbitrary"), vmem_limit_bytes=VMEM_LIMIT),
        name="mlstm",
    )(z, z, z, conv_w, wq, wk, wv, b_row, b_col, norm_g)


def _s5_kernel(u_ref, pin_ref, pout_ref, win_ref, wout_ref, lre_ref, lim_ref, d_ref, out_ref, x_ref, c_ref, *, bsz):
    steps = u_ref.shape[1]
    rows = steps * SUBLANES
    ns = S5_BLOCK_STATES
    cw = S5_BLOCK_CH
    nblk = S5_HALF // cw

    @pl.when(pl.program_id(0) == 0)
    def _():
        c_ref[...] = jnp.zeros_like(c_ref)

    u_bt = u_ref[...].reshape(bsz * steps, WIDTH)
    u = _mm(pin_ref[0], u_bt[:, :S5_HALF]) + _mm(pin_ref[1], u_bt[:, S5_HALF:])
    upper = (_iota((rows, 1), 0) % SUBLANES) >= bsz
    for j in range(nblk):
        uj = u[:, j * cw:(j + 1) * cw]
        uext = jnp.concatenate([jnp.where(upper, 0.0, uj), jnp.where(upper, uj, 0.0)], axis=1)
        x_ref[:, j * 2 * ns:(j + 1) * 2 * ns] = _mm(uext, win_ref[j])

    lre, lim = lre_ref[...], lim_ref[...]

    def step(t, carry):
        r0 = pl.multiple_of(t * SUBLANES, SUBLANES)
        new = []
        for j in range(nblk):
            sr, si = carry[2 * j], carry[2 * j + 1]
            lr, li = lre[:, j * ns:(j + 1) * ns], lim[:, j * ns:(j + 1) * ns]
            re_cols = slice(j * 2 * ns, j * 2 * ns + ns)
            im_cols = slice(j * 2 * ns + ns, (j + 1) * 2 * ns)
            nr = lr * sr - li * si + x_ref[pl.ds(r0, SUBLANES), re_cols]
            ni = lr * si + li * sr + x_ref[pl.ds(r0, SUBLANES), im_cols]
            x_ref[pl.ds(r0, SUBLANES), re_cols] = nr
            x_ref[pl.ds(r0, SUBLANES), im_cols] = ni
            new += [nr, ni]
        return tuple(new)

    init = tuple(c_ref[:, k * ns:(k + 1) * ns] for k in range(2 * nblk))
    last = lax.fori_loop(0, steps, step, init, unroll=8)
    for k in range(2 * nblk):
        c_ref[:, k * ns:(k + 1) * ns] = last[k]

    ys = []
    for j in range(nblk):
        res = _mm(x_ref[:, j * 2 * ns:(j + 1) * 2 * ns], wout_ref[j])
        cols = slice(j * cw, (j + 1) * cw)
        y = jnp.where(upper, res[:, cw:], res[:, :cw]) + d_ref[:, cols] * u[:, cols]
        ys.append(0.5 * y * (1.0 + jnp.tanh(math.sqrt(2.0 / math.pi) * (y + 0.044715 * y * y * y))))
    y = jnp.concatenate(ys, axis=1).astype(BF16)
    out = jnp.concatenate([_mm(pout_ref[0], y), _mm(pout_ref[1], y)], axis=1)
    out_ref[...] = out.astype(out_ref.dtype).reshape(bsz, steps, WIDTH)


def _s5_permutations(bsz, steps):
    rows = steps * 2 * bsz
    r = np.arange(rows)
    p = np.zeros((2, rows, bsz * steps), np.float32)
    p[(r % (2 * bsz)) // bsz, r, (r % bsz) * steps + r // (2 * bsz)] = 1.0
    return jnp.asarray(p, BF16), jnp.asarray(p.transpose(0, 2, 1), BF16)


def _s5(z, bsz, layer, w_in, w_out, lam_re, lam_im, d_tab):
    t = z.shape[1]
    steps = S5_STEPS
    rows = steps * SUBLANES
    nblk = S5_HALF // S5_BLOCK_CH
    nst = nblk * S5_BLOCK_STATES
    p_in, p_out = _s5_permutations(bsz, steps)
    whole = lambda *shape: pl.BlockSpec(shape, lambda i: (0,) * len(shape))
    return pl.pallas_call(
        functools.partial(_s5_kernel, bsz=bsz),
        grid=(t // steps,),
        in_specs=[
            pl.BlockSpec((bsz, steps, WIDTH), lambda i: (0, i, ZC_BU // WIDTH)),
            whole(2, rows, bsz * steps),
            whole(2, bsz * steps, rows),
            _layer_spec(layer, nblk, 2 * S5_BLOCK_CH, 2 * S5_BLOCK_STATES),
            _layer_spec(layer, nblk, 2 * S5_BLOCK_STATES, 2 * S5_BLOCK_CH),
            _layer_spec(layer, SUBLANES, nst),
            _layer_spec(layer, SUBLANES, nst),
            _layer_spec(layer, rows, S5_HALF),
        ],
        out_specs=pl.BlockSpec((bsz, steps, WIDTH), lambda i: (0, i, 0)),
        out_shape=jax.ShapeDtypeStruct((bsz, t, WIDTH), BF16),
        scratch_shapes=[pltpu.VMEM((rows, 2 * nst), F32), pltpu.VMEM((SUBLANES, 2 * nst), F32)],
        compiler_params=pltpu.CompilerParams(
            dimension_semantics=("arbitrary",), vmem_limit_bytes=VMEM_LIMIT),
        name="s5",
    )(z, p_in, p_out, w_in, w_out, lam_re, lam_im, d_tab)


def _gla_kernel(q_ref, k_ref, v_ref, g_ref, sm_ref, aup_ref, ab_ref, ng_ref, out_ref, st_ref):
    lb = q_ref.shape[0]
    ch = GLA_CHUNK
    kw = GLA_KEY_WIDTH

    @pl.when(pl.program_id(1) == 0)
    def _():
        st_ref[...] = jnp.zeros_like(st_ref)

    x = _mm(sm_ref[...], aup_ref[...], pb=2) + ab_ref[...]
    log_a = _log_sigmoid(x) * (1.0 / GLA_GATE_TAU)
    b_all = _mm(_tri(lb, ch), log_a, pb=3)
    lane_head = _iota((1, kw), 1) // GLA_HEAD_K
    masks = [jnp.where(lane_head == h, 1.0, 0.0).astype(BF16) for h in range(GLA_HEADS)]
    causal = _iota((ch, ch), 0) >= _iota((ch, ch), 1)
    mid = ch // 2 - 1
    heads = range(GLA_HEADS)
    chunks = range(lb // ch)
    rss = [slice(c * ch, (c + 1) * ch) for c in chunks]
    vss = [slice(h * GLA_HEAD_V, (h + 1) * GLA_HEAD_V) for h in heads]

    q_in, q_at, k_at, k_st, dec = [], [], [], [], []
    for c in chunks:
        b = b_all[rss[c]]
        b_mid, b_last = b[mid:mid + 1, :], b[ch - 1:ch, :]
        q = q_ref[rss[c], :].astype(F32) * GLA_HEAD_K ** -0.5
        k = k_ref[rss[c], :].astype(F32)
        q_in.append((q * jnp.exp(b)).astype(BF16))
        q_at.append((q * jnp.exp(b - b_mid)).astype(BF16))
        k_at.append((k * jnp.exp(b_mid - b)).astype(BF16))
        k_st.append((k * jnp.exp(b_last - b)).astype(BF16))
        dec.append(jnp.exp(b_last))
    v = [[v_ref[rss[c], vss[h]] for h in heads] for c in chunks]
    attn = [[jnp.where(causal, _mm(q_at[c] * masks[h], k_at[c], _NT), 0.0) for h in heads] for c in chunks]
    grow = [[_mm(v[c][h].astype(F32).T, k_st[c] * masks[h]) for h in heads] for c in chunks]
    state = [st_ref[h] for h in heads]
    for c in chunks:
        for h in heads:
            o = _mm(q_in[c] * masks[h], state[h], _NT) + _mm(attn[c][h], v[c][h])
            state[h] = dec[c] * state[h] + grow[c][h]
            on = o * lax.rsqrt(jnp.mean(o * o, axis=-1, keepdims=True) + NORM_EPS) * ng_ref[:, vss[h]]
            g = g_ref[rss[c], vss[h]].astype(F32)
            out_ref[rss[c], vss[h]] = (on * (g * _sigmoid(g))).astype(out_ref.dtype)
    for h in heads:
        st_ref[h] = state[h]


def _gla(z, bsz, layer, a_up, a_bias, norm_g):
    n = z.shape[0]
    lb = TIME_BLOCK
    nblk = n // bsz // lb
    row = lambda b, c: b * nblk + c
    return pl.pallas_call(
        _gla_kernel,
        grid=(bsz, nblk),
        in_specs=[
            pl.BlockSpec((lb, GLA_KEY_WIDTH), lambda b, c: (row(b, c), ZC_CQ // GLA_KEY_WIDTH)),
            pl.BlockSpec((lb, GLA_KEY_WIDTH), lambda b, c: (row(b, c), ZC_CK // GLA_KEY_WIDTH)),
            pl.BlockSpec((lb, WIDTH), lambda b, c: (row(b, c), ZC_CV // WIDTH)),
            pl.BlockSpec((lb, WIDTH), lambda b, c: (row(b, c), ZC_CG // WIDTH)),
            pl.BlockSpec((lb, SMALL), lambda b, c: (row(b, c), ZC_SM // SMALL)),
            _layer_spec(layer, SMALL, GLA_KEY_WIDTH),
            _layer_spec(layer, 1, GLA_KEY_WIDTH),
            _layer_spec(layer, 1, WIDTH),
        ],
        out_specs=pl.BlockSpec((lb, WIDTH), lambda b, c: (row(b, c), 0)),
        out_shape=jax.ShapeDtypeStruct((n, WIDTH), BF16),
        scratch_shapes=[pltpu.VMEM((GLA_HEADS, GLA_HEAD_V, GLA_KEY_WIDTH), F32)],
        compiler_params=pltpu.CompilerParams(
            dimension_semantics=("parallel", "arbitrary"), vmem_limit_bytes=VMEM_LIMIT),
        name="gla",
    )(z, z, z, z, z, a_up, a_bias, norm_g)


def _rwkv_kernel(r_ref, k_ref, v_ref, t_ref, mu_ref, mut_ref,
                 w0_ref, wup_ref, a0_ref, aup_ref, gup_ref, kk_ref, ka_ref, rk_ref, ng_ref,
                 out_ref, pr_ref, pk_ref, pv_ref, pt_ref, st_ref, y_ref):
    lb = r_ref.shape[0]
    ch = RWKV_CHUNK
    gl = RWKV_GROUP_LANES
    gh = RWKV_GROUP_HEADS
    rows_s = gh * ch

    @pl.when(pl.program_id(1) == 0)
    def _():
        for ref in (pr_ref, pk_ref, pv_ref, pt_ref, st_ref):
            ref[...] = jnp.zeros_like(ref)

    def shift_mix(x_ref, prev_ref, mu):
        x = x_ref[...].astype(F32)
        xs = _shift_rows(prev_ref, x, 1)
        prev_ref[...] = x[lb - prev_ref.shape[0]:]
        return x + mu * (xs - x)

    r = shift_mix(r_ref, pr_ref, mu_ref[0:1, :])
    k = shift_mix(k_ref, pk_ref, mu_ref[1:2, :])
    v = shift_mix(v_ref, pv_ref, mu_ref[2:3, :])
    tl = shift_mix(t_ref, pt_ref, mut_ref[...])

    ones_bd = _tri(WIDTH, RWKV_HEAD) + _tri(WIDTH, RWKV_HEAD, upper=True, strict=True)
    w_log = _log_sigmoid(w0_ref[...] + _mm(jnp.tanh(tl), wup_ref[...])) - 0.5
    log_w = -jnp.exp(w_log)
    a = _sigmoid(a0_ref[...] + _mm(tl, aup_ref[...]))
    g = _mm(_sigmoid(tl), gup_ref[...])
    kk = k * kk_ref[...]
    kk = kk / jnp.maximum(jnp.sqrt(_mm(kk * kk, ones_bd, pa=3)), 1e-12)
    k_rep = k * (1.0 + (a - 1.0) * ka_ref[...])
    av = -kk
    bv = kk * a
    bonus = _mm(r * k_rep * rk_ref[...], ones_bd, pa=3) * v
    gc_all = _mm(_tri(lb, ch), log_w, pb=3)

    lane_head = _iota((1, gl), 1) // RWKV_HEAD
    masks = [jnp.where(lane_head == h, 1.0, 0.0).astype(BF16) for h in range(gh)]

    def stack(x):
        xb = x.astype(BF16)
        return jnp.concatenate([xb * m for m in masks], axis=0)

    t_row = _iota((rows_s, rows_s), 0) % ch
    t_col = _iota((rows_s, rows_s), 1) % ch
    strict = t_row > t_col
    incl = t_row >= t_col
    eye = jnp.where(_iota((rows_s, rows_s), 0) == _iota((rows_s, rows_s), 1), 1.0, 0.0)

    pairs = [(gi, c) for gi in range(WIDTH // gl) for c in range(lb // ch)]
    pre = {}
    for gi, c in pairs:
        ls = slice(gi * gl, (gi + 1) * gl)
        rs = slice(c * ch, (c + 1) * ch)
        gc = gc_all[rs, ls]
        g_last = gc[ch - 1:ch, :]
        e_inv = jnp.exp(-gc)
        e_end = jnp.exp(g_last - gc)
        r_t = stack(r[rs, ls] * jnp.exp(gc))
        a_t = stack(av[rs, ls] * jnp.exp(gc - log_w[rs, ls]))
        b_t = stack(bv[rs, ls] * e_inv)
        k_t = stack(k_rep[rs, ls] * e_inv)
        mm = _mm(jnp.concatenate([a_t, r_t], axis=0), jnp.concatenate([b_t, k_t], axis=0), _NT)
        pre[gi, c] = dict(
            r_t=r_t, a_t=a_t, v_s=stack(v[rs, ls]), dec=jnp.exp(g_last),
            b_e=stack(bv[rs, ls] * e_end), k_e=stack(k_rep[rs, ls] * e_end),
            a_ab=jnp.where(strict, mm[:rows_s, :rows_s], 0.0),
            a_ak=jnp.where(strict, mm[:rows_s, rows_s:], 0.0).astype(BF16),
            a_rb=jnp.where(incl, mm[rows_s:, :rows_s], 0.0).astype(BF16),
            a_rk=jnp.where(incl, mm[rows_s:, rows_s:], 0.0).astype(BF16))
    tinv = {key: eye + d["a_ab"] for key, d in pre.items()}
    power = {key: d["a_ab"].astype(BF16) for key, d in pre.items()}
    step = 2
    while step < ch:
        for key in pairs:
            power[key] = _mm(power[key], power[key]).astype(BF16)
        for key in pairs:
            tinv[key] = tinv[key] + _mm(tinv[key], power[key])
        step *= 2

    def finish_phase1(key):
        d = pre[key]
        tb = tinv[key].astype(BF16)
        d["a_hat"] = _mm(tb, d["a_t"]).astype(BF16)
        d["u0"] = _mm(tb, _mm(d["a_ak"], d["v_s"]))
        d["y0"] = _mm(d["a_rk"], d["v_s"])
        d["s0"] = _mm(d["v_s"].astype(F32).T, d["k_e"])

    groups = range(WIDTH // gl)
    state = [st_ref[gi] for gi in groups]

    def solve_u(gi, c):
        d = pre[gi, c]
        d["st"] = state[gi].astype(BF16)
        return _mm(d["a_hat"], d["st"], _NT) + d["u0"]

    def emit_and_advance(gi, c, u_s):
        d = pre[gi, c]
        y_s = _mm(d["r_t"], d["st"], _NT) + _mm(d["a_rb"], u_s) + d["y0"]
        y = y_s[0:ch]
        for h in range(1, gh):
            y = y + y_s[h * ch:(h + 1) * ch]
        y_ref[c * ch:(c + 1) * ch, gi * gl:(gi + 1) * gl] = y
        state[gi] = d["dec"] * state[gi] + _mm(u_s.T, d["b_e"]) + d["s0"]

    n_chunks = lb // ch
    for c in range(n_chunks + 1):
        u_prev = {}
        for gi in groups:
            if c < n_chunks:
                finish_phase1((gi, c))
            if c > 0:
                u_prev[gi] = solve_u(gi, c - 1)
        if c > 0:
            for gi in groups:
                emit_and_advance(gi, c - 1, u_prev[gi])
    for gi in groups:
        st_ref[gi] = state[gi]

    y = y_ref[...]
    inv_n = 1.0 / RWKV_HEAD
    mu = _mm(y, ones_bd, pa=3) * inv_n
    yc = y - mu
    var = _mm(yc * yc, ones_bd, pa=3) * inv_n
    yn = yc * lax.rsqrt(var + RWKV_GN_EPS) * ng_ref[...]
    out_ref[...] = ((yn + bonus) * g).astype(out_ref.dtype)


def _rwkv(z, bsz, layer, mu_rkv, mu_t, w0, w_up, a0, a_up, g_up, k_k, k_a, r_k, norm_g):
    n = z.shape[0]
    lb = TIME_BLOCK
    nblk = n // bsz // lb
    row = lambda b, c: b * nblk + c
    vec = _layer_spec(layer, 1, WIDTH)
    mat = _layer_spec(layer, RWKV_TAIL, WIDTH)
    return pl.pallas_call(
        _rwkv_kernel,
        grid=(bsz, nblk),
        in_specs=[
            pl.BlockSpec((lb, WIDTH), lambda b, c: (row(b, c), ZC_DR // WIDTH)),
            pl.BlockSpec((lb, WIDTH), lambda b, c: (row(b, c), ZC_DK // WIDTH)),
            pl.BlockSpec((lb, WIDTH), lambda b, c: (row(b, c), ZC_DV // WIDTH)),
            pl.BlockSpec((lb, RWKV_TAIL), lambda b, c: (row(b, c), ZC_DT // RWKV_TAIL)),
            _layer_spec(layer, 3, WIDTH), _layer_spec(layer, 1, RWKV_TAIL),
            vec, mat, vec, mat, mat, vec, vec, vec, vec,
        ],
        out_specs=pl.BlockSpec((lb, WIDTH), lambda b, c: (row(b, c), 0)),
        out_shape=jax.ShapeDtypeStruct((n, WIDTH), BF16),
        scratch_shapes=[
            pltpu.VMEM((8, WIDTH), F32), pltpu.VMEM((8, WIDTH), F32), pltpu.VMEM((8, WIDTH), F32),
            pltpu.VMEM((8, RWKV_TAIL), F32),
            pltpu.VMEM((WIDTH // RWKV_GROUP_LANES, RWKV_GROUP_LANES, RWKV_GROUP_LANES), F32),
            pltpu.VMEM((lb, WIDTH), F32),
        ],
        compiler_params=pltpu.CompilerParams(
            dimension_semantics=("parallel", "arbitrary"), vmem_limit_bytes=VMEM_LIMIT),
        name="rwkv",
    )(z, z, z, z, mu_rkv, mu_t, w0, w_up, a0, a_up, g_up, k_k, k_a, r_k, norm_g)


def _merge_kernel(x_ref, ya_ref, zb_ref, yc_ref, yd_ref, g0_ref, g1_ref, g2_ref, g3_ref, gb_ref,
                  pa_ref, w1_ref, w2_ref, pc_ref, pd_ref, wo_ref, out_ref):
    d = D_MODEL

    def gate(g_ref, i):
        return _sigmoid(g_ref[...].astype(F32) + gb_ref[:, i * d:(i + 1) * d])

    zb = zb_ref[...]
    merged = gate(g0_ref, 0) * _mm(ya_ref[...], pa_ref[...])
    merged = merged + gate(g1_ref, 1) * (_mm(zb, w1_ref[...]) * _sigmoid(_mm(zb, w2_ref[...])))
    merged = merged + gate(g2_ref, 2) * _mm(yc_ref[...], pc_ref[...])
    merged = merged + gate(g3_ref, 3) * _mm(yd_ref[...], pd_ref[...])
    out_ref[...] = x_ref[...] + _mm(merged, wo_ref[...])


def _merge(x, z, ya, zb, yc, yd, layer, gate_bias, pa, w1, w2, pc, pd, wo, *, tm=256):
    n, d = x.shape
    tile = lambda w: pl.BlockSpec((tm, w), lambda i: (i, 0))
    gate = lambda b: pl.BlockSpec((tm, d), lambda i: (i, ZC_GATE // d + b))
    proj = _layer_spec(layer, WIDTH, d)
    return pl.pallas_call(
        _merge_kernel,
        grid=(n // tm,),
        in_specs=[tile(d), tile(WIDTH), tile(WIDTH), tile(WIDTH), tile(WIDTH),
                  gate(0), gate(1), gate(2), gate(3), _layer_spec(layer, 1, N_BRANCH * d),
                  proj, proj, proj, proj, proj, _layer_spec(layer, d, d)],
        out_specs=tile(d),
        out_shape=jax.ShapeDtypeStruct((n, d), F32),
        compiler_params=pltpu.CompilerParams(
            dimension_semantics=("parallel",), vmem_limit_bytes=VMEM_LIMIT),
        name="merge",
    )(x, ya, zb, yc, yd, z, z, z, z, gate_bias, pa, w1, w2, pc, pd, wo)


def _xattn_kernel(x_ref, g_ref, wq_ref, kv_ref, wo_ref, out_ref):
    d = D_MODEL
    hd = XATTN_HEAD_DIM
    x = x_ref[...]
    q = _mm(_rms(x, g_ref[...]), wq_ref[...])
    kv = kv_ref[...]
    outs = []
    for h in range(XATTN_HEADS):
        sl = slice(h * hd, (h + 1) * hd)
        s = _mm(q[:, sl], kv[:, sl], _NT) * hd ** -0.5
        s = s - jnp.max(s, axis=-1, keepdims=True)
        p = jnp.exp(s)
        p = p / jnp.sum(p, axis=-1, keepdims=True)
        outs.append(_mm(p, kv[:, d + h * hd:d + (h + 1) * hd]))
    o = jnp.concatenate(outs, axis=1)
    out_ref[...] = x + _mm(o, wo_ref[...])


def _xattn(x, bsz, layer, gain, wq, kv, wo, *, tm=512):
    n, d = x.shape
    nblk = n // bsz // tm
    m_len = kv.shape[0] // bsz
    return pl.pallas_call(
        _xattn_kernel,
        grid=(bsz, nblk),
        in_specs=[
            pl.BlockSpec((tm, d), lambda b, c: (b * nblk + c, 0)),
            _layer_spec(layer, 1, d),
            _layer_spec(layer, d, d),
            pl.BlockSpec((m_len, 2 * d), lambda b, c: (b, 0)),
            _layer_spec(layer, d, d),
        ],
        out_specs=pl.BlockSpec((tm, d), lambda b, c: (b * nblk + c, 0)),
        out_shape=jax.ShapeDtypeStruct((n, d), F32),
        compiler_params=pltpu.CompilerParams(
            dimension_semantics=("parallel", "parallel"), vmem_limit_bytes=VMEM_LIMIT),
        name="xattn",
    )(x, gain, wq, kv, wo)


def _regroup_w_in(w):
    o = np.cumsum((0, WIDTH, WIDTH, MLSTM_HEADS, MLSTM_HEADS, WIDTH, GLA_KEY_WIDTH, GLA_KEY_WIDTH, WIDTH, WIDTH,
                   GLA_GATE_RANK, 3 * WIDTH + RWKV_TAIL, N_BRANCH * D_MODEL))
    a_u, a_o, a_i, a_f, b_u, c_q, c_k, c_v, c_g, c_a, d_z, gate = (w[..., o[i]:o[i + 1]] for i in range(12))
    d_r, d_k, d_v, d_t = (d_z[..., :WIDTH], d_z[..., WIDTH:2 * WIDTH], d_z[..., 2 * WIDTH:3 * WIDTH],
                          d_z[..., 3 * WIDTH:])
    used = Z_COLS - 2 * SMALL + 2 * MLSTM_HEADS + GLA_GATE_RANK
    pad = jnp.zeros(w.shape[:-1] + (Z_COLS - used,), w.dtype)
    cols = jnp.concatenate([gate, a_u, a_o, b_u, c_v, c_g, d_r, d_k, d_v, c_q, c_k, d_t, a_i, a_f, c_a, pad], axis=-1)
    return cols.astype(BF16)


def _s5_tables(a_re, a_im, log_step, b_re, b_im, c_re, c_im, d_skip, bsz):
    depth = a_re.shape[0]
    step = jnp.exp(log_step)[..., None]
    lam_re = jnp.minimum(a_re, -S5_MIN_NEG)
    lam_im = a_im
    mag = jnp.exp(lam_re * step)
    bar_re = mag * jnp.cos(lam_im * step)
    bar_im = mag * jnp.sin(lam_im * step)
    denom = lam_re * lam_re + lam_im * lam_im
    coef_re = ((bar_re - 1.0) * lam_re + bar_im * lam_im) / denom
    coef_im = (bar_im * lam_re - (bar_re - 1.0) * lam_im) / denom
    bb_re = coef_re[..., None] * b_re - coef_im[..., None] * b_im
    bb_im = coef_re[..., None] * b_im + coef_im[..., None] * b_re
    nb = S5_HALF // S5_BLOCK_CH
    gpb = S5_BLOCK_GROUPS
    eye = jnp.eye(gpb, dtype=F32)

    def in_block(bb):
        bb = bb.reshape(depth, 2, nb, gpb, S5_STATE, S5_GROUP)
        return jnp.einsum('lhjgpc,gk->lhjgckp', bb, eye).reshape(depth, 2, nb, S5_BLOCK_CH, S5_BLOCK_STATES)

    def out_block(cc):
        cc = cc.reshape(depth, 2, nb, gpb, S5_GROUP, S5_STATE)
        return jnp.einsum('lhjgcp,gk->lhjgpkc', cc, eye).reshape(depth, 2, nb, S5_BLOCK_STATES, S5_BLOCK_CH)

    w_in = jnp.concatenate([in_block(bb_re), in_block(bb_im)], axis=-1)
    w_in = w_in.transpose(0, 2, 1, 3, 4).reshape(depth, nb, 2 * S5_BLOCK_CH, 2 * S5_BLOCK_STATES)
    w_out = jnp.concatenate([out_block(c_re), -out_block(c_im)], axis=-2)
    w_out = w_out.transpose(0, 2, 3, 1, 4).reshape(depth, nb, 2 * S5_BLOCK_STATES, 2 * S5_BLOCK_CH)

    def rows(t, width):
        t = t.reshape(depth, 2, 1, width)
        return jnp.broadcast_to(t, (depth, 2, bsz, width)).reshape(depth, 2 * bsz, width)

    nst = nb * S5_BLOCK_STATES
    lam_re_rows = rows(bar_re.reshape(depth, 2 * nst), nst)
    lam_im_rows = rows(bar_im.reshape(depth, 2 * nst), nst)
    d_tab = jnp.tile(rows(d_skip, S5_HALF), (1, S5_STEPS, 1))
    return w_in.astype(BF16), w_out.astype(BF16), lam_re_rows, lam_im_rows, d_tab


def _pad_rows(w, start, total):
    return jnp.pad(w, ((0, 0), (start, total - start - w.shape[1]), (0, 0)))


def kernel(x, mem, ffn1_norm, ffn1_w_gate, ffn1_w_up, ffn1_w_down, mix_norm, w_in, gate_bias, mlstm_conv, mlstm_wq, mlstm_wk, mlstm_wv, mlstm_b_i, mlstm_b_f, mlstm_norm, mlstm_proj, s5_a_re, s5_a_im, s5_log_step, s5_b_re, s5_b_im, s5_c_re, s5_c_im, s5_d, s5_glu_w1, s5_glu_w2, gla_a_up, gla_a_bias, gla_norm, gla_proj, rwkv_mu, rwkv_w0, rwkv_w_up, rwkv_a0, rwkv_a_up, rwkv_g_up, rwkv_k_k, rwkv_k_a, rwkv_r_k, rwkv_norm, rwkv_proj, w_out, xattn_norm, mem_norm, xattn_wq, xattn_wk, xattn_wv, xattn_wo, ffn2_norm, ffn2_w_gate, ffn2_w_up, ffn2_w_down, final_norm):
    bsz, t, d = x.shape
    depth = w_in.shape[0]
    assert d == D_MODEL and t % TIME_BLOCK == 0 and 2 * bsz == SUBLANES
    xs = x.reshape(bsz * t, d)
    mems = mem.reshape(bsz * mem.shape[1], d)
    bf = lambda w: w.astype(BF16)
    vec = lambda v: v.reshape(depth, 1, -1)
    fin = final_norm.reshape(1, d)

    w_in_r = _regroup_w_in(w_in)
    gate_b = jnp.pad(jnp.concatenate([mlstm_b_i, mlstm_b_f], axis=1), ((0, 0), (0, SMALL - 2 * MLSTM_HEADS)))
    s5_w_in, s5_w_out, s5_lre, s5_lim, s5_dtab = _s5_tables(
        s5_a_re, s5_a_im, s5_log_step, s5_b_re, s5_b_im, s5_c_re, s5_c_im, s5_d, bsz)
    gla_up = _pad_rows(gla_a_up, 2 * MLSTM_HEADS, SMALL)
    mu_rkv = rwkv_mu[:, :3 * WIDTH].reshape(depth, 3, WIDTH)
    mu_t = vec(rwkv_mu[:, 3 * WIDTH:])
    rw_wup = bf(_pad_rows(rwkv_w_up, 0, RWKV_TAIL))
    rw_aup = bf(_pad_rows(rwkv_a_up, RWKV_DECAY_RANK, RWKV_TAIL))
    rw_gup = bf(_pad_rows(rwkv_g_up, RWKV_DECAY_RANK + RWKV_ICLR_RANK, RWKV_TAIL))
    w_kv = bf(jnp.concatenate([xattn_wk, xattn_wv], axis=2))
    wq_a, wk_a, wv_a = bf(mlstm_wq), bf(mlstm_wk), bf(mlstm_wv)
    p_a, p_b1, p_b2, p_c, p_d, p_o = (bf(mlstm_proj), bf(s5_glu_w1), bf(s5_glu_w2), bf(gla_proj), bf(rwkv_proj),
                                      bf(w_out))
    x_wq, x_wo = bf(xattn_wq), bf(xattn_wo)
    f1_g, f1_u, f1_d = bf(ffn1_w_gate), bf(ffn1_w_up), bf(ffn1_w_down)
    f2_g, f2_u, f2_d = bf(ffn2_w_gate), bf(ffn2_w_up), bf(ffn2_w_down)

    for l in range(depth):
        xs = _ffn(xs, vec(ffn1_norm), f1_g, f1_u, f1_d, fin, layer=l, final=False)

        z = _norm_matmul(xs, vec(mix_norm), w_in_r, layer=l, tm=1024, tn=1536, name="in_proj",
                         out_dtype=BF16)

        y_a = _mlstm(z, bsz, l, mlstm_conv, wq_a, wk_a, wv_a, vec(gate_b), gate_b.reshape(depth, SMALL, 1),
                     vec(mlstm_norm))

        z_b = _s5(z.reshape(bsz, t, Z_COLS), bsz, l, s5_w_in, s5_w_out, s5_lre, s5_lim, s5_dtab)
        z_b = z_b.reshape(bsz * t, WIDTH)

        y_c = _gla(z, bsz, l, gla_up, vec(gla_a_bias), vec(gla_norm))

        y_d = _rwkv(z, bsz, l, mu_rkv, mu_t, vec(rwkv_w0), rw_wup, vec(rwkv_a0), rw_aup, rw_gup,
                    vec(rwkv_k_k), vec(rwkv_k_a), vec(rwkv_r_k), vec(rwkv_norm))

        xs = _merge(xs, z, y_a, z_b, y_c, y_d, l, vec(gate_bias), p_a, p_b1, p_b2, p_c, p_d, p_o)

        kv = _norm_matmul(mems, vec(mem_norm), w_kv, layer=l, tm=mems.shape[0], tn=512, name="mem_kv",
                          out_dtype=BF16)
        xs = _xattn(xs, bsz, l, vec(xattn_norm), x_wq, kv, x_wo)

        xs = _ffn(xs, vec(ffn2_norm), f2_g, f2_u, f2_d, fin, layer=l,
                  final=(l == depth - 1))
    return xs.reshape(bsz, t, d)
```

```python
import functools
import math

import jax
import jax.numpy as jnp
import numpy as np
from jax import lax
from jax.experimental import pallas as pl
from jax.experimental.pallas import tpu as pltpu

F32 = jnp.float32
BF16 = jnp.bfloat16

D_MODEL = 1024
D_FF = 11 * D_MODEL // 4
NORM_EPS = 1e-6
FFN_HALF = 0.5
N_BRANCH = 4
WIDTH = D_MODEL // 2

MLSTM_HEADS = 4
MLSTM_HEAD_DIM = WIDTH // MLSTM_HEADS
MLSTM_CONV = 4

S5_GROUP = 16
S5_GROUPS = WIDTH // S5_GROUP
S5_STATE = 64
S5_MIN_NEG = 1e-4
S5_HALF = WIDTH // 2
S5_BLOCK_GROUPS = 8
S5_BLOCK_CH = S5_BLOCK_GROUPS * S5_GROUP
S5_BLOCK_STATES = S5_BLOCK_GROUPS * S5_STATE
S5_STEPS = 64
SUBLANES = 8

GLA_HEADS = 4
GLA_KEY_WIDTH = WIDTH // 2
GLA_HEAD_K = GLA_KEY_WIDTH // GLA_HEADS
GLA_HEAD_V = WIDTH // GLA_HEADS
GLA_GATE_RANK = 16
GLA_GATE_TAU = 16.0
GLA_CHUNK = 64

RWKV_HEAD = 64
RWKV_HEADS = WIDTH // RWKV_HEAD
RWKV_DECAY_RANK = 64
RWKV_ICLR_RANK = 64
RWKV_GATE_RANK = 128
RWKV_TAIL = RWKV_DECAY_RANK + RWKV_ICLR_RANK + RWKV_GATE_RANK
RWKV_GN_EPS = 64e-5
RWKV_CHUNK = 64
RWKV_GROUP_LANES = 256
RWKV_GROUP_HEADS = RWKV_GROUP_LANES // RWKV_HEAD

XATTN_HEADS = 4
XATTN_HEAD_DIM = D_MODEL // XATTN_HEADS

ZC_GATE = 0
ZC_AU = 4096
ZC_AO = 4608
ZC_BU = 5120
ZC_CV = 5632
ZC_CG = 6144
ZC_DR = 6656
ZC_DK = 7168
ZC_DV = 7680
ZC_CQ = 8192
ZC_CK = 8448
ZC_DT = 8704
ZC_SM = 8960
Z_COLS = 9216
SMALL = 128

TIME_BLOCK = 256
VMEM_LIMIT = 56 * 1024 * 1024

_NN = (((1,), (0,)), ((), ()))
_NT = (((1,), (1,)), ((), ()))


def _split(x, n):
    parts = []
    rest = x
    for i in range(n):
        p = rest.astype(BF16)
        parts.append(p)
        if i + 1 < n:
            rest = rest - p.astype(F32)
    return parts


def _mm(a, b, dn=_NN, pa=1, pb=1):
    ap, bp = _split(a, pa), _split(b, pb)
    order = max(pa, pb)
    out = None
    for i, x in enumerate(ap):
        for j, y in enumerate(bp):
            if i + j < order:
                t = lax.dot_general(x, y, dn, preferred_element_type=F32)
                out = t if out is None else out + t
    return out


def _rms(x, g):
    return x * lax.rsqrt(jnp.mean(x * x, axis=-1, keepdims=True) + NORM_EPS) * g


def _sigmoid(x):
    return 0.5 * jnp.tanh(0.5 * x) + 0.5


def _log_sigmoid(x):
    return jnp.minimum(x, 0.0) - jnp.log(1.0 + jnp.exp(-jnp.abs(x)))


def _iota(shape, dim):
    return lax.broadcasted_iota(jnp.int32, shape, dim)


def _tri(n, block, upper=False, strict=False):
    r, c = _iota((n, n), 0), _iota((n, n), 1)
    same = (r // block) == (c // block)
    if upper:
        keep = (r < c) if strict else (r <= c)
    else:
        keep = (r > c) if strict else (r >= c)
    return jnp.where(same & keep, 1.0, 0.0).astype(BF16)


def _shift_rows(prev_ref, x, k):
    ext = jnp.concatenate([prev_ref[...], x], axis=0)
    return pltpu.roll(ext, k, 0)[prev_ref.shape[0]:]


def _layer_spec(layer, *shape):
    return pl.BlockSpec((None,) + shape, lambda *_: (layer,) + (0,) * len(shape))


def _ffn_kernel(x_ref, g_ref, wg_ref, wu_ref, wd_ref, fg_ref, o_ref, *, final, tf):
    x = x_ref[...]
    xn = _rms(x, g_ref[...]).astype(BF16)
    acc = None
    for j in range(wg_ref.shape[1] // tf):
        cols = slice(j * tf, (j + 1) * tf)
        gate = _mm(xn, wg_ref[:, cols])
        up = _mm(xn, wu_ref[:, cols])
        part = _mm(gate * _sigmoid(gate) * up, wd_ref[cols, :])
        acc = part if acc is None else acc + part
    y = x + FFN_HALF * acc
    if final:
        y = _rms(y, fg_ref[...])
    o_ref[...] = y


def _ffn(x, gain, wg, wu, wd, final_gain, *, layer, final, tm=512, tf=256):
    n, d = x.shape
    f = wg.shape[2]
    resident = lambda *shape: pl.BlockSpec((None,) + shape, lambda i: (layer, 0, 0), pipeline_mode=pl.Buffered(1))
    return pl.pallas_call(
        functools.partial(_ffn_kernel, final=final, tf=tf),
        grid=(n // tm,),
        in_specs=[
            pl.BlockSpec((tm, d), lambda i: (i, 0)),
            _layer_spec(layer, 1, d),
            resident(d, f), resident(d, f), resident(f, d),
            pl.BlockSpec((1, d), lambda i: (0, 0)),
        ],
        out_specs=pl.BlockSpec((tm, d), lambda i: (i, 0)),
        out_shape=jax.ShapeDtypeStruct((n, d), F32),
        compiler_params=pltpu.CompilerParams(
            dimension_semantics=("parallel",), vmem_limit_bytes=VMEM_LIMIT),
        name="ffn",
    )(x, gain, wg, wu, wd, final_gain)


def _norm_matmul_kernel(x_ref, g_ref, w_ref, o_ref, xn_ref):
    @pl.when(pl.program_id(1) == 0)
    def _():
        xn_ref[...] = _rms(x_ref[...], g_ref[...]).astype(BF16)

    o_ref[...] = _mm(xn_ref[...], w_ref[...]).astype(o_ref.dtype)


def _norm_matmul(x, gain, w, *, layer, tm, tn, name, out_dtype):
    n, d = x.shape
    m = w.shape[2]
    return pl.pallas_call(
        _norm_matmul_kernel,
        grid=(n // tm, m // tn),
        in_specs=[
            pl.BlockSpec((tm, d), lambda i, j: (i, 0)),
            _layer_spec(layer, 1, d),
            pl.BlockSpec((None, d, tn), lambda i, j: (layer, 0, j)),
        ],
        out_specs=pl.BlockSpec((tm, tn), lambda i, j: (i, j)),
        out_shape=jax.ShapeDtypeStruct((n, m), out_dtype),
        scratch_shapes=[pltpu.VMEM((tm, d), BF16)],
        compiler_params=pltpu.CompilerParams(
            dimension_semantics=("parallel", "arbitrary"), vmem_limit_bytes=VMEM_LIMIT),
        name=name,
    )(x, gain, w)


def _mlstm_kernel(u_ref, o_ref, sm_ref, conv_ref, wq_ref, wk_ref, wv_ref, brow_ref, bcol_ref, ng_ref,
                  out_ref, uprev_ref, c_ref, m_ref):
    nb, lb = u_ref.shape[0], u_ref.shape[1]
    e = MLSTM_HEAD_DIM
    nh = MLSTM_HEADS

    @pl.when(pl.program_id(0) == 0)
    def _():
        uprev_ref[...] = jnp.zeros_like(uprev_ref)
        c_ref[...] = jnp.zeros_like(c_ref)
        m_ref[...] = jnp.zeros_like(m_ref)

    u, uc, pre, pre_t, b_cols, b_rows = [], [], [], [], [], []
    lower, upper = _tri(lb, lb), _tri(lb, lb, upper=True)
    for bi in range(nb):
        ub = u_ref[bi].astype(F32)
        conv = None
        for j in range(MLSTM_CONV):
            k = MLSTM_CONV - 1 - j
            term = conv_ref[j:j + 1, :] * (_shift_rows(uprev_ref.at[bi], ub, k) if k else ub)
            conv = term if conv is None else conv + term
        uprev_ref[bi] = ub[lb - uprev_ref.shape[1]:]
        u.append(ub)
        uc.append(conv * _sigmoid(conv))
        sm = sm_ref[bi].astype(F32)
        pre.append(sm + brow_ref[...])
        pre_t.append(sm.T[0:8, :] + bcol_ref[0:8, :])
        b_cols.append(_mm(lower, _log_sigmoid(pre[bi]), pb=3))
        b_rows.append(_mm(_log_sigmoid(pre_t[bi]), upper, pa=3))

    causal = _iota((lb, lb), 0) >= _iota((lb, lb), 1)
    streams = [(bi, h) for bi in range(nb) for h in range(nh)]
    sls = [slice(h * e, (h + 1) * e) for h in range(nh)]
    ones = jnp.ones((lb, e), BF16)
    q = {s: _mm(uc[s[0]][:, sls[s[1]]], wq_ref[s[1]]).astype(BF16) for s in streams}
    k = {s: _mm(uc[s[0]][:, sls[s[1]]], wk_ref[s[1]]) * e ** -0.5 for s in streams}
    v = {s: jnp.concatenate([_mm(u[s[0]][:, sls[s[1]]], wv_ref[s[1]]).astype(BF16), ones], axis=1) for s in streams}
    bc = {(bi, h): b_cols[bi][:, nh + h:nh + h + 1] for bi, h in streams}
    m_st = {(bi, h): m_ref[bi * SUBLANES + h:bi * SUBLANES + h + 1, 0:1] for bi, h in streams}
    dmat = {(bi, h): jnp.where(causal, bc[bi, h] - b_rows[bi][nh + h:nh + h + 1, :] + pre_t[bi][h:h + 1, :], -jnp.inf)
            for bi, h in streams}
    m_inter = {s: bc[s] + m_st[s] for s in streams}
    m_row = {s: jnp.maximum(m_inter[s], jnp.max(dmat[s], axis=-1, keepdims=True)) for s in streams}
    qk = {s: _mm(q[s], k[s], _NT) for s in streams}
    w = {s: jnp.exp(dmat[s] - m_row[s]) * qk[s] for s in streams}
    from_state = {(bi, h): _mm(q[bi, h], c_ref[bi * nh + h]) for bi, h in streams}
    from_chunk = {s: _mm(w[s], v[s]) for s in streams}
    for bi, h in streams:
        s = (bi, h)
        tot = jnp.exp(m_inter[s] - m_row[s]) * from_state[s] + from_chunk[s]
        hh = tot[:, :e] / jnp.maximum(jnp.abs(tot[:, e:e + 1]), jnp.exp(-m_row[s]))
        hn = hh * lax.rsqrt(jnp.mean(hh * hh, axis=-1, keepdims=True) + NORM_EPS) * ng_ref[:, sls[h]]
        out_ref[bi, :, sls[h]] = (hn * _sigmoid(o_ref[bi, :, sls[h]].astype(F32))).astype(out_ref.dtype)
    for bi, h in streams:
        s = (bi, h)
        b_last = bc[s][lb - 1:lb, :]
        e_col = b_last - bc[s] + pre[bi][:, h:h + 1]
        m_new = jnp.maximum(b_last + m_st[s], jnp.max(e_col, axis=0, keepdims=True))
        decay = jnp.exp(b_last + m_st[s] - m_new)
        wk = jnp.exp(e_col - m_new) * k[s]
        c_ref[bi * nh + h] = decay * c_ref[bi * nh + h] + _mm(wk.T, v[s])
        m_ref[bi * SUBLANES + h:bi * SUBLANES + h + 1, :] = jnp.broadcast_to(m_new, (1, m_ref.shape[1]))


def _mlstm(z, layer, conv_w, wq, wk, wv, b_row, b_col, norm_g):
    bsz, t, _ = z.shape
    lb = TIME_BLOCK
    hd = MLSTM_HEAD_DIM
    return pl.pallas_call(
        _mlstm_kernel,
        grid=(t // lb,),
        in_specs=[
            pl.BlockSpec((bsz, lb, WIDTH), lambda c: (0, c, ZC_AU // WIDTH)),
            pl.BlockSpec((bsz, lb, WIDTH), lambda c: (0, c, ZC_AO // WIDTH)),
            pl.BlockSpec((bsz, lb, SMALL), lambda c: (0, c, ZC_SM // SMALL)),
            _layer_spec(layer, MLSTM_CONV, WIDTH),
            _layer_spec(layer, MLSTM_HEADS, hd, hd),
            _layer_spec(layer, MLSTM_HEADS, hd, hd),
            _layer_spec(layer, MLSTM_HEADS, hd, hd),
            _layer_spec(layer, 1, SMALL),
            _layer_spec(layer, SMALL, 1),
            _layer_spec(layer, 1, WIDTH),
        ],
        out_specs=pl.BlockSpec((bsz, lb, WIDTH), lambda c: (0, c, 0)),
        out_shape=jax.ShapeDtypeStruct((bsz, t, WIDTH), BF16),
        scratch_shapes=[
            pltpu.VMEM((bsz, SUBLANES, WIDTH), F32),
            pltpu.VMEM((bsz * MLSTM_HEADS, hd, 2 * hd), F32),
            pltpu.VMEM((bsz * SUBLANES, hd), F32),
        ],
        compiler_params=pltpu.CompilerParams(
            dimension_semantics=("arbitrary",), vmem_limit_bytes=VMEM_LIMIT),
        name="mlstm",
    )(z, z, z, conv_w, wq, wk, wv, b_row, b_col, norm_g)


def _s5_kernel(u_ref, pin_ref, pout_ref, win_ref, wout_ref, lre_ref, lim_ref, d_ref, out_ref, x_ref, c_ref, *, bsz):
    steps = u_ref.shape[1]
    rows = steps * SUBLANES
    ns = S5_BLOCK_STATES
    cw = S5_BLOCK_CH
    nblk = S5_HALF // cw

    @pl.when(pl.program_id(0) == 0)
    def _():
        c_ref[...] = jnp.zeros_like(c_ref)

    u_bt = u_ref[...].reshape(bsz * steps, WIDTH)
    u = _mm(pin_ref[0], u_bt[:, :S5_HALF]) + _mm(pin_ref[1], u_bt[:, S5_HALF:])
    upper = (_iota((rows, 1), 0) % SUBLANES) >= bsz
    for j in range(nblk):
        uj = u[:, j * cw:(j + 1) * cw]
        uext = jnp.concatenate([jnp.where(upper, 0.0, uj), jnp.where(upper, uj, 0.0)], axis=1)
        x_ref[:, j * 2 * ns:(j + 1) * 2 * ns] = _mm(uext, win_ref[j])

    lre, lim = lre_ref[...], lim_ref[...]

    def step(t, carry):
        r0 = pl.multiple_of(t * SUBLANES, SUBLANES)
        new = []
        for j in range(nblk):
            sr, si = carry[2 * j], carry[2 * j + 1]
            lr, li = lre[:, j * ns:(j + 1) * ns], lim[:, j * ns:(j + 1) * ns]
            re_cols = slice(j * 2 * ns, j * 2 * ns + ns)
            im_cols = slice(j * 2 * ns + ns, (j + 1) * 2 * ns)
            nr = lr * sr - li * si + x_ref[pl.ds(r0, SUBLANES), re_cols]
            ni = lr * si + li * sr + x_ref[pl.ds(r0, SUBLANES), im_cols]
            x_ref[pl.ds(r0, SUBLANES), re_cols] = nr
            x_ref[pl.ds(r0, SUBLANES), im_cols] = ni
            new += [nr, ni]
        return tuple(new)

    init = tuple(c_ref[:, k * ns:(k + 1) * ns] for k in range(2 * nblk))
    last = lax.fori_loop(0, steps, step, init, unroll=8)
    for k in range(2 * nblk):
        c_ref[:, k * ns:(k + 1) * ns] = last[k]

    ys = []
    for j in range(nblk):
        res = _mm(x_ref[:, j * 2 * ns:(j + 1) * 2 * ns], wout_ref[j])
        cols = slice(j * cw, (j + 1) * cw)
        y = jnp.where(upper, res[:, cw:], res[:, :cw]) + d_ref[:, cols] * u[:, cols]
        ys.append(0.5 * y * (1.0 + jnp.tanh(math.sqrt(2.0 / math.pi) * (y + 0.044715 * y * y * y))))
    y = jnp.concatenate(ys, axis=1).astype(BF16)
    out = jnp.concatenate([_mm(pout_ref[0], y), _mm(pout_ref[1], y)], axis=1)
    out_ref[...] = out.astype(out_ref.dtype).reshape(bsz, steps, WIDTH)


def _s5_permutations(bsz, steps):
    rows = steps * 2 * bsz
    r = np.arange(rows)
    p = np.zeros((2, rows, bsz * steps), np.float32)
    p[(r % (2 * bsz)) // bsz, r, (r % bsz) * steps + r // (2 * bsz)] = 1.0
    return jnp.asarray(p, BF16), jnp.asarray(p.transpose(0, 2, 1), BF16)


def _s5(z, bsz, layer, w_in, w_out, lam_re, lam_im, d_tab):
    t = z.shape[1]
    steps = S5_STEPS
    rows = steps * SUBLANES
    nblk = S5_HALF // S5_BLOCK_CH
    nst = nblk * S5_BLOCK_STATES
    p_in, p_out = _s5_permutations(bsz, steps)
    whole = lambda *shape: pl.BlockSpec(shape, lambda i: (0,) * len(shape))
    return pl.pallas_call(
        functools.partial(_s5_kernel, bsz=bsz),
        grid=(t // steps,),
        in_specs=[
            pl.BlockSpec((bsz, steps, WIDTH), lambda i: (0, i, ZC_BU // WIDTH)),
            whole(2, rows, bsz * steps),
            whole(2, bsz * steps, rows),
            _layer_spec(layer, nblk, 2 * S5_BLOCK_CH, 2 * S5_BLOCK_STATES),
            _layer_spec(layer, nblk, 2 * S5_BLOCK_STATES, 2 * S5_BLOCK_CH),
            _layer_spec(layer, SUBLANES, nst),
            _layer_spec(layer, SUBLANES, nst),
            _layer_spec(layer, rows, S5_HALF),
        ],
        out_specs=pl.BlockSpec((bsz, steps, WIDTH), lambda i: (0, i, 0)),
        out_shape=jax.ShapeDtypeStruct((bsz, t, WIDTH), BF16),
        scratch_shapes=[pltpu.VMEM((rows, 2 * nst), F32), pltpu.VMEM((SUBLANES, 2 * nst), F32)],
        compiler_params=pltpu.CompilerParams(
            dimension_semantics=("arbitrary",), vmem_limit_bytes=VMEM_LIMIT),
        name="s5",
    )(z, p_in, p_out, w_in, w_out, lam_re, lam_im, d_tab)


def _gla_kernel(q_ref, k_ref, v_ref, g_ref, sm_ref, aup_ref, ab_ref, ng_ref, out_ref, st_ref):
    lb = q_ref.shape[0]
    ch = GLA_CHUNK
    kw = GLA_KEY_WIDTH

    @pl.when(pl.program_id(1) == 0)
    def _():
        st_ref[...] = jnp.zeros_like(st_ref)

    x = _mm(sm_ref[...], aup_ref[...], pb=2) + ab_ref[...]
    log_a = _log_sigmoid(x) * (1.0 / GLA_GATE_TAU)
    b_all = _mm(_tri(lb, ch), log_a, pb=3)
    lane_head = _iota((1, kw), 1) // GLA_HEAD_K
    masks = [jnp.where(lane_head == h, 1.0, 0.0).astype(BF16) for h in range(GLA_HEADS)]
    causal = _iota((ch, ch), 0) >= _iota((ch, ch), 1)
    mid = ch // 2 - 1
    heads = range(GLA_HEADS)
    chunks = range(lb // ch)
    rss = [slice(c * ch, (c + 1) * ch) for c in chunks]
    vss = [slice(h * GLA_HEAD_V, (h + 1) * GLA_HEAD_V) for h in heads]

    q_in, q_at, k_at, k_st, dec = [], [], [], [], []
    for c in chunks:
        b = b_all[rss[c]]
        b_mid, b_last = b[mid:mid + 1, :], b[ch - 1:ch, :]
        q = q_ref[rss[c], :].astype(F32) * GLA_HEAD_K ** -0.5
        k = k_ref[rss[c], :].astype(F32)
        q_in.append((q * jnp.exp(b)).astype(BF16))
        q_at.append((q * jnp.exp(b - b_mid)).astype(BF16))
        k_at.append((k * jnp.exp(b_mid - b)).astype(BF16))
        k_st.append((k * jnp.exp(b_last - b)).astype(BF16))
        dec.append(jnp.exp(b_last))
    v = [[v_ref[rss[c], vss[h]] for h in heads] for c in chunks]
    attn = [[jnp.where(causal, _mm(q_at[c] * masks[h], k_at[c], _NT), 0.0) for h in heads] for c in chunks]
    grow = [[_mm(v[c][h].astype(F32).T, k_st[c] * masks[h]) for h in heads] for c in chunks]
    state = [st_ref[h] for h in heads]
    for c in chunks:
        for h in heads:
            o = _mm(q_in[c] * masks[h], state[h], _NT) + _mm(attn[c][h], v[c][h])
            state[h] = dec[c] * state[h] + grow[c][h]
            on = o * lax.rsqrt(jnp.mean(o * o, axis=-1, keepdims=True) + NORM_EPS) * ng_ref[:, vss[h]]
            g = g_ref[rss[c], vss[h]].astype(F32)
            out_ref[rss[c], vss[h]] = (on * (g * _sigmoid(g))).astype(out_ref.dtype)
    for h in heads:
        st_ref[h] = state[h]


def _gla(z, bsz, layer, a_up, a_bias, norm_g):
    n = z.shape[0]
    lb = TIME_BLOCK
    nblk = n // bsz // lb
    row = lambda b, c: b * nblk + c
    return pl.pallas_call(
        _gla_kernel,
        grid=(bsz, nblk),
        in_specs=[
            pl.BlockSpec((lb, GLA_KEY_WIDTH), lambda b, c: (row(b, c), ZC_CQ // GLA_KEY_WIDTH)),
            pl.BlockSpec((lb, GLA_KEY_WIDTH), lambda b, c: (row(b, c), ZC_CK // GLA_KEY_WIDTH)),
            pl.BlockSpec((lb, WIDTH), lambda b, c: (row(b, c), ZC_CV // WIDTH)),
            pl.BlockSpec((lb, WIDTH), lambda b, c: (row(b, c), ZC_CG // WIDTH)),
            pl.BlockSpec((lb, SMALL), lambda b, c: (row(b, c), ZC_SM // SMALL)),
            _layer_spec(layer, SMALL, GLA_KEY_WIDTH),
            _layer_spec(layer, 1, GLA_KEY_WIDTH),
            _layer_spec(layer, 1, WIDTH),
        ],
        out_specs=pl.BlockSpec((lb, WIDTH), lambda b, c: (row(b, c), 0)),
        out_shape=jax.ShapeDtypeStruct((n, WIDTH), BF16),
        scratch_shapes=[pltpu.VMEM((GLA_HEADS, GLA_HEAD_V, GLA_KEY_WIDTH), F32)],
        compiler_params=pltpu.CompilerParams(
            dimension_semantics=("parallel", "arbitrary"), vmem_limit_bytes=VMEM_LIMIT),
        name="gla",
    )(z, z, z, z, z, a_up, a_bias, norm_g)


def _rwkv_kernel(r_ref, k_ref, v_ref, t_ref, mu_ref, mut_ref,
                 w0_ref, wup_ref, a0_ref, aup_ref, gup_ref, kk_ref, ka_ref, rk_ref, ng_ref,
                 out_ref, pr_ref, pk_ref, pv_ref, pt_ref, st_ref, y_ref):
    nb, ch = r_ref.shape[0], r_ref.shape[1]
    lb = nb * ch
    gl = RWKV_GROUP_LANES
    gh = RWKV_GROUP_HEADS

    @pl.when(pl.program_id(0) == 0)
    def _():
        for ref in (pr_ref, pk_ref, pv_ref, pt_ref, st_ref):
            ref[...] = jnp.zeros_like(ref)

    def shift_mix(x_ref, prev_ref, mu):
        mixed = []
        for bi in range(nb):
            x = x_ref[bi].astype(F32)
            xs = _shift_rows(prev_ref.at[bi], x, 1)
            prev_ref[bi] = x[ch - prev_ref.shape[1]:]
            mixed.append(x + mu * (xs - x))
        return jnp.concatenate(mixed, axis=0)

    r = shift_mix(r_ref, pr_ref, mu_ref[0:1, :])
    k = shift_mix(k_ref, pk_ref, mu_ref[1:2, :])
    v = shift_mix(v_ref, pv_ref, mu_ref[2:3, :])
    tl = shift_mix(t_ref, pt_ref, mut_ref[...])

    ones_bd = _tri(WIDTH, RWKV_HEAD) + _tri(WIDTH, RWKV_HEAD, upper=True, strict=True)
    w_log = _log_sigmoid(w0_ref[...] + _mm(jnp.tanh(tl), wup_ref[...])) - 0.5
    log_w = -jnp.exp(w_log)
    a = _sigmoid(a0_ref[...] + _mm(tl, aup_ref[...]))
    g = _mm(_sigmoid(tl), gup_ref[...])
    kk = k * kk_ref[...]
    kk = kk / jnp.maximum(jnp.sqrt(_mm(kk * kk, ones_bd, pa=3)), 1e-12)
    k_rep = k * (1.0 + (a - 1.0) * ka_ref[...])
    av = -kk
    bv = kk * a
    bonus = _mm(r * k_rep * rk_ref[...], ones_bd, pa=3) * v
    gc_all = _mm(_tri(lb, ch), log_w, pb=3)

    lane_head = _iota((1, gl), 1) // RWKV_HEAD
    masks = [jnp.where(lane_head == h, 1.0, 0.0).astype(BF16) for h in range(gh)]

    def stack(x):
        xb = x.astype(BF16)
        return jnp.concatenate([xb * m for m in masks], axis=0)

    def stack_t(x):
        return jnp.concatenate([x * m.astype(F32) for m in masks], axis=0).T

    s_lane = _iota((ch, gl), 1) % ch
    t_row = _iota((ch, gl), 0)
    strict = t_row > s_lane
    incl = t_row >= s_lane
    eye = jnp.where(t_row == s_lane, 1.0, 0.0)

    pairs = [(gi, bi) for gi in range(WIDTH // gl) for bi in range(nb)]
    pre = {}
    for gi, bi in pairs:
        ls = slice(gi * gl, (gi + 1) * gl)
        rs = slice(bi * ch, (bi + 1) * ch)
        gc = gc_all[rs, ls]
        g_last = gc[ch - 1:ch, :]
        e_inv = jnp.exp(-gc)
        e_end = jnp.exp(g_last - gc)
        r_t = (r[rs, ls] * jnp.exp(gc)).astype(BF16)
        a_t = (av[rs, ls] * jnp.exp(gc - log_w[rs, ls])).astype(BF16)
        scores = _mm(jnp.concatenate([a_t, r_t], axis=0),
                     jnp.concatenate([stack(bv[rs, ls] * e_inv), stack(k_rep[rs, ls] * e_inv)], axis=0), _NT)
        pre[gi, bi] = dict(
            r_t=r_t, a_bd=stack(a_t), v_bd=stack(v[rs, ls]), dec=jnp.exp(g_last),
            b_e=stack(bv[rs, ls] * e_end), k_e=stack(k_rep[rs, ls] * e_end),
            a_ab=jnp.where(strict, scores[:ch, :gl], 0.0),
            a_kv=jnp.concatenate([jnp.where(strict, scores[:ch, gl:], 0.0),
                                  jnp.where(incl, scores[ch:, gl:], 0.0)], axis=0).astype(BF16),
            a_rb=jnp.where(incl, scores[ch:, :gl], 0.0).astype(BF16))
    tinv = {key: eye + pre[key]["a_ab"] for key in pairs}
    power = {key: pre[key]["a_ab"].astype(BF16) for key in pairs}
    for key in pairs:
        power[key] = _mm(power[key], stack(power[key])).astype(BF16)
    step = 2
    while step < ch:
        for key in pairs:
            if 2 * step < ch:
                both = _mm(jnp.concatenate([power[key], tinv[key].astype(BF16)], axis=0), stack(power[key]))
                power[key] = both[:ch].astype(BF16)
                tinv[key] = tinv[key] + both[ch:]
            else:
                tinv[key] = tinv[key] + _mm(tinv[key], stack(power[key]))
        step *= 2

    from_v = {key: _mm(pre[key]["a_kv"], pre[key]["v_bd"]) for key in pairs}
    t_bf = {key: tinv[key].astype(BF16) for key in pairs}
    a_hat = {key: _mm(t_bf[key], pre[key]["a_bd"]).astype(BF16) for key in pairs}
    u0 = {key: _mm(t_bf[key], stack(from_v[key][:ch])) for key in pairs}
    grow = {key: _mm(pre[key]["v_bd"].astype(F32).T, pre[key]["k_e"]) for key in pairs}
    n_groups = WIDTH // gl
    state = {(gi, bi): st_ref[bi * n_groups + gi] for gi, bi in pairs}
    from_state = {key: _mm(jnp.concatenate([a_hat[key], pre[key]["r_t"]], axis=0), state[key], _NT) for key in pairs}
    u = {key: from_state[key][:ch] + u0[key] for key in pairs}
    for gi, bi in pairs:
        key = (gi, bi)
        y_ref[bi * ch:(bi + 1) * ch, gi * gl:(gi + 1) * gl] = (
            from_state[key][ch:] + _mm(pre[key]["a_rb"], stack(u[key])) + from_v[key][ch:])
    for gi, bi in pairs:
        key = (gi, bi)
        st_ref[bi * n_groups + gi] = (pre[key]["dec"] * state[key] + _mm(stack_t(u[key]), pre[key]["b_e"])
                                      + grow[key])

    y = y_ref[...]
    inv_n = 1.0 / RWKV_HEAD
    mu = _mm(y, ones_bd, pa=3) * inv_n
    yc = y - mu
    var = _mm(yc * yc, ones_bd, pa=3) * inv_n
    yn = yc * lax.rsqrt(var + RWKV_GN_EPS) * ng_ref[...]
    out_ref[...] = ((yn + bonus) * g).astype(out_ref.dtype).reshape(nb, ch, WIDTH)


def _rwkv(z, layer, mu_rkv, mu_t, w0, w_up, a0, a_up, g_up, k_k, k_a, r_k, norm_g):
    bsz, t, _ = z.shape
    ch = RWKV_CHUNK
    gl = RWKV_GROUP_LANES
    vec = _layer_spec(layer, 1, WIDTH)
    mat = _layer_spec(layer, RWKV_TAIL, WIDTH)
    return pl.pallas_call(
        _rwkv_kernel,
        grid=(t // ch,),
        in_specs=[
            pl.BlockSpec((bsz, ch, WIDTH), lambda c: (0, c, ZC_DR // WIDTH)),
            pl.BlockSpec((bsz, ch, WIDTH), lambda c: (0, c, ZC_DK // WIDTH)),
            pl.BlockSpec((bsz, ch, WIDTH), lambda c: (0, c, ZC_DV // WIDTH)),
            pl.BlockSpec((bsz, ch, RWKV_TAIL), lambda c: (0, c, ZC_DT // RWKV_TAIL)),
            _layer_spec(layer, 3, WIDTH), _layer_spec(layer, 1, RWKV_TAIL),
            vec, mat, vec, mat, mat, vec, vec, vec, vec,
        ],
        out_specs=pl.BlockSpec((bsz, ch, WIDTH), lambda c: (0, c, 0)),
        out_shape=jax.ShapeDtypeStruct((bsz, t, WIDTH), BF16),
        scratch_shapes=[
            pltpu.VMEM((bsz, SUBLANES, WIDTH), F32), pltpu.VMEM((bsz, SUBLANES, WIDTH), F32),
            pltpu.VMEM((bsz, SUBLANES, WIDTH), F32), pltpu.VMEM((bsz, SUBLANES, RWKV_TAIL), F32),
            pltpu.VMEM((bsz * (WIDTH // gl), gl, gl), F32),
            pltpu.VMEM((bsz * ch, WIDTH), F32),
        ],
        compiler_params=pltpu.CompilerParams(
            dimension_semantics=("arbitrary",), vmem_limit_bytes=VMEM_LIMIT),
        name="rwkv",
    )(z, z, z, z, mu_rkv, mu_t, w0, w_up, a0, a_up, g_up, k_k, k_a, r_k, norm_g)


def _merge_kernel(x_ref, ya_ref, zb_ref, yc_ref, yd_ref, g0_ref, g1_ref, g2_ref, g3_ref, gb_ref,
                  pa_ref, w1_ref, w2_ref, pc_ref, pd_ref, wo_ref, out_ref):
    d = D_MODEL

    def gate(g_ref, i):
        return _sigmoid(g_ref[...].astype(F32) + gb_ref[:, i * d:(i + 1) * d])

    zb = zb_ref[...]
    merged = gate(g0_ref, 0) * _mm(ya_ref[...], pa_ref[...])
    merged = merged + gate(g1_ref, 1) * (_mm(zb, w1_ref[...]) * _sigmoid(_mm(zb, w2_ref[...])))
    merged = merged + gate(g2_ref, 2) * _mm(yc_ref[...], pc_ref[...])
    merged = merged + gate(g3_ref, 3) * _mm(yd_ref[...], pd_ref[...])
    out_ref[...] = x_ref[...] + _mm(merged, wo_ref[...])


def _merge(x, z, ya, zb, yc, yd, layer, gate_bias, pa, w1, w2, pc, pd, wo, *, tm=512):
    n, d = x.shape
    tile = lambda w: pl.BlockSpec((tm, w), lambda i: (i, 0))
    gate = lambda b: pl.BlockSpec((tm, d), lambda i: (i, ZC_GATE // d + b))
    resident = lambda rows: pl.BlockSpec((None, rows, d), lambda i: (layer, 0, 0), pipeline_mode=pl.Buffered(1))
    proj = resident(WIDTH)
    return pl.pallas_call(
        _merge_kernel,
        grid=(n // tm,),
        in_specs=[tile(d), tile(WIDTH), tile(WIDTH), tile(WIDTH), tile(WIDTH),
                  gate(0), gate(1), gate(2), gate(3), _layer_spec(layer, 1, N_BRANCH * d),
                  proj, proj, proj, proj, proj, resident(d)],
        out_specs=tile(d),
        out_shape=jax.ShapeDtypeStruct((n, d), F32),
        compiler_params=pltpu.CompilerParams(
            dimension_semantics=("parallel",), vmem_limit_bytes=VMEM_LIMIT),
        name="merge",
    )(x, ya, zb, yc, yd, z, z, z, z, gate_bias, pa, w1, w2, pc, pd, wo)


def _xattn_kernel(x_ref, g_ref, wq_ref, kv_ref, wo_ref, out_ref):
    d = D_MODEL
    hd = XATTN_HEAD_DIM
    x = x_ref[...]
    q = _mm(_rms(x, g_ref[...]), wq_ref[...])
    kv = kv_ref[...]
    outs = []
    for h in range(XATTN_HEADS):
        sl = slice(h * hd, (h + 1) * hd)
        s = _mm(q[:, sl], kv[:, sl], _NT) * hd ** -0.5
        s = s - jnp.max(s, axis=-1, keepdims=True)
        p = jnp.exp(s)
        p = p / jnp.sum(p, axis=-1, keepdims=True)
        outs.append(_mm(p, kv[:, d + h * hd:d + (h + 1) * hd]))
    o = jnp.concatenate(outs, axis=1)
    out_ref[...] = x + _mm(o, wo_ref[...])


def _xattn(x, bsz, layer, gain, wq, kv, wo, *, tm=512):
    n, d = x.shape
    nblk = n // bsz // tm
    m_len = kv.shape[0] // bsz
    return pl.pallas_call(
        _xattn_kernel,
        grid=(bsz, nblk),
        in_specs=[
            pl.BlockSpec((tm, d), lambda b, c: (b * nblk + c, 0)),
            _layer_spec(layer, 1, d),
            _layer_spec(layer, d, d),
            pl.BlockSpec((m_len, 2 * d), lambda b, c: (b, 0)),
            _layer_spec(layer, d, d),
        ],
        out_specs=pl.BlockSpec((tm, d), lambda b, c: (b * nblk + c, 0)),
        out_shape=jax.ShapeDtypeStruct((n, d), F32),
        compiler_params=pltpu.CompilerParams(
            dimension_semantics=("parallel", "parallel"), vmem_limit_bytes=VMEM_LIMIT),
        name="xattn",
    )(x, gain, wq, kv, wo)


def _regroup_w_in(w):
    o = np.cumsum((0, WIDTH, WIDTH, MLSTM_HEADS, MLSTM_HEADS, WIDTH, GLA_KEY_WIDTH, GLA_KEY_WIDTH, WIDTH, WIDTH,
                   GLA_GATE_RANK, 3 * WIDTH + RWKV_TAIL, N_BRANCH * D_MODEL))
    a_u, a_o, a_i, a_f, b_u, c_q, c_k, c_v, c_g, c_a, d_z, gate = (w[..., o[i]:o[i + 1]] for i in range(12))
    d_r, d_k, d_v, d_t = (d_z[..., :WIDTH], d_z[..., WIDTH:2 * WIDTH], d_z[..., 2 * WIDTH:3 * WIDTH],
                          d_z[..., 3 * WIDTH:])
    used = Z_COLS - 2 * SMALL + 2 * MLSTM_HEADS + GLA_GATE_RANK
    pad = jnp.zeros(w.shape[:-1] + (Z_COLS - used,), w.dtype)
    cols = jnp.concatenate([gate, a_u, a_o, b_u, c_v, c_g, d_r, d_k, d_v, c_q, c_k, d_t, a_i, a_f, c_a, pad], axis=-1)
    return cols.astype(BF16)


def _s5_tables(a_re, a_im, log_step, b_re, b_im, c_re, c_im, d_skip, bsz):
    depth = a_re.shape[0]
    step = jnp.exp(log_step)[..., None]
    lam_re = jnp.minimum(a_re, -S5_MIN_NEG)
    lam_im = a_im
    mag = jnp.exp(lam_re * step)
    bar_re = mag * jnp.cos(lam_im * step)
    bar_im = mag * jnp.sin(lam_im * step)
    denom = lam_re * lam_re + lam_im * lam_im
    coef_re = ((bar_re - 1.0) * lam_re + bar_im * lam_im) / denom
    coef_im = (bar_im * lam_re - (bar_re - 1.0) * lam_im) / denom
    bb_re = coef_re[..., None] * b_re - coef_im[..., None] * b_im
    bb_im = coef_re[..., None] * b_im + coef_im[..., None] * b_re
    nb = S5_HALF // S5_BLOCK_CH
    gpb = S5_BLOCK_GROUPS
    eye = jnp.eye(gpb, dtype=F32)

    def in_block(bb):
        bb = bb.reshape(depth, 2, nb, gpb, S5_STATE, S5_GROUP)
        return jnp.einsum('lhjgpc,gk->lhjgckp', bb, eye).reshape(depth, 2, nb, S5_BLOCK_CH, S5_BLOCK_STATES)

    def out_block(cc):
        cc = cc.reshape(depth, 2, nb, gpb, S5_GROUP, S5_STATE)
        return jnp.einsum('lhjgcp,gk->lhjgpkc', cc, eye).reshape(depth, 2, nb, S5_BLOCK_STATES, S5_BLOCK_CH)

    w_in = jnp.concatenate([in_block(bb_re), in_block(bb_im)], axis=-1)
    w_in = w_in.transpose(0, 2, 1, 3, 4).reshape(depth, nb, 2 * S5_BLOCK_CH, 2 * S5_BLOCK_STATES)
    w_out = jnp.concatenate([out_block(c_re), -out_block(c_im)], axis=-2)
    w_out = w_out.transpose(0, 2, 3, 1, 4).reshape(depth, nb, 2 * S5_BLOCK_STATES, 2 * S5_BLOCK_CH)

    def rows(t, width):
        t = t.reshape(depth, 2, 1, width)
        return jnp.broadcast_to(t, (depth, 2, bsz, width)).reshape(depth, 2 * bsz, width)

    nst = nb * S5_BLOCK_STATES
    lam_re_rows = rows(bar_re.reshape(depth, 2 * nst), nst)
    lam_im_rows = rows(bar_im.reshape(depth, 2 * nst), nst)
    d_tab = jnp.tile(rows(d_skip, S5_HALF), (1, S5_STEPS, 1))
    return w_in.astype(BF16), w_out.astype(BF16), lam_re_rows, lam_im_rows, d_tab


def _pad_rows(w, start, total):
    return jnp.pad(w, ((0, 0), (start, total - start - w.shape[1]), (0, 0)))


def kernel(x, mem, ffn1_norm, ffn1_w_gate, ffn1_w_up, ffn1_w_down, mix_norm, w_in, gate_bias, mlstm_conv, mlstm_wq, mlstm_wk, mlstm_wv, mlstm_b_i, mlstm_b_f, mlstm_norm, mlstm_proj, s5_a_re, s5_a_im, s5_log_step, s5_b_re, s5_b_im, s5_c_re, s5_c_im, s5_d, s5_glu_w1, s5_glu_w2, gla_a_up, gla_a_bias, gla_norm, gla_proj, rwkv_mu, rwkv_w0, rwkv_w_up, rwkv_a0, rwkv_a_up, rwkv_g_up, rwkv_k_k, rwkv_k_a, rwkv_r_k, rwkv_norm, rwkv_proj, w_out, xattn_norm, mem_norm, xattn_wq, xattn_wk, xattn_wv, xattn_wo, ffn2_norm, ffn2_w_gate, ffn2_w_up, ffn2_w_down, final_norm):
    bsz, t, d = x.shape
    depth = w_in.shape[0]
    assert d == D_MODEL and t % TIME_BLOCK == 0 and 2 * bsz == SUBLANES
    xs = x.reshape(bsz * t, d)
    mems = mem.reshape(bsz * mem.shape[1], d)
    bf = lambda w: w.astype(BF16)
    vec = lambda v: v.reshape(depth, 1, -1)
    fin = final_norm.reshape(1, d)

    w_in_r = _regroup_w_in(w_in)
    gate_b = jnp.pad(jnp.concatenate([mlstm_b_i, mlstm_b_f], axis=1), ((0, 0), (0, SMALL - 2 * MLSTM_HEADS)))
    s5_w_in, s5_w_out, s5_lre, s5_lim, s5_dtab = _s5_tables(
        s5_a_re, s5_a_im, s5_log_step, s5_b_re, s5_b_im, s5_c_re, s5_c_im, s5_d, bsz)
    gla_up = _pad_rows(gla_a_up, 2 * MLSTM_HEADS, SMALL)
    mu_rkv = rwkv_mu[:, :3 * WIDTH].reshape(depth, 3, WIDTH)
    mu_t = vec(rwkv_mu[:, 3 * WIDTH:])
    rw_wup = bf(_pad_rows(rwkv_w_up, 0, RWKV_TAIL))
    rw_aup = bf(_pad_rows(rwkv_a_up, RWKV_DECAY_RANK, RWKV_TAIL))
    rw_gup = bf(_pad_rows(rwkv_g_up, RWKV_DECAY_RANK + RWKV_ICLR_RANK, RWKV_TAIL))
    w_kv = bf(jnp.concatenate([xattn_wk, xattn_wv], axis=2))
    wq_a, wk_a, wv_a = bf(mlstm_wq), bf(mlstm_wk), bf(mlstm_wv)
    p_a, p_b1, p_b2, p_c, p_d, p_o = (bf(mlstm_proj), bf(s5_glu_w1), bf(s5_glu_w2), bf(gla_proj), bf(rwkv_proj),
                                      bf(w_out))
    x_wq, x_wo = bf(xattn_wq), bf(xattn_wo)
    f1_g, f1_u, f1_d = bf(ffn1_w_gate), bf(ffn1_w_up), bf(ffn1_w_down)
    f2_g, f2_u, f2_d = bf(ffn2_w_gate), bf(ffn2_w_up), bf(ffn2_w_down)

    for l in range(depth):
        xs = _ffn(xs, vec(ffn1_norm), f1_g, f1_u, f1_d, fin, layer=l, final=False)

        z = _norm_matmul(xs, vec(mix_norm), w_in_r, layer=l, tm=1024, tn=1536, name="in_proj",
                         out_dtype=BF16)

        z3 = z.reshape(bsz, t, Z_COLS)
        y_a = _mlstm(z3, l, mlstm_conv, wq_a, wk_a, wv_a, vec(gate_b), gate_b.reshape(depth, SMALL, 1),
                     vec(mlstm_norm)).reshape(bsz * t, WIDTH)

        z_b = _s5(z3, bsz, l, s5_w_in, s5_w_out, s5_lre, s5_lim, s5_dtab).reshape(bsz * t, WIDTH)

        y_c = _gla(z, bsz, l, gla_up, vec(gla_a_bias), vec(gla_norm))

        y_d = _rwkv(z3, l, mu_rkv, mu_t, vec(rwkv_w0), rw_wup, vec(rwkv_a0), rw_aup, rw_gup,
                    vec(rwkv_k_k), vec(rwkv_k_a), vec(rwkv_r_k), vec(rwkv_norm)).reshape(bsz * t, WIDTH)

        xs = _merge(xs, z, y_a, z_b, y_c, y_d, l, vec(gate_bias), p_a, p_b1, p_b2, p_c, p_d, p_o)

        kv = _norm_matmul(mems, vec(mem_norm), w_kv, layer=l, tm=mems.shape[0], tn=512, name="mem_kv",
                          out_dtype=BF16)
        xs = _xattn(xs, bsz, l, vec(xattn_norm), x_wq, kv, x_wo)

        xs = _ffn(xs, vec(ffn2_norm), f2_g, f2_u, f2_d, fin, layer=l,
                  final=(l == depth - 1))
    return xs.reshape(bsz, t, d)
```

```python
import functools
import math

import jax
import jax.numpy as jnp
import numpy as np
from jax import lax
from jax.experimental import pallas as pl
from jax.experimental.pallas import tpu as pltpu

F32 = jnp.float32
BF16 = jnp.bfloat16

D_MODEL = 1024
D_FF = 11 * D_MODEL // 4
NORM_EPS = 1e-6
FFN_HALF = 0.5
N_BRANCH = 4
WIDTH = D_MODEL // 2

MLSTM_HEADS = 4
MLSTM_HEAD_DIM = WIDTH // MLSTM_HEADS
MLSTM_CONV = 4

S5_GROUP = 16
S5_GROUPS = WIDTH // S5_GROUP
S5_STATE = 64
S5_MIN_NEG = 1e-4
S5_HALF = WIDTH // 2
S5_BLOCK_GROUPS = 8
S5_BLOCK_CH = S5_BLOCK_GROUPS * S5_GROUP
S5_BLOCK_STATES = S5_BLOCK_GROUPS * S5_STATE
S5_STEPS = 64
SUBLANES = 8

GLA_HEADS = 4
GLA_KEY_WIDTH = WIDTH // 2
GLA_HEAD_K = GLA_KEY_WIDTH // GLA_HEADS
GLA_HEAD_V = WIDTH // GLA_HEADS
GLA_GATE_RANK = 16
GLA_GATE_TAU = 16.0
GLA_CHUNK = 64

RWKV_HEAD = 64
RWKV_HEADS = WIDTH // RWKV_HEAD
RWKV_DECAY_RANK = 64
RWKV_ICLR_RANK = 64
RWKV_GATE_RANK = 128
RWKV_TAIL = RWKV_DECAY_RANK + RWKV_ICLR_RANK + RWKV_GATE_RANK
RWKV_GN_EPS = 64e-5
RWKV_CHUNK = 64
RWKV_GROUP_LANES = 256
RWKV_GROUP_HEADS = RWKV_GROUP_LANES // RWKV_HEAD

XATTN_HEADS = 4
XATTN_HEAD_DIM = D_MODEL // XATTN_HEADS

ZC_GATE = 0
ZC_AU = 4096
ZC_AO = 4608
ZC_BU = 5120
ZC_CV = 5632
ZC_CG = 6144
ZC_DR = 6656
ZC_DK = 7168
ZC_DV = 7680
ZC_CQ = 8192
ZC_CK = 8448
ZC_DT = 8704
ZC_SM = 8960
Z_COLS = 9216
SMALL = 128

TIME_BLOCK = 256
VMEM_LIMIT = 56 * 1024 * 1024

_NN = (((1,), (0,)), ((), ()))
_NT = (((1,), (1,)), ((), ()))


def _split(x, n):
    parts = []
    rest = x
    for i in range(n):
        p = rest.astype(BF16)
        parts.append(p)
        if i + 1 < n:
            rest = rest - p.astype(F32)
    return parts


def _mm(a, b, dn=_NN, pa=1, pb=1):
    ap, bp = _split(a, pa), _split(b, pb)
    order = max(pa, pb)
    out = None
    for i, x in enumerate(ap):
        for j, y in enumerate(bp):
            if i + j < order:
                t = lax.dot_general(x, y, dn, preferred_element_type=F32)
                out = t if out is None else out + t
    return out


def _rms(x, g):
    return x * lax.rsqrt(jnp.mean(x * x, axis=-1, keepdims=True) + NORM_EPS) * g


def _sigmoid(x):
    return 0.5 * jnp.tanh(0.5 * x) + 0.5


def _log_sigmoid(x):
    return jnp.minimum(x, 0.0) - jnp.log(1.0 + jnp.exp(-jnp.abs(x)))


def _iota(shape, dim):
    return lax.broadcasted_iota(jnp.int32, shape, dim)


def _tri(n, block, upper=False, strict=False):
    r, c = _iota((n, n), 0), _iota((n, n), 1)
    same = (r // block) == (c // block)
    if upper:
        keep = (r < c) if strict else (r <= c)
    else:
        keep = (r > c) if strict else (r >= c)
    return jnp.where(same & keep, 1.0, 0.0).astype(BF16)


def _shift_rows(prev_ref, x, k):
    ext = jnp.concatenate([prev_ref[...], x], axis=0)
    return pltpu.roll(ext, k, 0)[prev_ref.shape[0]:]


def _layer_spec(layer, *shape):
    return pl.BlockSpec((None,) + shape, lambda *_: (layer,) + (0,) * len(shape))


def _ffn_kernel(x_ref, g_ref, wg_ref, wu_ref, wd_ref, fg_ref, o_ref, *, final, tf):
    x = x_ref[...]
    xn = _rms(x, g_ref[...]).astype(BF16)
    acc = None
    for j in range(wg_ref.shape[1] // tf):
        cols = slice(j * tf, (j + 1) * tf)
        gate = _mm(xn, wg_ref[:, cols])
        up = _mm(xn, wu_ref[:, cols])
        part = _mm(gate * _sigmoid(gate) * up, wd_ref[cols, :])
        acc = part if acc is None else acc + part
    y = x + FFN_HALF * acc
    if final:
        y = _rms(y, fg_ref[...])
    o_ref[...] = y


def _ffn(x, gain, wg, wu, wd, final_gain, *, layer, final, tm=512, tf=256):
    n, d = x.shape
    f = wg.shape[2]
    resident = lambda *shape: pl.BlockSpec((None,) + shape, lambda i: (layer, 0, 0), pipeline_mode=pl.Buffered(1))
    return pl.pallas_call(
        functools.partial(_ffn_kernel, final=final, tf=tf),
        grid=(n // tm,),
        in_specs=[
            pl.BlockSpec((tm, d), lambda i: (i, 0)),
            _layer_spec(layer, 1, d),
            resident(d, f), resident(d, f), resident(f, d),
            pl.BlockSpec((1, d), lambda i: (0, 0)),
        ],
        out_specs=pl.BlockSpec((tm, d), lambda i: (i, 0)),
        out_shape=jax.ShapeDtypeStruct((n, d), F32),
        compiler_params=pltpu.CompilerParams(
            dimension_semantics=("parallel",), vmem_limit_bytes=VMEM_LIMIT),
        name="ffn",
    )(x, gain, wg, wu, wd, final_gain)


def _in_proj_kernel(x_ref, g_ref, w_ref, o_ref, *, tn):
    xn = _rms(x_ref[...], g_ref[...]).astype(BF16)
    for j in range(w_ref.shape[1] // tn):
        cols = slice(j * tn, (j + 1) * tn)
        o_ref[:, cols] = _mm(xn, w_ref[:, cols]).astype(o_ref.dtype)


def _in_proj(x, gain, w, *, layer, tm=512, tn=1536):
    n, d = x.shape
    m = w.shape[2]
    return pl.pallas_call(
        functools.partial(_in_proj_kernel, tn=tn),
        grid=(n // tm,),
        in_specs=[
            pl.BlockSpec((tm, d), lambda i: (i, 0)),
            _layer_spec(layer, 1, d),
            pl.BlockSpec((None, d, m), lambda i: (layer, 0, 0), pipeline_mode=pl.Buffered(1)),
        ],
        out_specs=pl.BlockSpec((tm, m), lambda i: (i, 0)),
        out_shape=jax.ShapeDtypeStruct((n, m), BF16),
        compiler_params=pltpu.CompilerParams(
            dimension_semantics=("parallel",), vmem_limit_bytes=VMEM_LIMIT),
        name="in_proj",
    )(x, gain, w)


def _norm_matmul_kernel(x_ref, g_ref, w_ref, o_ref, xn_ref):
    @pl.when(pl.program_id(1) == 0)
    def _():
        xn_ref[...] = _rms(x_ref[...], g_ref[...]).astype(BF16)

    o_ref[...] = _mm(xn_ref[...], w_ref[...]).astype(o_ref.dtype)


def _norm_matmul(x, gain, w, *, layer, tm, tn, name, out_dtype):
    n, d = x.shape
    m = w.shape[2]
    return pl.pallas_call(
        _norm_matmul_kernel,
        grid=(n // tm, m // tn),
        in_specs=[
            pl.BlockSpec((tm, d), lambda i, j: (i, 0)),
            _layer_spec(layer, 1, d),
            pl.BlockSpec((None, d, tn), lambda i, j: (layer, 0, j)),
        ],
        out_specs=pl.BlockSpec((tm, tn), lambda i, j: (i, j)),
        out_shape=jax.ShapeDtypeStruct((n, m), out_dtype),
        scratch_shapes=[pltpu.VMEM((tm, d), BF16)],
        compiler_params=pltpu.CompilerParams(
            dimension_semantics=("parallel", "arbitrary"), vmem_limit_bytes=VMEM_LIMIT),
        name=name,
    )(x, gain, w)


def _mlstm_kernel(u_ref, o_ref, sm_ref, conv_ref, wq_ref, wk_ref, wv_ref, brow_ref, bcol_ref, ng_ref,
                  out_ref, uprev_ref, c_ref, m_ref):
    nb, lb = u_ref.shape[0], u_ref.shape[1]
    e = MLSTM_HEAD_DIM
    nh = MLSTM_HEADS

    @pl.when(pl.program_id(0) == 0)
    def _():
        uprev_ref[...] = jnp.zeros_like(uprev_ref)
        c_ref[...] = jnp.zeros_like(c_ref)
        m_ref[...] = jnp.zeros_like(m_ref)

    u, uc, pre, pre_t, b_cols, b_rows = [], [], [], [], [], []
    lower, upper = _tri(lb, lb), _tri(lb, lb, upper=True)
    for bi in range(nb):
        ub = u_ref[bi].astype(F32)
        conv = None
        for j in range(MLSTM_CONV):
            k = MLSTM_CONV - 1 - j
            term = conv_ref[j:j + 1, :] * (_shift_rows(uprev_ref.at[bi], ub, k) if k else ub)
            conv = term if conv is None else conv + term
        uprev_ref[bi] = ub[lb - uprev_ref.shape[1]:]
        u.append(ub)
        uc.append(conv * _sigmoid(conv))
        sm = sm_ref[bi].astype(F32)
        pre.append(sm + brow_ref[...])
        pre_t.append(sm.T[0:8, :] + bcol_ref[0:8, :])
        b_cols.append(_mm(lower, _log_sigmoid(pre[bi]), pb=3))
        b_rows.append(_mm(_log_sigmoid(pre_t[bi]), upper, pa=3))

    causal = _iota((lb, lb), 0) >= _iota((lb, lb), 1)
    streams = [(bi, h) for bi in range(nb) for h in range(nh)]
    sls = [slice(h * e, (h + 1) * e) for h in range(nh)]
    ones = jnp.ones((lb, e), BF16)
    q = {s: _mm(uc[s[0]][:, sls[s[1]]], wq_ref[s[1]]).astype(BF16) for s in streams}
    k = {s: _mm(uc[s[0]][:, sls[s[1]]], wk_ref[s[1]]) * e ** -0.5 for s in streams}
    v = {s: jnp.concatenate([_mm(u[s[0]][:, sls[s[1]]], wv_ref[s[1]]).astype(BF16), ones], axis=1) for s in streams}
    bc = {(bi, h): b_cols[bi][:, nh + h:nh + h + 1] for bi, h in streams}
    m_st = {(bi, h): m_ref[bi * SUBLANES + h:bi * SUBLANES + h + 1, 0:1] for bi, h in streams}
    dmat = {(bi, h): jnp.where(causal, bc[bi, h] - b_rows[bi][nh + h:nh + h + 1, :] + pre_t[bi][h:h + 1, :], -jnp.inf)
            for bi, h in streams}
    m_inter = {s: bc[s] + m_st[s] for s in streams}
    m_row = {s: jnp.maximum(m_inter[s], jnp.max(dmat[s], axis=-1, keepdims=True)) for s in streams}
    qk = {s: _mm(q[s], k[s], _NT) for s in streams}
    w = {s: jnp.exp(dmat[s] - m_row[s]) * qk[s] for s in streams}
    from_state = {(bi, h): _mm(q[bi, h], c_ref[bi * nh + h]) for bi, h in streams}
    from_chunk = {s: _mm(w[s], v[s]) for s in streams}
    for bi, h in streams:
        s = (bi, h)
        tot = jnp.exp(m_inter[s] - m_row[s]) * from_state[s] + from_chunk[s]
        hh = tot[:, :e] / jnp.maximum(jnp.abs(tot[:, e:e + 1]), jnp.exp(-m_row[s]))
        hn = hh * lax.rsqrt(jnp.mean(hh * hh, axis=-1, keepdims=True) + NORM_EPS) * ng_ref[:, sls[h]]
        out_ref[bi, :, sls[h]] = (hn * _sigmoid(o_ref[bi, :, sls[h]].astype(F32))).astype(out_ref.dtype)
    for bi, h in streams:
        s = (bi, h)
        b_last = bc[s][lb - 1:lb, :]
        e_col = b_last - bc[s] + pre[bi][:, h:h + 1]
        m_new = jnp.maximum(b_last + m_st[s], jnp.max(e_col, axis=0, keepdims=True))
        decay = jnp.exp(b_last + m_st[s] - m_new)
        wk = jnp.exp(e_col - m_new) * k[s]
        c_ref[bi * nh + h] = decay * c_ref[bi * nh + h] + _mm(wk.T, v[s])
        m_ref[bi * SUBLANES + h:bi * SUBLANES + h + 1, :] = jnp.broadcast_to(m_new, (1, m_ref.shape[1]))


def _mlstm(z, layer, conv_w, wq, wk, wv, b_row, b_col, norm_g):
    bsz, t, _ = z.shape
    lb = TIME_BLOCK
    hd = MLSTM_HEAD_DIM
    return pl.pallas_call(
        _mlstm_kernel,
        grid=(t // lb,),
        in_specs=[
            pl.BlockSpec((bsz, lb, WIDTH), lambda c: (0, c, ZC_AU // WIDTH)),
            pl.BlockSpec((bsz, lb, WIDTH), lambda c: (0, c, ZC_AO // WIDTH)),
            pl.BlockSpec((bsz, lb, SMALL), lambda c: (0, c, ZC_SM // SMALL)),
            _layer_spec(layer, MLSTM_CONV, WIDTH),
            _layer_spec(layer, MLSTM_HEADS, hd, hd),
            _layer_spec(layer, MLSTM_HEADS, hd, hd),
            _layer_spec(layer, MLSTM_HEADS, hd, hd),
            _layer_spec(layer, 1, SMALL),
            _layer_spec(layer, SMALL, 1),
            _layer_spec(layer, 1, WIDTH),
        ],
        out_specs=pl.BlockSpec((bsz, lb, WIDTH), lambda c: (0, c, 0)),
        out_shape=jax.ShapeDtypeStruct((bsz, t, WIDTH), BF16),
        scratch_shapes=[
            pltpu.VMEM((bsz, SUBLANES, WIDTH), F32),
            pltpu.VMEM((bsz * MLSTM_HEADS, hd, 2 * hd), F32),
            pltpu.VMEM((bsz * SUBLANES, hd), F32),
        ],
        compiler_params=pltpu.CompilerParams(
            dimension_semantics=("arbitrary",), vmem_limit_bytes=VMEM_LIMIT),
        name="mlstm",
    )(z, z, z, conv_w, wq, wk, wv, b_row, b_col, norm_g)


def _s5_kernel(u_ref, pin_ref, pout_ref, win_ref, wout_ref, lre_ref, lim_ref, d_ref, out_ref, x_ref, c_ref, *, bsz):
    steps = u_ref.shape[1]
    rows = steps * SUBLANES
    ns = S5_BLOCK_STATES
    cw = S5_BLOCK_CH
    nblk = S5_HALF // cw

    @pl.when(pl.program_id(0) == 0)
    def _():
        c_ref[...] = jnp.zeros_like(c_ref)

    u_bt = u_ref[...].reshape(bsz * steps, WIDTH)
    u = _mm(pin_ref[0], u_bt[:, :S5_HALF]) + _mm(pin_ref[1], u_bt[:, S5_HALF:])
    upper = (_iota((rows, 1), 0) % SUBLANES) >= bsz
    for j in range(nblk):
        uj = u[:, j * cw:(j + 1) * cw]
        uext = jnp.concatenate([jnp.where(upper, 0.0, uj), jnp.where(upper, uj, 0.0)], axis=1)
        x_ref[:, j * 2 * ns:(j + 1) * 2 * ns] = _mm(uext, win_ref[j])

    lre, lim = lre_ref[...], lim_ref[...]

    def step(t, carry):
        r0 = pl.multiple_of(t * SUBLANES, SUBLANES)
        new = []
        for j in range(nblk):
            sr, si = carry[2 * j], carry[2 * j + 1]
            lr, li = lre[:, j * ns:(j + 1) * ns], lim[:, j * ns:(j + 1) * ns]
            re_cols = slice(j * 2 * ns, j * 2 * ns + ns)
            im_cols = slice(j * 2 * ns + ns, (j + 1) * 2 * ns)
            nr = lr * sr - li * si + x_ref[pl.ds(r0, SUBLANES), re_cols]
            ni = lr * si + li * sr + x_ref[pl.ds(r0, SUBLANES), im_cols]
            x_ref[pl.ds(r0, SUBLANES), re_cols] = nr
            x_ref[pl.ds(r0, SUBLANES), im_cols] = ni
            new += [nr, ni]
        return tuple(new)

    init = tuple(c_ref[:, k * ns:(k + 1) * ns] for k in range(2 * nblk))
    last = lax.fori_loop(0, steps, step, init, unroll=8)
    for k in range(2 * nblk):
        c_ref[:, k * ns:(k + 1) * ns] = last[k]

    ys = []
    for j in range(nblk):
        res = _mm(x_ref[:, j * 2 * ns:(j + 1) * 2 * ns], wout_ref[j])
        cols = slice(j * cw, (j + 1) * cw)
        y = jnp.where(upper, res[:, cw:], res[:, :cw]) + d_ref[:, cols] * u[:, cols]
        ys.append(0.5 * y * (1.0 + jnp.tanh(math.sqrt(2.0 / math.pi) * (y + 0.044715 * y * y * y))))
    y = jnp.concatenate(ys, axis=1).astype(BF16)
    out = jnp.concatenate([_mm(pout_ref[0], y), _mm(pout_ref[1], y)], axis=1)
    out_ref[...] = out.astype(out_ref.dtype).reshape(bsz, steps, WIDTH)


def _s5_permutations(bsz, steps):
    rows = steps * 2 * bsz
    r = np.arange(rows)
    p = np.zeros((2, rows, bsz * steps), np.float32)
    p[(r % (2 * bsz)) // bsz, r, (r % bsz) * steps + r // (2 * bsz)] = 1.0
    return jnp.asarray(p, BF16), jnp.asarray(p.transpose(0, 2, 1), BF16)


def _s5(z, bsz, layer, w_in, w_out, lam_re, lam_im, d_tab):
    t = z.shape[1]
    steps = S5_STEPS
    rows = steps * SUBLANES
    nblk = S5_HALF // S5_BLOCK_CH
    nst = nblk * S5_BLOCK_STATES
    p_in, p_out = _s5_permutations(bsz, steps)
    whole = lambda *shape: pl.BlockSpec(shape, lambda i: (0,) * len(shape))
    return pl.pallas_call(
        functools.partial(_s5_kernel, bsz=bsz),
        grid=(t // steps,),
        in_specs=[
            pl.BlockSpec((bsz, steps, WIDTH), lambda i: (0, i, ZC_BU // WIDTH)),
            whole(2, rows, bsz * steps),
            whole(2, bsz * steps, rows),
            _layer_spec(layer, nblk, 2 * S5_BLOCK_CH, 2 * S5_BLOCK_STATES),
            _layer_spec(layer, nblk, 2 * S5_BLOCK_STATES, 2 * S5_BLOCK_CH),
            _layer_spec(layer, SUBLANES, nst),
            _layer_spec(layer, SUBLANES, nst),
            _layer_spec(layer, rows, S5_HALF),
        ],
        out_specs=pl.BlockSpec((bsz, steps, WIDTH), lambda i: (0, i, 0)),
        out_shape=jax.ShapeDtypeStruct((bsz, t, WIDTH), BF16),
        scratch_shapes=[pltpu.VMEM((rows, 2 * nst), F32), pltpu.VMEM((SUBLANES, 2 * nst), F32)],
        compiler_params=pltpu.CompilerParams(
            dimension_semantics=("arbitrary",), vmem_limit_bytes=VMEM_LIMIT),
        name="s5",
    )(z, p_in, p_out, w_in, w_out, lam_re, lam_im, d_tab)


def _gla_kernel(q_ref, k_ref, v_ref, g_ref, sm_ref, aup_ref, ab_ref, ng_ref, out_ref, st_ref):
    lb = q_ref.shape[0]
    ch = GLA_CHUNK
    kw = GLA_KEY_WIDTH

    @pl.when(pl.program_id(1) == 0)
    def _():
        st_ref[...] = jnp.zeros_like(st_ref)

    x = _mm(sm_ref[...], aup_ref[...], pb=2) + ab_ref[...]
    log_a = _log_sigmoid(x) * (1.0 / GLA_GATE_TAU)
    b_all = _mm(_tri(lb, ch), log_a, pb=3)
    lane_head = _iota((1, kw), 1) // GLA_HEAD_K
    masks = [jnp.where(lane_head == h, 1.0, 0.0).astype(BF16) for h in range(GLA_HEADS)]
    causal = _iota((ch, ch), 0) >= _iota((ch, ch), 1)
    mid = ch // 2 - 1
    heads = range(GLA_HEADS)
    chunks = range(lb // ch)
    rss = [slice(c * ch, (c + 1) * ch) for c in chunks]
    vss = [slice(h * GLA_HEAD_V, (h + 1) * GLA_HEAD_V) for h in heads]

    q_in, q_at, k_at, k_st, dec = [], [], [], [], []
    for c in chunks:
        b = b_all[rss[c]]
        b_mid, b_last = b[mid:mid + 1, :], b[ch - 1:ch, :]
        q = q_ref[rss[c], :].astype(F32) * GLA_HEAD_K ** -0.5
        k = k_ref[rss[c], :].astype(F32)
        q_in.append((q * jnp.exp(b)).astype(BF16))
        q_at.append((q * jnp.exp(b - b_mid)).astype(BF16))
        k_at.append((k * jnp.exp(b_mid - b)).astype(BF16))
        k_st.append((k * jnp.exp(b_last - b)).astype(BF16))
        dec.append(jnp.exp(b_last))
    v = [[v_ref[rss[c], vss[h]] for h in heads] for c in chunks]
    attn = [[jnp.where(causal, _mm(q_at[c] * masks[h], k_at[c], _NT), 0.0) for h in heads] for c in chunks]
    grow = [[_mm(v[c][h].astype(F32).T, k_st[c] * masks[h]) for h in heads] for c in chunks]
    state = [st_ref[h] for h in heads]
    for c in chunks:
        for h in heads:
            o = _mm(q_in[c] * masks[h], state[h], _NT) + _mm(attn[c][h], v[c][h])
            state[h] = dec[c] * state[h] + grow[c][h]
            on = o * lax.rsqrt(jnp.mean(o * o, axis=-1, keepdims=True) + NORM_EPS) * ng_ref[:, vss[h]]
            g = g_ref[rss[c], vss[h]].astype(F32)
            out_ref[rss[c], vss[h]] = (on * (g * _sigmoid(g))).astype(out_ref.dtype)
    for h in heads:
        st_ref[h] = state[h]


def _gla(z, bsz, layer, a_up, a_bias, norm_g):
    n = z.shape[0]
    lb = TIME_BLOCK
    nblk = n // bsz // lb
    row = lambda b, c: b * nblk + c
    return pl.pallas_call(
        _gla_kernel,
        grid=(bsz, nblk),
        in_specs=[
            pl.BlockSpec((lb, GLA_KEY_WIDTH), lambda b, c: (row(b, c), ZC_CQ // GLA_KEY_WIDTH)),
            pl.BlockSpec((lb, GLA_KEY_WIDTH), lambda b, c: (row(b, c), ZC_CK // GLA_KEY_WIDTH)),
            pl.BlockSpec((lb, WIDTH), lambda b, c: (row(b, c), ZC_CV // WIDTH)),
            pl.BlockSpec((lb, WIDTH), lambda b, c: (row(b, c), ZC_CG // WIDTH)),
            pl.BlockSpec((lb, SMALL), lambda b, c: (row(b, c), ZC_SM // SMALL)),
            _layer_spec(layer, SMALL, GLA_KEY_WIDTH),
            _layer_spec(layer, 1, GLA_KEY_WIDTH),
            _layer_spec(layer, 1, WIDTH),
        ],
        out_specs=pl.BlockSpec((lb, WIDTH), lambda b, c: (row(b, c), 0)),
        out_shape=jax.ShapeDtypeStruct((n, WIDTH), BF16),
        scratch_shapes=[pltpu.VMEM((GLA_HEADS, GLA_HEAD_V, GLA_KEY_WIDTH), F32)],
        compiler_params=pltpu.CompilerParams(
            dimension_semantics=("parallel", "arbitrary"), vmem_limit_bytes=VMEM_LIMIT),
        name="gla",
    )(z, z, z, z, z, a_up, a_bias, norm_g)


def _rwkv_kernel(r_ref, k_ref, v_ref, t_ref, mu_ref, mut_ref,
                 w0_ref, wup_ref, a0_ref, aup_ref, gup_ref, kk_ref, ka_ref, rk_ref, ng_ref,
                 out_ref, pr_ref, pk_ref, pv_ref, pt_ref, st_ref, y_ref):
    nb, ch = r_ref.shape[0], r_ref.shape[1]
    lb = nb * ch
    gl = RWKV_GROUP_LANES
    gh = RWKV_GROUP_HEADS

    @pl.when(pl.program_id(0) == 0)
    def _():
        for ref in (pr_ref, pk_ref, pv_ref, pt_ref, st_ref):
            ref[...] = jnp.zeros_like(ref)

    def shift_mix(x_ref, prev_ref, mu):
        mixed = []
        for bi in range(nb):
            x = x_ref[bi].astype(F32)
            xs = _shift_rows(prev_ref.at[bi], x, 1)
            prev_ref[bi] = x[ch - prev_ref.shape[1]:]
            mixed.append(x + mu * (xs - x))
        return jnp.concatenate(mixed, axis=0)

    r = shift_mix(r_ref, pr_ref, mu_ref[0:1, :])
    k = shift_mix(k_ref, pk_ref, mu_ref[1:2, :])
    v = shift_mix(v_ref, pv_ref, mu_ref[2:3, :])
    tl = shift_mix(t_ref, pt_ref, mut_ref[...])

    ones_bd = _tri(gl, RWKV_HEAD) + _tri(gl, RWKV_HEAD, upper=True, strict=True)

    def head_sum(x):
        return jnp.concatenate([_mm(x[:, i * gl:(i + 1) * gl], ones_bd, pa=2) for i in range(WIDTH // gl)], axis=1)

    w_log = _log_sigmoid(w0_ref[...] + _mm(jnp.tanh(tl), wup_ref[...])) - 0.5
    log_w = -jnp.exp(w_log)
    a = _sigmoid(a0_ref[...] + _mm(tl, aup_ref[...]))
    g = _mm(_sigmoid(tl), gup_ref[...])
    kk = k * kk_ref[...]
    kk = kk / jnp.maximum(jnp.sqrt(head_sum(kk * kk)), 1e-12)
    k_rep = k * (1.0 + (a - 1.0) * ka_ref[...])
    av = -kk
    bv = kk * a
    bonus = head_sum(r * k_rep * rk_ref[...]) * v
    gc_all = _mm(_tri(lb, ch), log_w, pb=3)

    lane_head = _iota((1, gl), 1) // RWKV_HEAD
    masks = [jnp.where(lane_head == h, 1.0, 0.0).astype(BF16) for h in range(gh)]

    def stack(x):
        xb = x.astype(BF16)
        return jnp.concatenate([xb * m for m in masks], axis=0)

    def stack_t(x):
        return jnp.concatenate([x * m.astype(F32) for m in masks], axis=0).T

    s_lane = _iota((ch, gl), 1) % ch
    t_row = _iota((ch, gl), 0)
    strict = t_row > s_lane
    incl = t_row >= s_lane
    eye = jnp.where(t_row == s_lane, 1.0, 0.0)

    pairs = [(gi, bi) for gi in range(WIDTH // gl) for bi in range(nb)]
    pre = {}
    for gi, bi in pairs:
        ls = slice(gi * gl, (gi + 1) * gl)
        rs = slice(bi * ch, (bi + 1) * ch)
        gc = gc_all[rs, ls]
        g_last = gc[ch - 1:ch, :]
        e_inv = jnp.exp(-gc)
        e_end = jnp.exp(g_last - gc)
        r_t = (r[rs, ls] * jnp.exp(gc)).astype(BF16)
        a_t = (av[rs, ls] * jnp.exp(gc - log_w[rs, ls])).astype(BF16)
        scores = _mm(jnp.concatenate([a_t, r_t], axis=0),
                     jnp.concatenate([stack(bv[rs, ls] * e_inv), stack(k_rep[rs, ls] * e_inv)], axis=0), _NT)
        pre[gi, bi] = dict(
            r_t=r_t, a_bd=stack(a_t), v_bd=stack(v[rs, ls]), dec=jnp.exp(g_last),
            b_e=stack(bv[rs, ls] * e_end), k_e=stack(k_rep[rs, ls] * e_end),
            a_ab=jnp.where(strict, scores[:ch, :gl], 0.0),
            a_kv=jnp.concatenate([jnp.where(strict, scores[:ch, gl:], 0.0),
                                  jnp.where(incl, scores[ch:, gl:], 0.0)], axis=0).astype(BF16),
            a_rb=jnp.where(incl, scores[ch:, :gl], 0.0).astype(BF16))
    tinv = {key: eye + pre[key]["a_ab"] for key in pairs}
    power = {key: pre[key]["a_ab"].astype(BF16) for key in pairs}
    for key in pairs:
        power[key] = _mm(power[key], stack(power[key])).astype(BF16)
    step = 2
    while step < ch:
        for key in pairs:
            if 2 * step < ch:
                both = _mm(jnp.concatenate([power[key], tinv[key].astype(BF16)], axis=0), stack(power[key]))
                power[key] = both[:ch].astype(BF16)
                tinv[key] = tinv[key] + both[ch:]
            else:
                tinv[key] = tinv[key] + _mm(tinv[key], stack(power[key]))
        step *= 2

    from_v = {key: _mm(pre[key]["a_kv"], pre[key]["v_bd"]) for key in pairs}
    t_bf = {key: tinv[key].astype(BF16) for key in pairs}
    a_hat = {key: _mm(t_bf[key], pre[key]["a_bd"]).astype(BF16) for key in pairs}
    u0 = {key: _mm(t_bf[key], stack(from_v[key][:ch])) for key in pairs}
    grow = {key: _mm(pre[key]["v_bd"].astype(F32).T, pre[key]["k_e"]) for key in pairs}
    n_groups = WIDTH // gl
    state = {(gi, bi): st_ref[bi * n_groups + gi] for gi, bi in pairs}
    from_state = {key: _mm(jnp.concatenate([a_hat[key], pre[key]["r_t"]], axis=0), state[key], _NT) for key in pairs}
    u = {key: from_state[key][:ch] + u0[key] for key in pairs}
    for gi, bi in pairs:
        key = (gi, bi)
        y_ref[bi * ch:(bi + 1) * ch, gi * gl:(gi + 1) * gl] = (
            from_state[key][ch:] + _mm(pre[key]["a_rb"], stack(u[key])) + from_v[key][ch:])
    for gi, bi in pairs:
        key = (gi, bi)
        st_ref[bi * n_groups + gi] = (pre[key]["dec"] * state[key] + _mm(stack_t(u[key]), pre[key]["b_e"])
                                      + grow[key])

    y = y_ref[...]
    inv_n = 1.0 / RWKV_HEAD
    mu = head_sum(y) * inv_n
    yc = y - mu
    var = head_sum(yc * yc) * inv_n
    yn = yc * lax.rsqrt(var + RWKV_GN_EPS) * ng_ref[...]
    out_ref[...] = ((yn + bonus) * g).astype(out_ref.dtype).reshape(nb, ch, WIDTH)


def _rwkv(z, layer, mu_rkv, mu_t, w0, w_up, a0, a_up, g_up, k_k, k_a, r_k, norm_g):
    bsz, t, _ = z.shape
    ch = RWKV_CHUNK
    gl = RWKV_GROUP_LANES
    vec = _layer_spec(layer, 1, WIDTH)
    mat = _layer_spec(layer, RWKV_TAIL, WIDTH)
    return pl.pallas_call(
        _rwkv_kernel,
        grid=(t // ch,),
        in_specs=[
            pl.BlockSpec((bsz, ch, WIDTH), lambda c: (0, c, ZC_DR // WIDTH)),
            pl.BlockSpec((bsz, ch, WIDTH), lambda c: (0, c, ZC_DK // WIDTH)),
            pl.BlockSpec((bsz, ch, WIDTH), lambda c: (0, c, ZC_DV // WIDTH)),
            pl.BlockSpec((bsz, ch, RWKV_TAIL), lambda c: (0, c, ZC_DT // RWKV_TAIL)),
            _layer_spec(layer, 3, WIDTH), _layer_spec(layer, 1, RWKV_TAIL),
            vec, mat, vec, mat, mat, vec, vec, vec, vec,
        ],
        out_specs=pl.BlockSpec((bsz, ch, WIDTH), lambda c: (0, c, 0)),
        out_shape=jax.ShapeDtypeStruct((bsz, t, WIDTH), BF16),
        scratch_shapes=[
            pltpu.VMEM((bsz, SUBLANES, WIDTH), F32), pltpu.VMEM((bsz, SUBLANES, WIDTH), F32),
            pltpu.VMEM((bsz, SUBLANES, WIDTH), F32), pltpu.VMEM((bsz, SUBLANES, RWKV_TAIL), F32),
            pltpu.VMEM((bsz * (WIDTH // gl), gl, gl), F32),
            pltpu.VMEM((bsz * ch, WIDTH), F32),
        ],
        compiler_params=pltpu.CompilerParams(
            dimension_semantics=("arbitrary",), vmem_limit_bytes=VMEM_LIMIT),
        name="rwkv",
    )(z, z, z, z, mu_rkv, mu_t, w0, w_up, a0, a_up, g_up, k_k, k_a, r_k, norm_g)


def _merge_kernel(x_ref, ya_ref, zb_ref, yc_ref, yd_ref, g0_ref, g1_ref, g2_ref, g3_ref, gb_ref,
                  pa_ref, w1_ref, w2_ref, pc_ref, pd_ref, wo_ref, out_ref):
    d = D_MODEL

    def gate(g_ref, i):
        return _sigmoid(g_ref[...].astype(F32) + gb_ref[:, i * d:(i + 1) * d])

    zb = zb_ref[...]
    merged = gate(g0_ref, 0) * _mm(ya_ref[...], pa_ref[...])
    merged = merged + gate(g1_ref, 1) * (_mm(zb, w1_ref[...]) * _sigmoid(_mm(zb, w2_ref[...])))
    merged = merged + gate(g2_ref, 2) * _mm(yc_ref[...], pc_ref[...])
    merged = merged + gate(g3_ref, 3) * _mm(yd_ref[...], pd_ref[...])
    out_ref[...] = x_ref[...] + _mm(merged, wo_ref[...])


def _merge(x, z, ya, zb, yc, yd, layer, gate_bias, pa, w1, w2, pc, pd, wo, *, tm=512):
    n, d = x.shape
    tile = lambda w: pl.BlockSpec((tm, w), lambda i: (i, 0))
    gate = lambda b: pl.BlockSpec((tm, d), lambda i: (i, ZC_GATE // d + b))
    resident = lambda rows: pl.BlockSpec((None, rows, d), lambda i: (layer, 0, 0), pipeline_mode=pl.Buffered(1))
    proj = resident(WIDTH)
    return pl.pallas_call(
        _merge_kernel,
        grid=(n // tm,),
        in_specs=[tile(d), tile(WIDTH), tile(WIDTH), tile(WIDTH), tile(WIDTH),
                  gate(0), gate(1), gate(2), gate(3), _layer_spec(layer, 1, N_BRANCH * d),
                  proj, proj, proj, proj, proj, resident(d)],
        out_specs=tile(d),
        out_shape=jax.ShapeDtypeStruct((n, d), F32),
        compiler_params=pltpu.CompilerParams(
            dimension_semantics=("parallel",), vmem_limit_bytes=VMEM_LIMIT),
        name="merge",
    )(x, ya, zb, yc, yd, z, z, z, z, gate_bias, pa, w1, w2, pc, pd, wo)


def _xattn_kernel(x_ref, g_ref, wq_ref, kv_ref, wo_ref, out_ref):
    d = D_MODEL
    hd = XATTN_HEAD_DIM
    x = x_ref[...]
    q = _mm(_rms(x, g_ref[...]), wq_ref[...])
    kv = kv_ref[...]
    outs = []
    for h in range(XATTN_HEADS):
        sl = slice(h * hd, (h + 1) * hd)
        s = _mm(q[:, sl], kv[:, sl], _NT) * hd ** -0.5
        s = s - jnp.max(s, axis=-1, keepdims=True)
        p = jnp.exp(s)
        p = p / jnp.sum(p, axis=-1, keepdims=True)
        outs.append(_mm(p, kv[:, d + h * hd:d + (h + 1) * hd]))
    o = jnp.concatenate(outs, axis=1)
    out_ref[...] = x + _mm(o, wo_ref[...])


def _xattn(x, bsz, layer, gain, wq, kv, wo, *, tm=512):
    n, d = x.shape
    nblk = n // bsz // tm
    m_len = kv.shape[0] // bsz
    return pl.pallas_call(
        _xattn_kernel,
        grid=(bsz, nblk),
        in_specs=[
            pl.BlockSpec((tm, d), lambda b, c: (b * nblk + c, 0)),
            _layer_spec(layer, 1, d),
            _layer_spec(layer, d, d),
            pl.BlockSpec((m_len, 2 * d), lambda b, c: (b, 0)),
            _layer_spec(layer, d, d),
        ],
        out_specs=pl.BlockSpec((tm, d), lambda b, c: (b * nblk + c, 0)),
        out_shape=jax.ShapeDtypeStruct((n, d), F32),
        compiler_params=pltpu.CompilerParams(
            dimension_semantics=("parallel", "parallel"), vmem_limit_bytes=VMEM_LIMIT),
        name="xattn",
    )(x, gain, wq, kv, wo)


def _w_in_segments():
    o = np.cumsum((0, WIDTH, WIDTH, MLSTM_HEADS, MLSTM_HEADS, WIDTH, GLA_KEY_WIDTH, GLA_KEY_WIDTH, WIDTH, WIDTH,
                   GLA_GATE_RANK, 3 * WIDTH + RWKV_TAIL, N_BRANCH * D_MODEL))
    a_u, a_o, a_i, a_f, b_u, c_q, c_k, c_v, c_g, c_a, d_z, gate = range(12)
    seg = lambda i: (int(o[i]), int(o[i + 1] - o[i]))
    d_parts = [(int(o[d_z]) + i * WIDTH, WIDTH) for i in range(3)] + [(int(o[d_z]) + 3 * WIDTH, RWKV_TAIL)]
    return ([seg(gate), seg(a_u), seg(a_o), seg(b_u), seg(c_v), seg(c_g)] + d_parts[:3]
            + [seg(c_q), seg(c_k), d_parts[3], seg(a_i), seg(a_f), seg(c_a)])


def _regroup_kernel(w_ref, o_ref):
    dst = 0
    for src, width in _w_in_segments():
        o_ref[:, dst:dst + width] = w_ref[:, src:src + width].astype(o_ref.dtype)
        dst += width
    o_ref[:, dst:] = jnp.zeros((o_ref.shape[0], o_ref.shape[1] - dst), o_ref.dtype)


def _regroup_w_in(w, *, rows=256):
    depth, d, n_in = w.shape
    return pl.pallas_call(
        _regroup_kernel,
        grid=(depth, d // rows),
        in_specs=[pl.BlockSpec((None, rows, n_in), lambda l, i: (l, i, 0))],
        out_specs=pl.BlockSpec((None, rows, Z_COLS), lambda l, i: (l, i, 0)),
        out_shape=jax.ShapeDtypeStruct((depth, d, Z_COLS), BF16),
        compiler_params=pltpu.CompilerParams(
            dimension_semantics=("parallel", "parallel"), vmem_limit_bytes=VMEM_LIMIT),
        name="regroup_w_in",
    )(w)


def _s5_tables(a_re, a_im, log_step, b_re, b_im, c_re, c_im, d_skip, bsz):
    depth = a_re.shape[0]
    step = jnp.exp(log_step)[..., None]
    lam_re = jnp.minimum(a_re, -S5_MIN_NEG)
    lam_im = a_im
    mag = jnp.exp(lam_re * step)
    bar_re = mag * jnp.cos(lam_im * step)
    bar_im = mag * jnp.sin(lam_im * step)
    denom = lam_re * lam_re + lam_im * lam_im
    coef_re = ((bar_re - 1.0) * lam_re + bar_im * lam_im) / denom
    coef_im = (bar_im * lam_re - (bar_re - 1.0) * lam_im) / denom
    bb_re = coef_re[..., None] * b_re - coef_im[..., None] * b_im
    bb_im = coef_re[..., None] * b_im + coef_im[..., None] * b_re
    nb = S5_HALF // S5_BLOCK_CH
    gpb = S5_BLOCK_GROUPS
    eye = jnp.eye(gpb, dtype=F32)

    def in_block(bb):
        bb = bb.reshape(depth, 2, nb, gpb, S5_STATE, S5_GROUP)
        return jnp.einsum('lhjgpc,gk->lhjgckp', bb, eye).reshape(depth, 2, nb, S5_BLOCK_CH, S5_BLOCK_STATES)

    def out_block(cc):
        cc = cc.reshape(depth, 2, nb, gpb, S5_GROUP, S5_STATE)
        return jnp.einsum('lhjgcp,gk->lhjgpkc', cc, eye).reshape(depth, 2, nb, S5_BLOCK_STATES, S5_BLOCK_CH)

    w_in = jnp.concatenate([in_block(bb_re), in_block(bb_im)], axis=-1)
    w_in = w_in.transpose(0, 2, 1, 3, 4).reshape(depth, nb, 2 * S5_BLOCK_CH, 2 * S5_BLOCK_STATES)
    w_out = jnp.concatenate([out_block(c_re), -out_block(c_im)], axis=-2)
    w_out = w_out.transpose(0, 2, 3, 1, 4).reshape(depth, nb, 2 * S5_BLOCK_STATES, 2 * S5_BLOCK_CH)

    def rows(t, width):
        t = t.reshape(depth, 2, 1, width)
        return jnp.broadcast_to(t, (depth, 2, bsz, width)).reshape(depth, 2 * bsz, width)

    nst = nb * S5_BLOCK_STATES
    lam_re_rows = rows(bar_re.reshape(depth, 2 * nst), nst)
    lam_im_rows = rows(bar_im.reshape(depth, 2 * nst), nst)
    d_tab = jnp.tile(rows(d_skip, S5_HALF), (1, S5_STEPS, 1))
    return w_in.astype(BF16), w_out.astype(BF16), lam_re_rows, lam_im_rows, d_tab


def _pad_rows(w, start, total):
    return jnp.pad(w, ((0, 0), (start, total - start - w.shape[1]), (0, 0)))


def kernel(x, mem, ffn1_norm, ffn1_w_gate, ffn1_w_up, ffn1_w_down, mix_norm, w_in, gate_bias, mlstm_conv, mlstm_wq, mlstm_wk, mlstm_wv, mlstm_b_i, mlstm_b_f, mlstm_norm, mlstm_proj, s5_a_re, s5_a_im, s5_log_step, s5_b_re, s5_b_im, s5_c_re, s5_c_im, s5_d, s5_glu_w1, s5_glu_w2, gla_a_up, gla_a_bias, gla_norm, gla_proj, rwkv_mu, rwkv_w0, rwkv_w_up, rwkv_a0, rwkv_a_up, rwkv_g_up, rwkv_k_k, rwkv_k_a, rwkv_r_k, rwkv_norm, rwkv_proj, w_out, xattn_norm, mem_norm, xattn_wq, xattn_wk, xattn_wv, xattn_wo, ffn2_norm, ffn2_w_gate, ffn2_w_up, ffn2_w_down, final_norm):
    bsz, t, d = x.shape
    depth = w_in.shape[0]
    assert d == D_MODEL and t % TIME_BLOCK == 0 and 2 * bsz == SUBLANES
    xs = x.reshape(bsz * t, d)
    mems = mem.reshape(bsz * mem.shape[1], d)
    bf = lambda w: w.astype(BF16)
    vec = lambda v: v.reshape(depth, 1, -1)
    fin = final_norm.reshape(1, d)

    w_in_r = _regroup_w_in(w_in)
    gate_b = jnp.pad(jnp.concatenate([mlstm_b_i, mlstm_b_f], axis=1), ((0, 0), (0, SMALL - 2 * MLSTM_HEADS)))
    s5_w_in, s5_w_out, s5_lre, s5_lim, s5_dtab = _s5_tables(
        s5_a_re, s5_a_im, s5_log_step, s5_b_re, s5_b_im, s5_c_re, s5_c_im, s5_d, bsz)
    gla_up = _pad_rows(gla_a_up, 2 * MLSTM_HEADS, SMALL)
    mu_rkv = rwkv_mu[:, :3 * WIDTH].reshape(depth, 3, WIDTH)
    mu_t = vec(rwkv_mu[:, 3 * WIDTH:])
    rw_wup = bf(_pad_rows(rwkv_w_up, 0, RWKV_TAIL))
    rw_aup = bf(_pad_rows(rwkv_a_up, RWKV_DECAY_RANK, RWKV_TAIL))
    rw_gup = bf(_pad_rows(rwkv_g_up, RWKV_DECAY_RANK + RWKV_ICLR_RANK, RWKV_TAIL))
    w_kv = bf(jnp.concatenate([xattn_wk, xattn_wv], axis=2))
    wq_a, wk_a, wv_a = bf(mlstm_wq), bf(mlstm_wk), bf(mlstm_wv)
    p_a, p_b1, p_b2, p_c, p_d, p_o = (bf(mlstm_proj), bf(s5_glu_w1), bf(s5_glu_w2), bf(gla_proj), bf(rwkv_proj),
                                      bf(w_out))
    x_wq, x_wo = bf(xattn_wq), bf(xattn_wo)
    f1_g, f1_u, f1_d = bf(ffn1_w_gate), bf(ffn1_w_up), bf(ffn1_w_down)
    f2_g, f2_u, f2_d = bf(ffn2_w_gate), bf(ffn2_w_up), bf(ffn2_w_down)

    for l in range(depth):
        xs = _ffn(xs, vec(ffn1_norm), f1_g, f1_u, f1_d, fin, layer=l, final=False)

        z = _in_proj(xs, vec(mix_norm), w_in_r, layer=l)

        z3 = z.reshape(bsz, t, Z_COLS)
        y_a = _mlstm(z3, l, mlstm_conv, wq_a, wk_a, wv_a, vec(gate_b), gate_b.reshape(depth, SMALL, 1),
                     vec(mlstm_norm)).reshape(bsz * t, WIDTH)

        z_b = _s5(z3, bsz, l, s5_w_in, s5_w_out, s5_lre, s5_lim, s5_dtab).reshape(bsz * t, WIDTH)

        y_c = _gla(z, bsz, l, gla_up, vec(gla_a_bias), vec(gla_norm))

        y_d = _rwkv(z3, l, mu_rkv, mu_t, vec(rwkv_w0), rw_wup, vec(rwkv_a0), rw_aup, rw_gup,
                    vec(rwkv_k_k), vec(rwkv_k_a), vec(rwkv_r_k), vec(rwkv_norm)).reshape(bsz * t, WIDTH)

        xs = _merge(xs, z, y_a, z_b, y_c, y_d, l, vec(gate_bias), p_a, p_b1, p_b2, p_c, p_d, p_o)

        kv = _norm_matmul(mems, vec(mem_norm), w_kv, layer=l, tm=mems.shape[0], tn=512, name="mem_kv",
                          out_dtype=BF16)
        xs = _xattn(xs, bsz, l, vec(xattn_norm), x_wq, kv, x_wo)

        xs = _ffn(xs, vec(ffn2_norm), f2_g, f2_u, f2_d, fin, layer=l,
                  final=(l == depth - 1))
    return xs.reshape(bsz, t, d)
```

```python
import functools
import math

import jax
import jax.numpy as jnp
import numpy as np
from jax import lax
from jax.experimental import pallas as pl
from jax.experimental.pallas import tpu as pltpu

F32 = jnp.float32
BF16 = jnp.bfloat16

D_MODEL = 1024
D_FF = 11 * D_MODEL // 4
NORM_EPS = 1e-6
FFN_HALF = 0.5
N_BRANCH = 4
WIDTH = D_MODEL // 2

MLSTM_HEADS = 4
MLSTM_HEAD_DIM = WIDTH // MLSTM_HEADS
MLSTM_CONV = 4
MLSTM_CHUNK = 256

S5_GROUP = 16
S5_GROUPS = WIDTH // S5_GROUP
S5_STATE = 64
S5_MIN_NEG = 1e-4
S5_HALF = WIDTH // 2
S5_BLOCK_GROUPS = 8
S5_BLOCK_CH = S5_BLOCK_GROUPS * S5_GROUP
S5_BLOCK_STATES = S5_BLOCK_GROUPS * S5_STATE
S5_STEPS = 64
SUBLANES = 8

GLA_HEADS = 4
GLA_KEY_WIDTH = WIDTH // 2
GLA_HEAD_K = GLA_KEY_WIDTH // GLA_HEADS
GLA_HEAD_V = WIDTH // GLA_HEADS
GLA_GATE_RANK = 16
GLA_GATE_TAU = 16.0
GLA_CHUNK = 64

RWKV_HEAD = 64
RWKV_HEADS = WIDTH // RWKV_HEAD
RWKV_DECAY_RANK = 64
RWKV_ICLR_RANK = 64
RWKV_GATE_RANK = 128
RWKV_TAIL = RWKV_DECAY_RANK + RWKV_ICLR_RANK + RWKV_GATE_RANK
RWKV_GN_EPS = 64e-5
RWKV_CHUNK = 64
RWKV_GROUP_LANES = 256
RWKV_GROUP_HEADS = RWKV_GROUP_LANES // RWKV_HEAD

XATTN_HEADS = 4
XATTN_HEAD_DIM = D_MODEL // XATTN_HEADS

ZC_GATE = 0
ZC_AU = 4096
ZC_AO = 4608
ZC_BU = 5120
ZC_CV = 5632
ZC_CG = 6144
ZC_DR = 6656
ZC_DK = 7168
ZC_DV = 7680
ZC_CQ = 8192
ZC_CK = 8448
ZC_DT = 8704
ZC_SM = 8960
Z_COLS = 9216
SMALL = 128

TIME_BLOCK = 256
VMEM_LIMIT = 56 * 1024 * 1024

_NN = (((1,), (0,)), ((), ()))
_NT = (((1,), (1,)), ((), ()))


def _split(x, n):
    parts = []
    rest = x
    for i in range(n):
        p = rest.astype(BF16)
        parts.append(p)
        if i + 1 < n:
            rest = rest - p.astype(F32)
    return parts


def _mm(a, b, dn=_NN, pa=1, pb=1):
    ap, bp = _split(a, pa), _split(b, pb)
    order = max(pa, pb)
    out = None
    for i, x in enumerate(ap):
        for j, y in enumerate(bp):
            if i + j < order:
                t = lax.dot_general(x, y, dn, preferred_element_type=F32)
                out = t if out is None else out + t
    return out


def _rms(x, g):
    return x * lax.rsqrt(jnp.mean(x * x, axis=-1, keepdims=True) + NORM_EPS) * g


def _sigmoid(x):
    return 0.5 * jnp.tanh(0.5 * x) + 0.5


def _log_sigmoid(x):
    return jnp.minimum(x, 0.0) - jnp.log(1.0 + jnp.exp(-jnp.abs(x)))


def _iota(shape, dim):
    return lax.broadcasted_iota(jnp.int32, shape, dim)


def _tri(n, block, upper=False, strict=False):
    r, c = _iota((n, n), 0), _iota((n, n), 1)
    same = (r // block) == (c // block)
    if upper:
        keep = (r < c) if strict else (r <= c)
    else:
        keep = (r > c) if strict else (r >= c)
    return jnp.where(same & keep, 1.0, 0.0).astype(BF16)


def _shift_rows(prev_ref, x, k):
    ext = jnp.concatenate([prev_ref[...], x], axis=0)
    return pltpu.roll(ext, k, 0)[prev_ref.shape[0]:]


def _layer_spec(layer, *shape):
    return pl.BlockSpec((None,) + shape, lambda *_: (layer,) + (0,) * len(shape))


def _ffn_kernel(x_ref, g_ref, wg_ref, wu_ref, wd_ref, fg_ref, o_ref, *, final, tf):
    x = x_ref[...]
    xn = _rms(x, g_ref[...]).astype(BF16)
    acc = None
    for j in range(wg_ref.shape[1] // tf):
        cols = slice(j * tf, (j + 1) * tf)
        gate = _mm(xn, wg_ref[:, cols])
        up = _mm(xn, wu_ref[:, cols])
        part = _mm(gate * _sigmoid(gate) * up, wd_ref[cols, :])
        acc = part if acc is None else acc + part
    y = x + FFN_HALF * acc
    if final:
        y = _rms(y, fg_ref[...])
    o_ref[...] = y


def _ffn(x, gain, wg, wu, wd, final_gain, *, layer, final, tm=512, tf=256):
    n, d = x.shape
    f = wg.shape[2]
    resident = lambda *shape: pl.BlockSpec((None,) + shape, lambda i: (layer, 0, 0), pipeline_mode=pl.Buffered(1))
    return pl.pallas_call(
        functools.partial(_ffn_kernel, final=final, tf=tf),
        grid=(n // tm,),
        in_specs=[
            pl.BlockSpec((tm, d), lambda i: (i, 0)),
            _layer_spec(layer, 1, d),
            resident(d, f), resident(d, f), resident(f, d),
            pl.BlockSpec((1, d), lambda i: (0, 0)),
        ],
        out_specs=pl.BlockSpec((tm, d), lambda i: (i, 0)),
        out_shape=jax.ShapeDtypeStruct((n, d), F32),
        compiler_params=pltpu.CompilerParams(
            dimension_semantics=("parallel",), vmem_limit_bytes=VMEM_LIMIT),
        name="ffn",
    )(x, gain, wg, wu, wd, final_gain)


def _in_proj_kernel(x_ref, g_ref, w_ref, o_ref, *, tn):
    xn = _rms(x_ref[...], g_ref[...]).astype(BF16)
    for j in range(w_ref.shape[1] // tn):
        cols = slice(j * tn, (j + 1) * tn)
        o_ref[:, cols] = _mm(xn, w_ref[:, cols]).astype(o_ref.dtype)


def _in_proj(x, gain, w, *, layer, tm=512, tn=1536):
    n, d = x.shape
    m = w.shape[2]
    return pl.pallas_call(
        functools.partial(_in_proj_kernel, tn=tn),
        grid=(n // tm,),
        in_specs=[
            pl.BlockSpec((tm, d), lambda i: (i, 0)),
            _layer_spec(layer, 1, d),
            pl.BlockSpec((None, d, m), lambda i: (layer, 0, 0), pipeline_mode=pl.Buffered(1)),
        ],
        out_specs=pl.BlockSpec((tm, m), lambda i: (i, 0)),
        out_shape=jax.ShapeDtypeStruct((n, m), BF16),
        compiler_params=pltpu.CompilerParams(
            dimension_semantics=("parallel",), vmem_limit_bytes=VMEM_LIMIT),
        name="in_proj",
    )(x, gain, w)


def _norm_matmul_kernel(x_ref, g_ref, w_ref, o_ref, xn_ref):
    @pl.when(pl.program_id(1) == 0)
    def _():
        xn_ref[...] = _rms(x_ref[...], g_ref[...]).astype(BF16)

    o_ref[...] = _mm(xn_ref[...], w_ref[...]).astype(o_ref.dtype)


def _norm_matmul(x, gain, w, *, layer, tm, tn, name, out_dtype):
    n, d = x.shape
    m = w.shape[2]
    return pl.pallas_call(
        _norm_matmul_kernel,
        grid=(n // tm, m // tn),
        in_specs=[
            pl.BlockSpec((tm, d), lambda i, j: (i, 0)),
            _layer_spec(layer, 1, d),
            pl.BlockSpec((None, d, tn), lambda i, j: (layer, 0, j)),
        ],
        out_specs=pl.BlockSpec((tm, tn), lambda i, j: (i, j)),
        out_shape=jax.ShapeDtypeStruct((n, m), out_dtype),
        scratch_shapes=[pltpu.VMEM((tm, d), BF16)],
        compiler_params=pltpu.CompilerParams(
            dimension_semantics=("parallel", "arbitrary"), vmem_limit_bytes=VMEM_LIMIT),
        name=name,
    )(x, gain, w)


def _mlstm_kernel(u_ref, o_ref, sm_ref, conv_ref, wq_ref, wk_ref, wv_ref, brow_ref, bcol_ref, ng_ref,
                  out_ref, uprev_ref, c_ref, m_ref):
    nb, lb = u_ref.shape[0], u_ref.shape[1]
    e = MLSTM_HEAD_DIM
    nh = MLSTM_HEADS

    @pl.when(pl.program_id(0) == 0)
    def _():
        uprev_ref[...] = jnp.zeros_like(uprev_ref)
        c_ref[...] = jnp.zeros_like(c_ref)
        m_ref[...] = jnp.zeros_like(m_ref)

    u, uc, pre, pre_t, b_cols, b_rows = [], [], [], [], [], []
    lower, upper = _tri(lb, lb), _tri(lb, lb, upper=True)
    for bi in range(nb):
        ub = u_ref[bi].astype(F32)
        conv = None
        for j in range(MLSTM_CONV):
            k = MLSTM_CONV - 1 - j
            term = conv_ref[j:j + 1, :] * (_shift_rows(uprev_ref.at[bi], ub, k) if k else ub)
            conv = term if conv is None else conv + term
        uprev_ref[bi] = ub[lb - uprev_ref.shape[1]:]
        u.append(ub)
        uc.append(conv * _sigmoid(conv))
        sm = sm_ref[bi].astype(F32)
        pre.append(sm + brow_ref[...])
        pre_t.append(sm.T[0:8, :] + bcol_ref[0:8, :])
        b_cols.append(_mm(lower, _log_sigmoid(pre[bi]), pb=3))
        b_rows.append(_mm(_log_sigmoid(pre_t[bi]), upper, pa=3))

    causal = _iota((lb, lb), 0) >= _iota((lb, lb), 1)
    streams = [(bi, h) for bi in range(nb) for h in range(nh)]
    sls = [slice(h * e, (h + 1) * e) for h in range(nh)]
    ones = jnp.ones((lb, e), BF16)
    q = {s: _mm(uc[s[0]][:, sls[s[1]]], wq_ref[s[1]]).astype(BF16) for s in streams}
    k = {s: _mm(uc[s[0]][:, sls[s[1]]], wk_ref[s[1]]) * e ** -0.5 for s in streams}
    v = {s: jnp.concatenate([_mm(u[s[0]][:, sls[s[1]]], wv_ref[s[1]]).astype(BF16), ones], axis=1) for s in streams}
    bc = {(bi, h): b_cols[bi][:, nh + h:nh + h + 1] for bi, h in streams}
    m_st = {(bi, h): m_ref[bi * SUBLANES + h:bi * SUBLANES + h + 1, 0:1] for bi, h in streams}
    dmat = {(bi, h): jnp.where(causal, bc[bi, h] - b_rows[bi][nh + h:nh + h + 1, :] + pre_t[bi][h:h + 1, :], -jnp.inf)
            for bi, h in streams}
    m_inter = {s: bc[s] + m_st[s] for s in streams}
    m_row = {s: jnp.maximum(m_inter[s], jnp.max(dmat[s], axis=-1, keepdims=True)) for s in streams}
    qk = {s: _mm(q[s], k[s], _NT) for s in streams}
    w = {s: jnp.exp(dmat[s] - m_row[s]) * qk[s] for s in streams}
    from_state = {(bi, h): _mm(q[bi, h], c_ref[bi * nh + h]) for bi, h in streams}
    from_chunk = {s: _mm(w[s], v[s]) for s in streams}
    for bi, h in streams:
        s = (bi, h)
        tot = jnp.exp(m_inter[s] - m_row[s]) * from_state[s] + from_chunk[s]
        hh = tot[:, :e] / jnp.maximum(jnp.abs(tot[:, e:e + 1]), jnp.exp(-m_row[s]))
        hn = hh * lax.rsqrt(jnp.mean(hh * hh, axis=-1, keepdims=True) + NORM_EPS) * ng_ref[:, sls[h]]
        out_ref[bi, :, sls[h]] = (hn * _sigmoid(o_ref[bi, :, sls[h]].astype(F32))).astype(out_ref.dtype)
    for bi, h in streams:
        s = (bi, h)
        b_last = bc[s][lb - 1:lb, :]
        e_col = b_last - bc[s] + pre[bi][:, h:h + 1]
        m_new = jnp.maximum(b_last + m_st[s], jnp.max(e_col, axis=0, keepdims=True))
        decay = jnp.exp(b_last + m_st[s] - m_new)
        wk = jnp.exp(e_col - m_new) * k[s]
        c_ref[bi * nh + h] = decay * c_ref[bi * nh + h] + _mm(wk.T, v[s])
        m_ref[bi * SUBLANES + h:bi * SUBLANES + h + 1, :] = jnp.broadcast_to(m_new, (1, m_ref.shape[1]))


def _mlstm(z, layer, conv_w, wq, wk, wv, b_row, b_col, norm_g):
    bsz, t, _ = z.shape
    lb = MLSTM_CHUNK
    hd = MLSTM_HEAD_DIM
    return pl.pallas_call(
        _mlstm_kernel,
        grid=(t // lb,),
        in_specs=[
            pl.BlockSpec((bsz, lb, WIDTH), lambda c: (0, c, ZC_AU // WIDTH)),
            pl.BlockSpec((bsz, lb, WIDTH), lambda c: (0, c, ZC_AO // WIDTH)),
            pl.BlockSpec((bsz, lb, SMALL), lambda c: (0, c, ZC_SM // SMALL)),
            _layer_spec(layer, MLSTM_CONV, WIDTH),
            _layer_spec(layer, MLSTM_HEADS, hd, hd),
            _layer_spec(layer, MLSTM_HEADS, hd, hd),
            _layer_spec(layer, MLSTM_HEADS, hd, hd),
            _layer_spec(layer, 1, SMALL),
            _layer_spec(layer, SMALL, 1),
            _layer_spec(layer, 1, WIDTH),
        ],
        out_specs=pl.BlockSpec((bsz, lb, WIDTH), lambda c: (0, c, 0)),
        out_shape=jax.ShapeDtypeStruct((bsz, t, WIDTH), BF16),
        scratch_shapes=[
            pltpu.VMEM((bsz, SUBLANES, WIDTH), F32),
            pltpu.VMEM((bsz * MLSTM_HEADS, hd, 2 * hd), F32),
            pltpu.VMEM((bsz * SUBLANES, hd), F32),
        ],
        compiler_params=pltpu.CompilerParams(
            dimension_semantics=("arbitrary",), vmem_limit_bytes=VMEM_LIMIT),
        name="mlstm",
    )(z, z, z, conv_w, wq, wk, wv, b_row, b_col, norm_g)


def _s5_kernel(u_ref, pin_ref, pout_ref, win_ref, wout_ref, lre_ref, lim_ref, d_ref, out_ref, x_ref, c_ref, *, bsz):
    steps = u_ref.shape[1]
    rows = steps * SUBLANES
    ns = S5_BLOCK_STATES
    cw = S5_BLOCK_CH
    nblk = S5_HALF // cw

    @pl.when(pl.program_id(0) == 0)
    def _():
        c_ref[...] = jnp.zeros_like(c_ref)

    u_bt = u_ref[...].reshape(bsz * steps, WIDTH)
    u = _mm(pin_ref[0], u_bt[:, :S5_HALF]) + _mm(pin_ref[1], u_bt[:, S5_HALF:])
    upper = (_iota((rows, 1), 0) % SUBLANES) >= bsz
    for j in range(nblk):
        uj = u[:, j * cw:(j + 1) * cw]
        uext = jnp.concatenate([jnp.where(upper, 0.0, uj), jnp.where(upper, uj, 0.0)], axis=1)
        x_ref[:, j * 2 * ns:(j + 1) * 2 * ns] = _mm(uext, win_ref[j])

    lre, lim = lre_ref[...], lim_ref[...]

    ys = []
    for j in range(nblk):
        lr, li = lre[:, j * ns:(j + 1) * ns], lim[:, j * ns:(j + 1) * ns]
        re_cols = slice(j * 2 * ns, j * 2 * ns + ns)
        im_cols = slice(j * 2 * ns + ns, (j + 1) * 2 * ns)
        sr, si = c_ref[:, re_cols], c_ref[:, im_cols]
        for t in range(steps):
            rows_t = slice(t * SUBLANES, (t + 1) * SUBLANES)
            sr, si = (lr * sr - li * si + x_ref[rows_t, re_cols],
                      lr * si + li * sr + x_ref[rows_t, im_cols])
            x_ref[rows_t, re_cols] = sr
            x_ref[rows_t, im_cols] = si
        c_ref[:, re_cols] = sr
        c_ref[:, im_cols] = si
        res = _mm(x_ref[:, j * 2 * ns:(j + 1) * 2 * ns], wout_ref[j])
        cols = slice(j * cw, (j + 1) * cw)
        y = jnp.where(upper, res[:, cw:], res[:, :cw]) + d_ref[:, cols] * u[:, cols]
        ys.append(0.5 * y * (1.0 + jnp.tanh(math.sqrt(2.0 / math.pi) * (y + 0.044715 * y * y * y))))
    y = jnp.concatenate(ys, axis=1).astype(BF16)
    out = jnp.concatenate([_mm(pout_ref[0], y), _mm(pout_ref[1], y)], axis=1)
    out_ref[...] = out.astype(out_ref.dtype).reshape(bsz, steps, WIDTH)


def _s5_permutations(bsz, steps):
    rows = steps * 2 * bsz
    r = np.arange(rows)
    p = np.zeros((2, rows, bsz * steps), np.float32)
    p[(r % (2 * bsz)) // bsz, r, (r % bsz) * steps + r // (2 * bsz)] = 1.0
    return jnp.asarray(p, BF16), jnp.asarray(p.transpose(0, 2, 1), BF16)


def _s5(z, bsz, layer, w_in, w_out, lam_re, lam_im, d_tab):
    t = z.shape[1]
    steps = S5_STEPS
    rows = steps * SUBLANES
    nblk = S5_HALF // S5_BLOCK_CH
    nst = nblk * S5_BLOCK_STATES
    p_in, p_out = _s5_permutations(bsz, steps)
    whole = lambda *shape: pl.BlockSpec(shape, lambda i: (0,) * len(shape))
    return pl.pallas_call(
        functools.partial(_s5_kernel, bsz=bsz),
        grid=(t // steps,),
        in_specs=[
            pl.BlockSpec((bsz, steps, WIDTH), lambda i: (0, i, ZC_BU // WIDTH)),
            whole(2, rows, bsz * steps),
            whole(2, bsz * steps, rows),
            _layer_spec(layer, nblk, 2 * S5_BLOCK_CH, 2 * S5_BLOCK_STATES),
            _layer_spec(layer, nblk, 2 * S5_BLOCK_STATES, 2 * S5_BLOCK_CH),
            _layer_spec(layer, SUBLANES, nst),
            _layer_spec(layer, SUBLANES, nst),
            _layer_spec(layer, rows, S5_HALF),
        ],
        out_specs=pl.BlockSpec((bsz, steps, WIDTH), lambda i: (0, i, 0)),
        out_shape=jax.ShapeDtypeStruct((bsz, t, WIDTH), BF16),
        scratch_shapes=[pltpu.VMEM((rows, 2 * nst), F32), pltpu.VMEM((SUBLANES, 2 * nst), F32)],
        compiler_params=pltpu.CompilerParams(
            dimension_semantics=("arbitrary",), vmem_limit_bytes=VMEM_LIMIT),
        name="s5",
    )(z, p_in, p_out, w_in, w_out, lam_re, lam_im, d_tab)


def _gla_kernel(q_ref, k_ref, v_ref, g_ref, sm_ref, aup_ref, ab_ref, ng_ref, out_ref, st_ref):
    nb, lb = q_ref.shape[0], q_ref.shape[1]
    ch = GLA_CHUNK
    kw = GLA_KEY_WIDTH

    @pl.when(pl.program_id(0) == 0)
    def _():
        st_ref[...] = jnp.zeros_like(st_ref)

    heads = range(GLA_HEADS)
    key_head = _iota((1, kw), 1) // GLA_HEAD_K
    val_head = _iota((1, WIDTH), 1) // GLA_HEAD_V
    key_masks = [jnp.where(key_head == h, 1.0, 0.0).astype(BF16) for h in heads]
    val_masks = [jnp.where(val_head == h, 1.0, 0.0).astype(BF16) for h in heads]

    def stack(x, masks):
        return jnp.concatenate([x * m for m in masks], axis=0)

    causal = _iota((ch, GLA_HEADS * ch), 0) >= _iota((ch, GLA_HEADS * ch), 1) % ch
    own_head = (_iota((WIDTH, kw), 0) // GLA_HEAD_V) == (_iota((WIDTH, kw), 1) // GLA_HEAD_K)
    lower = _tri(lb, ch)
    mid = ch // 2 - 1
    rss = [slice(c * ch, (c + 1) * ch) for c in range(lb // ch)]
    units = [(bi, c) for bi in range(nb) for c in range(lb // ch)]

    b_all = []
    for bi in range(nb):
        x = _mm(sm_ref[bi], aup_ref[...], pb=2) + ab_ref[...]
        b_all.append(_mm(lower, _log_sigmoid(x) * (1.0 / GLA_GATE_TAU), pb=3))
    q_in, q_at, k_at, k_st, dec = {}, {}, {}, {}, {}
    for bi, c in units:
        b = b_all[bi][rss[c]]
        b_mid, b_last = b[mid:mid + 1, :], b[ch - 1:ch, :]
        q = q_ref[bi, rss[c], :].astype(F32) * GLA_HEAD_K ** -0.5
        k = k_ref[bi, rss[c], :].astype(F32)
        q_in[bi, c] = (q * jnp.exp(b)).astype(BF16)
        q_at[bi, c] = (q * jnp.exp(b - b_mid)).astype(BF16)
        k_at[bi, c] = (k * jnp.exp(b_mid - b)).astype(BF16)
        k_st[bi, c] = (k * jnp.exp(b_last - b)).astype(BF16)
        dec[bi, c] = jnp.exp(b_last)
    v = {(bi, c): v_ref[bi, rss[c], :] for bi, c in units}
    attn = {u: jnp.where(causal, _mm(q_at[u], stack(k_at[u], key_masks), _NT), 0.0) for u in units}
    intra = {u: _mm(attn[u], stack(v[u], val_masks)) for u in units}
    grow = {u: jnp.where(own_head, _mm(v[u].astype(F32).T, k_st[u]), 0.0) for u in units}
    state = {}
    for bi in range(nb):
        st = st_ref[bi]
        for c in range(lb // ch):
            state[bi, c] = st
            st = dec[bi, c] * st + grow[bi, c]
        st_ref[bi] = st
    for bi, c in units:
        o = _mm(q_in[bi, c], state[bi, c], _NT) + intra[bi, c]
        for h in heads:
            vs = slice(h * GLA_HEAD_V, (h + 1) * GLA_HEAD_V)
            oh = o[:, vs]
            on = oh * lax.rsqrt(jnp.mean(oh * oh, axis=-1, keepdims=True) + NORM_EPS) * ng_ref[:, vs]
            g = g_ref[bi, rss[c], vs].astype(F32)
            out_ref[bi, rss[c], vs] = (on * (g * _sigmoid(g))).astype(out_ref.dtype)


def _gla(z, layer, a_up, a_bias, norm_g):
    bsz, t, _ = z.shape
    lb = TIME_BLOCK
    kw = GLA_KEY_WIDTH
    return pl.pallas_call(
        _gla_kernel,
        grid=(t // lb,),
        in_specs=[
            pl.BlockSpec((bsz, lb, kw), lambda c: (0, c, ZC_CQ // kw)),
            pl.BlockSpec((bsz, lb, kw), lambda c: (0, c, ZC_CK // kw)),
            pl.BlockSpec((bsz, lb, WIDTH), lambda c: (0, c, ZC_CV // WIDTH)),
            pl.BlockSpec((bsz, lb, WIDTH), lambda c: (0, c, ZC_CG // WIDTH)),
            pl.BlockSpec((bsz, lb, SMALL), lambda c: (0, c, ZC_SM // SMALL)),
            _layer_spec(layer, SMALL, kw),
            _layer_spec(layer, 1, kw),
            _layer_spec(layer, 1, WIDTH),
        ],
        out_specs=pl.BlockSpec((bsz, lb, WIDTH), lambda c: (0, c, 0)),
        out_shape=jax.ShapeDtypeStruct((bsz, t, WIDTH), BF16),
        scratch_shapes=[pltpu.VMEM((bsz, WIDTH, kw), F32)],
        compiler_params=pltpu.CompilerParams(
            dimension_semantics=("arbitrary",), vmem_limit_bytes=VMEM_LIMIT),
        name="gla",
    )(z, z, z, z, z, a_up, a_bias, norm_g)


def _rwkv_kernel(r_ref, k_ref, v_ref, t_ref, mu_ref, mut_ref,
                 w0_ref, wup_ref, a0_ref, aup_ref, gup_ref, kk_ref, ka_ref, rk_ref, ng_ref,
                 out_ref, pr_ref, pk_ref, pv_ref, pt_ref, st_ref, y_ref):
    nb, ch = r_ref.shape[0], r_ref.shape[1]
    lb = nb * ch
    gl = RWKV_GROUP_LANES
    gh = RWKV_GROUP_HEADS

    @pl.when(pl.program_id(0) == 0)
    def _():
        for ref in (pr_ref, pk_ref, pv_ref, pt_ref, st_ref):
            ref[...] = jnp.zeros_like(ref)

    def shift_mix(x_ref, prev_ref, mu):
        mixed = []
        for bi in range(nb):
            x = x_ref[bi].astype(F32)
            xs = _shift_rows(prev_ref.at[bi], x, 1)
            prev_ref[bi] = x[ch - prev_ref.shape[1]:]
            mixed.append(x + mu * (xs - x))
        return jnp.concatenate(mixed, axis=0)

    r = shift_mix(r_ref, pr_ref, mu_ref[0:1, :])
    k = shift_mix(k_ref, pk_ref, mu_ref[1:2, :])
    v = shift_mix(v_ref, pv_ref, mu_ref[2:3, :])
    tl = shift_mix(t_ref, pt_ref, mut_ref[...])

    ones_bd = _tri(gl, RWKV_HEAD) + _tri(gl, RWKV_HEAD, upper=True, strict=True)

    def head_sum(x):
        return jnp.concatenate([_mm(x[:, i * gl:(i + 1) * gl], ones_bd, pa=2) for i in range(WIDTH // gl)], axis=1)

    w_log = _log_sigmoid(w0_ref[...] + _mm(jnp.tanh(tl), wup_ref[...])) - 0.5
    log_w = -jnp.exp(w_log)
    a = _sigmoid(a0_ref[...] + _mm(tl, aup_ref[...]))
    g = _mm(_sigmoid(tl), gup_ref[...])
    kk = k * kk_ref[...]
    kk = kk / jnp.maximum(jnp.sqrt(head_sum(kk * kk)), 1e-12)
    k_rep = k * (1.0 + (a - 1.0) * ka_ref[...])
    av = -kk
    bv = kk * a
    bonus = head_sum(r * k_rep * rk_ref[...]) * v
    gc_all = _mm(_tri(lb, ch), log_w, pb=3)

    lane_head = _iota((1, gl), 1) // RWKV_HEAD
    masks = [jnp.where(lane_head == h, 1.0, 0.0).astype(BF16) for h in range(gh)]

    def stack(x):
        xb = x.astype(BF16)
        return jnp.concatenate([xb * m for m in masks], axis=0)

    def stack_t(x):
        return jnp.concatenate([x * m.astype(F32) for m in masks], axis=0).T

    s_lane = _iota((ch, gl), 1) % ch
    t_row = _iota((ch, gl), 0)
    strict = t_row > s_lane
    incl = t_row >= s_lane
    eye = jnp.where(t_row == s_lane, 1.0, 0.0)

    pairs = [(gi, bi) for gi in range(WIDTH // gl) for bi in range(nb)]
    pre = {}
    for gi, bi in pairs:
        ls = slice(gi * gl, (gi + 1) * gl)
        rs = slice(bi * ch, (bi + 1) * ch)
        gc = gc_all[rs, ls]
        g_last = gc[ch - 1:ch, :]
        e_inv = jnp.exp(-gc)
        e_end = jnp.exp(g_last - gc)
        r_t = (r[rs, ls] * jnp.exp(gc)).astype(BF16)
        a_t = (av[rs, ls] * jnp.exp(gc - log_w[rs, ls])).astype(BF16)
        scores = _mm(jnp.concatenate([a_t, r_t], axis=0),
                     jnp.concatenate([stack(bv[rs, ls] * e_inv), stack(k_rep[rs, ls] * e_inv)], axis=0), _NT)
        pre[gi, bi] = dict(
            r_t=r_t, a_bd=stack(a_t), v_bd=stack(v[rs, ls]), dec=jnp.exp(g_last),
            b_e=stack(bv[rs, ls] * e_end), k_e=stack(k_rep[rs, ls] * e_end),
            a_ab=jnp.where(strict, scores[:ch, :gl], 0.0),
            a_kv=jnp.concatenate([jnp.where(strict, scores[:ch, gl:], 0.0),
                                  jnp.where(incl, scores[ch:, gl:], 0.0)], axis=0).astype(BF16),
            a_rb=jnp.where(incl, scores[ch:, :gl], 0.0).astype(BF16))
    tinv = {key: eye + pre[key]["a_ab"] for key in pairs}
    power = {key: pre[key]["a_ab"].astype(BF16) for key in pairs}
    for key in pairs:
        power[key] = _mm(power[key], stack(power[key])).astype(BF16)
    step = 2
    while step < ch:
        for key in pairs:
            if 2 * step < ch:
                both = _mm(jnp.concatenate([power[key], tinv[key].astype(BF16)], axis=0), stack(power[key]))
                power[key] = both[:ch].astype(BF16)
                tinv[key] = tinv[key] + both[ch:]
            else:
                tinv[key] = tinv[key] + _mm(tinv[key], stack(power[key]))
        step *= 2

    from_v = {key: _mm(pre[key]["a_kv"], pre[key]["v_bd"]) for key in pairs}
    t_bf = {key: tinv[key].astype(BF16) for key in pairs}
    a_hat = {key: _mm(t_bf[key], pre[key]["a_bd"]).astype(BF16) for key in pairs}
    u0 = {key: _mm(t_bf[key], stack(from_v[key][:ch])) for key in pairs}
    grow = {key: _mm(pre[key]["v_bd"].astype(F32).T, pre[key]["k_e"]) for key in pairs}
    n_groups = WIDTH // gl
    state = {(gi, bi): st_ref[bi * n_groups + gi] for gi, bi in pairs}
    from_state = {key: _mm(jnp.concatenate([a_hat[key], pre[key]["r_t"]], axis=0), state[key], _NT) for key in pairs}
    u = {key: from_state[key][:ch] + u0[key] for key in pairs}
    for gi, bi in pairs:
        key = (gi, bi)
        y_ref[bi * ch:(bi + 1) * ch, gi * gl:(gi + 1) * gl] = (
            from_state[key][ch:] + _mm(pre[key]["a_rb"], stack(u[key])) + from_v[key][ch:])
    for gi, bi in pairs:
        key = (gi, bi)
        st_ref[bi * n_groups + gi] = (pre[key]["dec"] * state[key] + _mm(stack_t(u[key]), pre[key]["b_e"])
                                      + grow[key])

    y = y_ref[...]
    inv_n = 1.0 / RWKV_HEAD
    mu = head_sum(y) * inv_n
    yc = y - mu
    var = head_sum(yc * yc) * inv_n
    yn = yc * lax.rsqrt(var + RWKV_GN_EPS) * ng_ref[...]
    out_ref[...] = ((yn + bonus) * g).astype(out_ref.dtype).reshape(nb, ch, WIDTH)


def _rwkv(z, layer, mu_rkv, mu_t, w0, w_up, a0, a_up, g_up, k_k, k_a, r_k, norm_g):
    bsz, t, _ = z.shape
    ch = RWKV_CHUNK
    gl = RWKV_GROUP_LANES
    vec = _layer_spec(layer, 1, WIDTH)
    mat = _layer_spec(layer, RWKV_TAIL, WIDTH)
    return pl.pallas_call(
        _rwkv_kernel,
        grid=(t // ch,),
        in_specs=[
            pl.BlockSpec((bsz, ch, WIDTH), lambda c: (0, c, ZC_DR // WIDTH)),
            pl.BlockSpec((bsz, ch, WIDTH), lambda c: (0, c, ZC_DK // WIDTH)),
            pl.BlockSpec((bsz, ch, WIDTH), lambda c: (0, c, ZC_DV // WIDTH)),
            pl.BlockSpec((bsz, ch, RWKV_TAIL), lambda c: (0, c, ZC_DT // RWKV_TAIL)),
            _layer_spec(layer, 3, WIDTH), _layer_spec(layer, 1, RWKV_TAIL),
            vec, mat, vec, mat, mat, vec, vec, vec, vec,
        ],
        out_specs=pl.BlockSpec((bsz, ch, WIDTH), lambda c: (0, c, 0)),
        out_shape=jax.ShapeDtypeStruct((bsz, t, WIDTH), BF16),
        scratch_shapes=[
            pltpu.VMEM((bsz, SUBLANES, WIDTH), F32), pltpu.VMEM((bsz, SUBLANES, WIDTH), F32),
            pltpu.VMEM((bsz, SUBLANES, WIDTH), F32), pltpu.VMEM((bsz, SUBLANES, RWKV_TAIL), F32),
            pltpu.VMEM((bsz * (WIDTH // gl), gl, gl), F32),
            pltpu.VMEM((bsz * ch, WIDTH), F32),
        ],
        compiler_params=pltpu.CompilerParams(
            dimension_semantics=("arbitrary",), vmem_limit_bytes=VMEM_LIMIT),
        name="rwkv",
    )(z, z, z, z, mu_rkv, mu_t, w0, w_up, a0, a_up, g_up, k_k, k_a, r_k, norm_g)


def _merge_kernel(x_ref, ya_ref, zb_ref, yc_ref, yd_ref, g0_ref, g1_ref, g2_ref, g3_ref, gb_ref,
                  pa_ref, w1_ref, w2_ref, pc_ref, pd_ref, wo_ref, out_ref):
    d = D_MODEL

    def gate(g_ref, i):
        return _sigmoid(g_ref[...].astype(F32) + gb_ref[:, i * d:(i + 1) * d])

    zb = zb_ref[...]
    merged = gate(g0_ref, 0) * _mm(ya_ref[...], pa_ref[...])
    merged = merged + gate(g1_ref, 1) * (_mm(zb, w1_ref[...]) * _sigmoid(_mm(zb, w2_ref[...])))
    merged = merged + gate(g2_ref, 2) * _mm(yc_ref[...], pc_ref[...])
    merged = merged + gate(g3_ref, 3) * _mm(yd_ref[...], pd_ref[...])
    out_ref[...] = x_ref[...] + _mm(merged, wo_ref[...])


def _merge(x, z, ya, zb, yc, yd, layer, gate_bias, pa, w1, w2, pc, pd, wo, *, tm=512):
    n, d = x.shape
    tile = lambda w: pl.BlockSpec((tm, w), lambda i: (i, 0))
    gate = lambda b: pl.BlockSpec((tm, d), lambda i: (i, ZC_GATE // d + b))
    resident = lambda rows: pl.BlockSpec((None, rows, d), lambda i: (layer, 0, 0), pipeline_mode=pl.Buffered(1))
    proj = resident(WIDTH)
    return pl.pallas_call(
        _merge_kernel,
        grid=(n // tm,),
        in_specs=[tile(d), tile(WIDTH), tile(WIDTH), tile(WIDTH), tile(WIDTH),
                  gate(0), gate(1), gate(2), gate(3), _layer_spec(layer, 1, N_BRANCH * d),
                  proj, proj, proj, proj, proj, resident(d)],
        out_specs=tile(d),
        out_shape=jax.ShapeDtypeStruct((n, d), F32),
        compiler_params=pltpu.CompilerParams(
            dimension_semantics=("parallel",), vmem_limit_bytes=VMEM_LIMIT),
        name="merge",
    )(x, ya, zb, yc, yd, z, z, z, z, gate_bias, pa, w1, w2, pc, pd, wo)


def _xattn_kernel(x_ref, g_ref, wq_ref, kv_ref, wo_ref, out_ref):
    d = D_MODEL
    hd = XATTN_HEAD_DIM
    x = x_ref[...]
    q = _mm(_rms(x, g_ref[...]), wq_ref[...])
    kv = kv_ref[...]
    outs = []
    for h in range(XATTN_HEADS):
        sl = slice(h * hd, (h + 1) * hd)
        s = _mm(q[:, sl], kv[:, sl], _NT) * hd ** -0.5
        s = s - jnp.max(s, axis=-1, keepdims=True)
        p = jnp.exp(s)
        p = p / jnp.sum(p, axis=-1, keepdims=True)
        outs.append(_mm(p, kv[:, d + h * hd:d + (h + 1) * hd]))
    o = jnp.concatenate(outs, axis=1)
    out_ref[...] = x + _mm(o, wo_ref[...])


def _xattn(x, bsz, layer, gain, wq, kv, wo, *, tm=512):
    n, d = x.shape
    nblk = n // bsz // tm
    m_len = kv.shape[0] // bsz
    return pl.pallas_call(
        _xattn_kernel,
        grid=(bsz, nblk),
        in_specs=[
            pl.BlockSpec((tm, d), lambda b, c: (b * nblk + c, 0)),
            _layer_spec(layer, 1, d),
            _layer_spec(layer, d, d),
            pl.BlockSpec((m_len, 2 * d), lambda b, c: (b, 0)),
            _layer_spec(layer, d, d),
        ],
        out_specs=pl.BlockSpec((tm, d), lambda b, c: (b * nblk + c, 0)),
        out_shape=jax.ShapeDtypeStruct((n, d), F32),
        compiler_params=pltpu.CompilerParams(
            dimension_semantics=("parallel", "parallel"), vmem_limit_bytes=VMEM_LIMIT),
        name="xattn",
    )(x, gain, wq, kv, wo)


def _w_in_segments():
    o = np.cumsum((0, WIDTH, WIDTH, MLSTM_HEADS, MLSTM_HEADS, WIDTH, GLA_KEY_WIDTH, GLA_KEY_WIDTH, WIDTH, WIDTH,
                   GLA_GATE_RANK, 3 * WIDTH + RWKV_TAIL, N_BRANCH * D_MODEL))
    a_u, a_o, a_i, a_f, b_u, c_q, c_k, c_v, c_g, c_a, d_z, gate = range(12)
    seg = lambda i: (int(o[i]), int(o[i + 1] - o[i]))
    d_parts = [(int(o[d_z]) + i * WIDTH, WIDTH) for i in range(3)] + [(int(o[d_z]) + 3 * WIDTH, RWKV_TAIL)]
    return ([seg(gate), seg(a_u), seg(a_o), seg(b_u), seg(c_v), seg(c_g)] + d_parts[:3]
            + [seg(c_q), seg(c_k), d_parts[3], seg(a_i), seg(a_f), seg(c_a)])


def _regroup_kernel(w_ref, o_ref):
    dst = 0
    for src, width in _w_in_segments():
        o_ref[:, dst:dst + width] = w_ref[:, src:src + width].astype(o_ref.dtype)
        dst += width
    o_ref[:, dst:] = jnp.zeros((o_ref.shape[0], o_ref.shape[1] - dst), o_ref.dtype)


def _regroup_w_in(w, *, rows=256):
    depth, d, n_in = w.shape
    return pl.pallas_call(
        _regroup_kernel,
        grid=(depth, d // rows),
        in_specs=[pl.BlockSpec((None, rows, n_in), lambda l, i: (l, i, 0))],
        out_specs=pl.BlockSpec((None, rows, Z_COLS), lambda l, i: (l, i, 0)),
        out_shape=jax.ShapeDtypeStruct((depth, d, Z_COLS), BF16),
        compiler_params=pltpu.CompilerParams(
            dimension_semantics=("parallel", "parallel"), vmem_limit_bytes=VMEM_LIMIT),
        name="regroup_w_in",
    )(w)


def _s5_tables(a_re, a_im, log_step, b_re, b_im, c_re, c_im, d_skip, bsz):
    depth = a_re.shape[0]
    step = jnp.exp(log_step)[..., None]
    lam_re = jnp.minimum(a_re, -S5_MIN_NEG)
    lam_im = a_im
    mag = jnp.exp(lam_re * step)
    bar_re = mag * jnp.cos(lam_im * step)
    bar_im = mag * jnp.sin(lam_im * step)
    denom = lam_re * lam_re + lam_im * lam_im
    coef_re = ((bar_re - 1.0) * lam_re + bar_im * lam_im) / denom
    coef_im = (bar_im * lam_re - (bar_re - 1.0) * lam_im) / denom
    bb_re = coef_re[..., None] * b_re - coef_im[..., None] * b_im
    bb_im = coef_re[..., None] * b_im + coef_im[..., None] * b_re
    nb = S5_HALF // S5_BLOCK_CH
    gpb = S5_BLOCK_GROUPS
    eye = jnp.eye(gpb, dtype=F32)

    def in_block(bb):
        bb = bb.reshape(depth, 2, nb, gpb, S5_STATE, S5_GROUP)
        return jnp.einsum('lhjgpc,gk->lhjgckp', bb, eye).reshape(depth, 2, nb, S5_BLOCK_CH, S5_BLOCK_STATES)

    def out_block(cc):
        cc = cc.reshape(depth, 2, nb, gpb, S5_GROUP, S5_STATE)
        return jnp.einsum('lhjgcp,gk->lhjgpkc', cc, eye).reshape(depth, 2, nb, S5_BLOCK_STATES, S5_BLOCK_CH)

    w_in = jnp.concatenate([in_block(bb_re), in_block(bb_im)], axis=-1)
    w_in = w_in.transpose(0, 2, 1, 3, 4).reshape(depth, nb, 2 * S5_BLOCK_CH, 2 * S5_BLOCK_STATES)
    w_out = jnp.concatenate([out_block(c_re), -out_block(c_im)], axis=-2)
    w_out = w_out.transpose(0, 2, 3, 1, 4).reshape(depth, nb, 2 * S5_BLOCK_STATES, 2 * S5_BLOCK_CH)

    def rows(t, width):
        t = t.reshape(depth, 2, 1, width)
        return jnp.broadcast_to(t, (depth, 2, bsz, width)).reshape(depth, 2 * bsz, width)

    nst = nb * S5_BLOCK_STATES
    lam_re_rows = rows(bar_re.reshape(depth, 2 * nst), nst)
    lam_im_rows = rows(bar_im.reshape(depth, 2 * nst), nst)
    d_tab = jnp.tile(rows(d_skip, S5_HALF), (1, S5_STEPS, 1))
    return w_in.astype(BF16), w_out.astype(BF16), lam_re_rows, lam_im_rows, d_tab


def _pad_rows(w, start, total):
    return jnp.pad(w, ((0, 0), (start, total - start - w.shape[1]), (0, 0)))


def kernel(x, mem, ffn1_norm, ffn1_w_gate, ffn1_w_up, ffn1_w_down, mix_norm, w_in, gate_bias, mlstm_conv, mlstm_wq, mlstm_wk, mlstm_wv, mlstm_b_i, mlstm_b_f, mlstm_norm, mlstm_proj, s5_a_re, s5_a_im, s5_log_step, s5_b_re, s5_b_im, s5_c_re, s5_c_im, s5_d, s5_glu_w1, s5_glu_w2, gla_a_up, gla_a_bias, gla_norm, gla_proj, rwkv_mu, rwkv_w0, rwkv_w_up, rwkv_a0, rwkv_a_up, rwkv_g_up, rwkv_k_k, rwkv_k_a, rwkv_r_k, rwkv_norm, rwkv_proj, w_out, xattn_norm, mem_norm, xattn_wq, xattn_wk, xattn_wv, xattn_wo, ffn2_norm, ffn2_w_gate, ffn2_w_up, ffn2_w_down, final_norm):
    bsz, t, d = x.shape
    depth = w_in.shape[0]
    assert d == D_MODEL and t % TIME_BLOCK == 0 and 2 * bsz == SUBLANES
    xs = x.reshape(bsz * t, d)
    mems = mem.reshape(bsz * mem.shape[1], d)
    bf = lambda w: w.astype(BF16)
    vec = lambda v: v.reshape(depth, 1, -1)
    fin = final_norm.reshape(1, d)

    w_in_r = _regroup_w_in(w_in)
    gate_b = jnp.pad(jnp.concatenate([mlstm_b_i, mlstm_b_f], axis=1), ((0, 0), (0, SMALL - 2 * MLSTM_HEADS)))
    s5_w_in, s5_w_out, s5_lre, s5_lim, s5_dtab = _s5_tables(
        s5_a_re, s5_a_im, s5_log_step, s5_b_re, s5_b_im, s5_c_re, s5_c_im, s5_d, bsz)
    gla_up = _pad_rows(gla_a_up, 2 * MLSTM_HEADS, SMALL)
    mu_rkv = rwkv_mu[:, :3 * WIDTH].reshape(depth, 3, WIDTH)
    mu_t = vec(rwkv_mu[:, 3 * WIDTH:])
    rw_wup = bf(_pad_rows(rwkv_w_up, 0, RWKV_TAIL))
    rw_aup = bf(_pad_rows(rwkv_a_up, RWKV_DECAY_RANK, RWKV_TAIL))
    rw_gup = bf(_pad_rows(rwkv_g_up, RWKV_DECAY_RANK + RWKV_ICLR_RANK, RWKV_TAIL))
    w_kv = bf(jnp.concatenate([xattn_wk, xattn_wv], axis=2))
    wq_a, wk_a, wv_a = bf(mlstm_wq), bf(mlstm_wk), bf(mlstm_wv)
    p_a, p_b1, p_b2, p_c, p_d, p_o = (bf(mlstm_proj), bf(s5_glu_w1), bf(s5_glu_w2), bf(gla_proj), bf(rwkv_proj),
                                      bf(w_out))
    x_wq, x_wo = bf(xattn_wq), bf(xattn_wo)
    f1_g, f1_u, f1_d = bf(ffn1_w_gate), bf(ffn1_w_up), bf(ffn1_w_down)
    f2_g, f2_u, f2_d = bf(ffn2_w_gate), bf(ffn2_w_up), bf(ffn2_w_down)

    for l in range(depth):
        xs = _ffn(xs, vec(ffn1_norm), f1_g, f1_u, f1_d, fin, layer=l, final=False)

        z = _in_proj(xs, vec(mix_norm), w_in_r, layer=l)

        z3 = z.reshape(bsz, t, Z_COLS)
        y_a = _mlstm(z3, l, mlstm_conv, wq_a, wk_a, wv_a, vec(gate_b), gate_b.reshape(depth, SMALL, 1),
                     vec(mlstm_norm)).reshape(bsz * t, WIDTH)

        z_b = _s5(z3, bsz, l, s5_w_in, s5_w_out, s5_lre, s5_lim, s5_dtab).reshape(bsz * t, WIDTH)

        y_c = _gla(z3, l, gla_up, vec(gla_a_bias), vec(gla_norm)).reshape(bsz * t, WIDTH)

        y_d = _rwkv(z3, l, mu_rkv, mu_t, vec(rwkv_w0), rw_wup, vec(rwkv_a0), rw_aup, rw_gup,
                    vec(rwkv_k_k), vec(rwkv_k_a), vec(rwkv_r_k), vec(rwkv_norm)).reshape(bsz * t, WIDTH)

        xs = _merge(xs, z, y_a, z_b, y_c, y_d, l, vec(gate_bias), p_a, p_b1, p_b2, p_c, p_d, p_o)

        kv = _norm_matmul(mems, vec(mem_norm), w_kv, layer=l, tm=mems.shape[0], tn=512, name="mem_kv",
                          out_dtype=BF16)
        xs = _xattn(xs, bsz, l, vec(xattn_norm), x_wq, kv, x_wo)

        xs = _ffn(xs, vec(ffn2_norm), f2_g, f2_u, f2_d, fin, layer=l,
                  final=(l == depth - 1))
    return xs.reshape(bsz, t, d)
```

```python
import functools
import math

import jax
import jax.numpy as jnp
import numpy as np
from jax import lax
from jax.experimental import pallas as pl
from jax.experimental.pallas import tpu as pltpu

F32 = jnp.float32
BF16 = jnp.bfloat16

D_MODEL = 1024
D_FF = 11 * D_MODEL // 4
NORM_EPS = 1e-6
FFN_HALF = 0.5
N_BRANCH = 4
WIDTH = D_MODEL // 2

MLSTM_HEADS = 4
MLSTM_HEAD_DIM = WIDTH // MLSTM_HEADS
MLSTM_CONV = 4
MLSTM_CHUNK = 256

S5_GROUP = 16
S5_GROUPS = WIDTH // S5_GROUP
S5_STATE = 64
S5_MIN_NEG = 1e-4
S5_HALF = WIDTH // 2
S5_BLOCK_GROUPS = 8
S5_BLOCK_CH = S5_BLOCK_GROUPS * S5_GROUP
S5_BLOCK_STATES = S5_BLOCK_GROUPS * S5_STATE
S5_STEPS = 64
SUBLANES = 8

GLA_HEADS = 4
GLA_KEY_WIDTH = WIDTH // 2
GLA_HEAD_K = GLA_KEY_WIDTH // GLA_HEADS
GLA_HEAD_V = WIDTH // GLA_HEADS
GLA_GATE_RANK = 16
GLA_GATE_TAU = 16.0
GLA_CHUNK = 64

RWKV_HEAD = 64
RWKV_HEADS = WIDTH // RWKV_HEAD
RWKV_DECAY_RANK = 64
RWKV_ICLR_RANK = 64
RWKV_GATE_RANK = 128
RWKV_TAIL = RWKV_DECAY_RANK + RWKV_ICLR_RANK + RWKV_GATE_RANK
RWKV_GN_EPS = 64e-5
RWKV_CHUNK = 64
RWKV_GROUP_LANES = 256
RWKV_GROUP_HEADS = RWKV_GROUP_LANES // RWKV_HEAD

XATTN_HEADS = 4
XATTN_HEAD_DIM = D_MODEL // XATTN_HEADS

ZC_GATE = 0
ZC_AU = 4096
ZC_AO = 4608
ZC_BU = 5120
ZC_CV = 5632
ZC_CG = 6144
ZC_DR = 6656
ZC_DK = 7168
ZC_DV = 7680
ZC_CQ = 8192
ZC_CK = 8448
ZC_DT = 8704
ZC_SM = 8960
Z_COLS = 9216
SMALL = 128

TIME_BLOCK = 256
VMEM_LIMIT = 56 * 1024 * 1024

_NN = (((1,), (0,)), ((), ()))
_NT = (((1,), (1,)), ((), ()))


def _split(x, n):
    parts = []
    rest = x
    for i in range(n):
        p = rest.astype(BF16)
        parts.append(p)
        if i + 1 < n:
            rest = rest - p.astype(F32)
    return parts


def _mm(a, b, dn=_NN, pa=1, pb=1):
    ap, bp = _split(a, pa), _split(b, pb)
    order = max(pa, pb)
    out = None
    for i, x in enumerate(ap):
        for j, y in enumerate(bp):
            if i + j < order:
                t = lax.dot_general(x, y, dn, preferred_element_type=F32)
                out = t if out is None else out + t
    return out


def _rms(x, g):
    return x * lax.rsqrt(jnp.mean(x * x, axis=-1, keepdims=True) + NORM_EPS) * g


def _sigmoid(x):
    return 0.5 * jnp.tanh(0.5 * x) + 0.5


def _log_sigmoid(x):
    return jnp.minimum(x, 0.0) - jnp.log(1.0 + jnp.exp(-jnp.abs(x)))


def _iota(shape, dim):
    return lax.broadcasted_iota(jnp.int32, shape, dim)


def _tri(n, block, upper=False, strict=False):
    r, c = _iota((n, n), 0), _iota((n, n), 1)
    same = (r // block) == (c // block)
    if upper:
        keep = (r < c) if strict else (r <= c)
    else:
        keep = (r > c) if strict else (r >= c)
    return jnp.where(same & keep, 1.0, 0.0).astype(BF16)


def _shift_rows(prev_ref, x, k):
    ext = jnp.concatenate([prev_ref[...], x], axis=0)
    return pltpu.roll(ext, k, 0)[prev_ref.shape[0]:]


def _layer_spec(layer, *shape):
    return pl.BlockSpec((None,) + shape, lambda *_: (layer,) + (0,) * len(shape))


def _ffn_kernel(x_ref, g_ref, wg_ref, wu_ref, wd_ref, fg_ref, o_ref, *, final, tf):
    x = x_ref[...]
    xn = _rms(x, g_ref[...]).astype(BF16)
    acc = None
    for j in range(wg_ref.shape[1] // tf):
        cols = slice(j * tf, (j + 1) * tf)
        gate = _mm(xn, wg_ref[:, cols])
        up = _mm(xn, wu_ref[:, cols])
        part = _mm(gate * _sigmoid(gate) * up, wd_ref[cols, :])
        acc = part if acc is None else acc + part
    y = x + FFN_HALF * acc
    if final:
        y = _rms(y, fg_ref[...])
    o_ref[...] = y


def _ffn(x, gain, wg, wu, wd, final_gain, *, layer, final, tm=512, tf=256):
    n, d = x.shape
    f = wg.shape[2]
    resident = lambda *shape: pl.BlockSpec((None,) + shape, lambda i: (layer, 0, 0), pipeline_mode=pl.Buffered(1))
    return pl.pallas_call(
        functools.partial(_ffn_kernel, final=final, tf=tf),
        grid=(n // tm,),
        in_specs=[
            pl.BlockSpec((tm, d), lambda i: (i, 0)),
            _layer_spec(layer, 1, d),
            resident(d, f), resident(d, f), resident(f, d),
            pl.BlockSpec((1, d), lambda i: (0, 0)),
        ],
        out_specs=pl.BlockSpec((tm, d), lambda i: (i, 0)),
        out_shape=jax.ShapeDtypeStruct((n, d), F32),
        compiler_params=pltpu.CompilerParams(
            dimension_semantics=("parallel",), vmem_limit_bytes=VMEM_LIMIT),
        name="ffn",
    )(x, gain, wg, wu, wd, final_gain)


def _in_proj_kernel(x_ref, g_ref, w_ref, o_ref, *, tn):
    xn = _rms(x_ref[...], g_ref[...]).astype(BF16)
    for j in range(w_ref.shape[0] // tn):
        cols = slice(j * tn, (j + 1) * tn)
        o_ref[:, cols] = _mm(xn, w_ref[cols, :], _NT).astype(o_ref.dtype)


def _in_proj(x, gain, w, *, layer, tm=512, tn=1536):
    n, d = x.shape
    m = w.shape[1]
    return pl.pallas_call(
        functools.partial(_in_proj_kernel, tn=tn),
        grid=(n // tm,),
        in_specs=[
            pl.BlockSpec((tm, d), lambda i: (i, 0)),
            _layer_spec(layer, 1, d),
            pl.BlockSpec((None, m, d), lambda i: (layer, 0, 0), pipeline_mode=pl.Buffered(1)),
        ],
        out_specs=pl.BlockSpec((tm, m), lambda i: (i, 0)),
        out_shape=jax.ShapeDtypeStruct((n, m), BF16),
        compiler_params=pltpu.CompilerParams(
            dimension_semantics=("parallel",), vmem_limit_bytes=VMEM_LIMIT),
        name="in_proj",
    )(x, gain, w)


def _norm_matmul_kernel(x_ref, g_ref, w_ref, o_ref, xn_ref):
    @pl.when(pl.program_id(1) == 0)
    def _():
        xn_ref[...] = _rms(x_ref[...], g_ref[...]).astype(BF16)

    o_ref[...] = _mm(xn_ref[...], w_ref[...]).astype(o_ref.dtype)


def _norm_matmul(x, gain, w, *, layer, tm, tn, name, out_dtype):
    n, d = x.shape
    m = w.shape[2]
    return pl.pallas_call(
        _norm_matmul_kernel,
        grid=(n // tm, m // tn),
        in_specs=[
            pl.BlockSpec((tm, d), lambda i, j: (i, 0)),
            _layer_spec(layer, 1, d),
            pl.BlockSpec((None, d, tn), lambda i, j: (layer, 0, j)),
        ],
        out_specs=pl.BlockSpec((tm, tn), lambda i, j: (i, j)),
        out_shape=jax.ShapeDtypeStruct((n, m), out_dtype),
        scratch_shapes=[pltpu.VMEM((tm, d), BF16)],
        compiler_params=pltpu.CompilerParams(
            dimension_semantics=("parallel", "arbitrary"), vmem_limit_bytes=VMEM_LIMIT),
        name=name,
    )(x, gain, w)


def _mlstm_kernel(u_ref, o_ref, sm_ref, conv_ref, wq_ref, wk_ref, wv_ref, brow_ref, bcol_ref, ng_ref,
                  out_ref, uprev_ref, c_ref, m_ref):
    nb, lb = u_ref.shape[0], u_ref.shape[1]
    e = MLSTM_HEAD_DIM
    nh = MLSTM_HEADS

    @pl.when(pl.program_id(0) == 0)
    def _():
        uprev_ref[...] = jnp.zeros_like(uprev_ref)
        c_ref[...] = jnp.zeros_like(c_ref)
        m_ref[...] = jnp.zeros_like(m_ref)

    u, uc, pre, pre_t, b_cols, b_rows = [], [], [], [], [], []
    lower, upper = _tri(lb, lb), _tri(lb, lb, upper=True)
    for bi in range(nb):
        ub = u_ref[bi].astype(F32)
        conv = None
        for j in range(MLSTM_CONV):
            k = MLSTM_CONV - 1 - j
            term = conv_ref[j:j + 1, :] * (_shift_rows(uprev_ref.at[bi], ub, k) if k else ub)
            conv = term if conv is None else conv + term
        uprev_ref[bi] = ub[lb - uprev_ref.shape[1]:]
        u.append(ub)
        uc.append(conv * _sigmoid(conv))
        sm = sm_ref[bi].astype(F32)
        pre.append(sm + brow_ref[...])
        pre_t.append(sm.T[0:8, :] + bcol_ref[0:8, :])
        b_cols.append(_mm(lower, _log_sigmoid(pre[bi]), pb=3))
        b_rows.append(_mm(_log_sigmoid(pre_t[bi]), upper, pa=3))

    causal = _iota((lb, lb), 0) >= _iota((lb, lb), 1)
    streams = [(bi, h) for bi in range(nb) for h in range(nh)]
    sls = [slice(h * e, (h + 1) * e) for h in range(nh)]
    ones = jnp.ones((lb, e), BF16)
    q = {s: _mm(uc[s[0]][:, sls[s[1]]], wq_ref[s[1]]).astype(BF16) for s in streams}
    k = {s: _mm(uc[s[0]][:, sls[s[1]]], wk_ref[s[1]]) * e ** -0.5 for s in streams}
    v = {s: jnp.concatenate([_mm(u[s[0]][:, sls[s[1]]], wv_ref[s[1]]).astype(BF16), ones], axis=1) for s in streams}
    bc = {(bi, h): b_cols[bi][:, nh + h:nh + h + 1] for bi, h in streams}
    m_st = {(bi, h): m_ref[bi * SUBLANES + h:bi * SUBLANES + h + 1, 0:1] for bi, h in streams}
    dmat = {(bi, h): jnp.where(causal, bc[bi, h] - b_rows[bi][nh + h:nh + h + 1, :] + pre_t[bi][h:h + 1, :], -jnp.inf)
            for bi, h in streams}
    m_inter = {s: bc[s] + m_st[s] for s in streams}
    m_row = {s: jnp.maximum(m_inter[s], jnp.max(dmat[s], axis=-1, keepdims=True)) for s in streams}
    qk = {s: _mm(q[s], k[s], _NT) for s in streams}
    w = {s: jnp.exp(dmat[s] - m_row[s]) * qk[s] for s in streams}
    from_state = {(bi, h): _mm(q[bi, h], c_ref[bi * nh + h]) for bi, h in streams}
    from_chunk = {s: _mm(w[s], v[s]) for s in streams}
    for bi, h in streams:
        s = (bi, h)
        tot = jnp.exp(m_inter[s] - m_row[s]) * from_state[s] + from_chunk[s]
        hh = tot[:, :e] / jnp.maximum(jnp.abs(tot[:, e:e + 1]), jnp.exp(-m_row[s]))
        hn = hh * lax.rsqrt(jnp.mean(hh * hh, axis=-1, keepdims=True) + NORM_EPS) * ng_ref[:, sls[h]]
        out_ref[bi, :, sls[h]] = (hn * _sigmoid(o_ref[bi, :, sls[h]].astype(F32))).astype(out_ref.dtype)
    for bi, h in streams:
        s = (bi, h)
        b_last = bc[s][lb - 1:lb, :]
        e_col = b_last - bc[s] + pre[bi][:, h:h + 1]
        m_new = jnp.maximum(b_last + m_st[s], jnp.max(e_col, axis=0, keepdims=True))
        decay = jnp.exp(b_last + m_st[s] - m_new)
        wk = jnp.exp(e_col - m_new) * k[s]
        c_ref[bi * nh + h] = decay * c_ref[bi * nh + h] + _mm(wk.T, v[s])
        m_ref[bi * SUBLANES + h:bi * SUBLANES + h + 1, :] = jnp.broadcast_to(m_new, (1, m_ref.shape[1]))


def _mlstm(z, layer, conv_w, wq, wk, wv, b_row, b_col, norm_g):
    bsz, t, _ = z.shape
    lb = MLSTM_CHUNK
    hd = MLSTM_HEAD_DIM
    return pl.pallas_call(
        _mlstm_kernel,
        grid=(t // lb,),
        in_specs=[
            pl.BlockSpec((bsz, lb, WIDTH), lambda c: (0, c, ZC_AU // WIDTH)),
            pl.BlockSpec((bsz, lb, WIDTH), lambda c: (0, c, ZC_AO // WIDTH)),
            pl.BlockSpec((bsz, lb, SMALL), lambda c: (0, c, ZC_SM // SMALL)),
            _layer_spec(layer, MLSTM_CONV, WIDTH),
            _layer_spec(layer, MLSTM_HEADS, hd, hd),
            _layer_spec(layer, MLSTM_HEADS, hd, hd),
            _layer_spec(layer, MLSTM_HEADS, hd, hd),
            _layer_spec(layer, 1, SMALL),
            _layer_spec(layer, SMALL, 1),
            _layer_spec(layer, 1, WIDTH),
        ],
        out_specs=pl.BlockSpec((bsz, lb, WIDTH), lambda c: (0, c, 0)),
        out_shape=jax.ShapeDtypeStruct((bsz, t, WIDTH), BF16),
        scratch_shapes=[
            pltpu.VMEM((bsz, SUBLANES, WIDTH), F32),
            pltpu.VMEM((bsz * MLSTM_HEADS, hd, 2 * hd), F32),
            pltpu.VMEM((bsz * SUBLANES, hd), F32),
        ],
        compiler_params=pltpu.CompilerParams(
            dimension_semantics=("arbitrary",), vmem_limit_bytes=VMEM_LIMIT),
        name="mlstm",
    )(z, z, z, conv_w, wq, wk, wv, b_row, b_col, norm_g)


def _s5_kernel(u_ref, pin_ref, pout_ref, win_ref, wout_ref, lre_ref, lim_ref, d_ref, out_ref, x_ref, c_ref, *, bsz):
    steps = u_ref.shape[1]
    rows = steps * SUBLANES
    ns = S5_BLOCK_STATES
    cw = S5_BLOCK_CH
    nblk = S5_HALF // cw

    @pl.when(pl.program_id(0) == 0)
    def _():
        c_ref[...] = jnp.zeros_like(c_ref)

    u_bt = u_ref[...].reshape(bsz * steps, WIDTH)
    u = _mm(pin_ref[0], u_bt[:, :S5_HALF]) + _mm(pin_ref[1], u_bt[:, S5_HALF:])
    upper = (_iota((rows, 1), 0) % SUBLANES) >= bsz
    for j in range(nblk):
        uj = u[:, j * cw:(j + 1) * cw]
        uext = jnp.concatenate([jnp.where(upper, 0.0, uj), jnp.where(upper, uj, 0.0)], axis=1)
        x_ref[:, j * 2 * ns:(j + 1) * 2 * ns] = _mm(uext, win_ref[j])

    lre, lim = lre_ref[...], lim_ref[...]

    ys = []
    for j in range(nblk):
        lr, li = lre[:, j * ns:(j + 1) * ns], lim[:, j * ns:(j + 1) * ns]
        re_cols = slice(j * 2 * ns, j * 2 * ns + ns)
        im_cols = slice(j * 2 * ns + ns, (j + 1) * 2 * ns)
        sr, si = c_ref[:, re_cols], c_ref[:, im_cols]
        for t in range(steps):
            rows_t = slice(t * SUBLANES, (t + 1) * SUBLANES)
            sr, si = (lr * sr - li * si + x_ref[rows_t, re_cols],
                      lr * si + li * sr + x_ref[rows_t, im_cols])
            x_ref[rows_t, re_cols] = sr
            x_ref[rows_t, im_cols] = si
        c_ref[:, re_cols] = sr
        c_ref[:, im_cols] = si
        res = _mm(x_ref[:, j * 2 * ns:(j + 1) * 2 * ns], wout_ref[j])
        cols = slice(j * cw, (j + 1) * cw)
        y = jnp.where(upper, res[:, cw:], res[:, :cw]) + d_ref[:, cols] * u[:, cols]
        ys.append(0.5 * y * (1.0 + jnp.tanh(math.sqrt(2.0 / math.pi) * (y + 0.044715 * y * y * y))))
    y = jnp.concatenate(ys, axis=1).astype(BF16)
    out = jnp.concatenate([_mm(pout_ref[0], y), _mm(pout_ref[1], y)], axis=1)
    out_ref[...] = out.astype(out_ref.dtype).reshape(bsz, steps, WIDTH)


def _s5_permutations(bsz, steps):
    rows = steps * 2 * bsz
    r = np.arange(rows)
    p = np.zeros((2, rows, bsz * steps), np.float32)
    p[(r % (2 * bsz)) // bsz, r, (r % bsz) * steps + r // (2 * bsz)] = 1.0
    return jnp.asarray(p, BF16), jnp.asarray(p.transpose(0, 2, 1), BF16)


def _s5(z, bsz, layer, w_in, w_out, lam_re, lam_im, d_tab):
    t = z.shape[1]
    steps = S5_STEPS
    rows = steps * SUBLANES
    nblk = S5_HALF // S5_BLOCK_CH
    nst = nblk * S5_BLOCK_STATES
    p_in, p_out = _s5_permutations(bsz, steps)
    whole = lambda *shape: pl.BlockSpec(shape, lambda i: (0,) * len(shape))
    return pl.pallas_call(
        functools.partial(_s5_kernel, bsz=bsz),
        grid=(t // steps,),
        in_specs=[
            pl.BlockSpec((bsz, steps, WIDTH), lambda i: (0, i, ZC_BU // WIDTH)),
            whole(2, rows, bsz * steps),
            whole(2, bsz * steps, rows),
            _layer_spec(layer, nblk, 2 * S5_BLOCK_CH, 2 * S5_BLOCK_STATES),
            _layer_spec(layer, nblk, 2 * S5_BLOCK_STATES, 2 * S5_BLOCK_CH),
            _layer_spec(layer, SUBLANES, nst),
            _layer_spec(layer, SUBLANES, nst),
            _layer_spec(layer, rows, S5_HALF),
        ],
        out_specs=pl.BlockSpec((bsz, steps, WIDTH), lambda i: (0, i, 0)),
        out_shape=jax.ShapeDtypeStruct((bsz, t, WIDTH), BF16),
        scratch_shapes=[pltpu.VMEM((rows, 2 * nst), F32), pltpu.VMEM((SUBLANES, 2 * nst), F32)],
        compiler_params=pltpu.CompilerParams(
            dimension_semantics=("arbitrary",), vmem_limit_bytes=VMEM_LIMIT),
        name="s5",
    )(z, p_in, p_out, w_in, w_out, lam_re, lam_im, d_tab)


def _gla_kernel(q_ref, k_ref, v_ref, g_ref, sm_ref, aup_ref, ab_ref, ng_ref, out_ref, st_ref):
    nb, lb = q_ref.shape[0], q_ref.shape[1]
    ch = GLA_CHUNK
    kw = GLA_KEY_WIDTH

    @pl.when(pl.program_id(0) == 0)
    def _():
        st_ref[...] = jnp.zeros_like(st_ref)

    heads = range(GLA_HEADS)
    key_head = _iota((1, kw), 1) // GLA_HEAD_K
    val_head = _iota((1, WIDTH), 1) // GLA_HEAD_V
    key_masks = [jnp.where(key_head == h, 1.0, 0.0).astype(BF16) for h in heads]
    val_masks = [jnp.where(val_head == h, 1.0, 0.0).astype(BF16) for h in heads]

    def stack(x, masks):
        return jnp.concatenate([x * m for m in masks], axis=0)

    causal = _iota((ch, GLA_HEADS * ch), 0) >= _iota((ch, GLA_HEADS * ch), 1) % ch
    own_head = (_iota((WIDTH, kw), 0) // GLA_HEAD_V) == (_iota((WIDTH, kw), 1) // GLA_HEAD_K)
    lower = _tri(lb, ch)
    mid = ch // 2 - 1
    rss = [slice(c * ch, (c + 1) * ch) for c in range(lb // ch)]
    units = [(bi, c) for bi in range(nb) for c in range(lb // ch)]

    b_all = []
    for bi in range(nb):
        x = _mm(sm_ref[bi], aup_ref[...], pb=2) + ab_ref[...]
        b_all.append(_mm(lower, _log_sigmoid(x) * (1.0 / GLA_GATE_TAU), pb=3))
    q_in, q_at, k_at, k_st, dec = {}, {}, {}, {}, {}
    for bi, c in units:
        b = b_all[bi][rss[c]]
        b_mid, b_last = b[mid:mid + 1, :], b[ch - 1:ch, :]
        q = q_ref[bi, rss[c], :].astype(F32) * GLA_HEAD_K ** -0.5
        k = k_ref[bi, rss[c], :].astype(F32)
        q_in[bi, c] = (q * jnp.exp(b)).astype(BF16)
        q_at[bi, c] = (q * jnp.exp(b - b_mid)).astype(BF16)
        k_at[bi, c] = (k * jnp.exp(b_mid - b)).astype(BF16)
        k_st[bi, c] = (k * jnp.exp(b_last - b)).astype(BF16)
        dec[bi, c] = jnp.exp(b_last)
    v = {(bi, c): v_ref[bi, rss[c], :] for bi, c in units}
    attn = {u: jnp.where(causal, _mm(q_at[u], stack(k_at[u], key_masks), _NT), 0.0) for u in units}
    intra = {u: _mm(attn[u], stack(v[u], val_masks)) for u in units}
    grow = {u: jnp.where(own_head, _mm(v[u].astype(F32).T, k_st[u]), 0.0) for u in units}
    state = {}
    for bi in range(nb):
        st = st_ref[bi]
        for c in range(lb // ch):
            state[bi, c] = st
            st = dec[bi, c] * st + grow[bi, c]
        st_ref[bi] = st
    for bi, c in units:
        o = _mm(q_in[bi, c], state[bi, c], _NT) + intra[bi, c]
        for h in heads:
            vs = slice(h * GLA_HEAD_V, (h + 1) * GLA_HEAD_V)
            oh = o[:, vs]
            on = oh * lax.rsqrt(jnp.mean(oh * oh, axis=-1, keepdims=True) + NORM_EPS) * ng_ref[:, vs]
            g = g_ref[bi, rss[c], vs].astype(F32)
            out_ref[bi, rss[c], vs] = (on * (g * _sigmoid(g))).astype(out_ref.dtype)


def _gla(z, layer, a_up, a_bias, norm_g):
    bsz, t, _ = z.shape
    lb = TIME_BLOCK
    kw = GLA_KEY_WIDTH
    return pl.pallas_call(
        _gla_kernel,
        grid=(t // lb,),
        in_specs=[
            pl.BlockSpec((bsz, lb, kw), lambda c: (0, c, ZC_CQ // kw)),
            pl.BlockSpec((bsz, lb, kw), lambda c: (0, c, ZC_CK // kw)),
            pl.BlockSpec((bsz, lb, WIDTH), lambda c: (0, c, ZC_CV // WIDTH)),
            pl.BlockSpec((bsz, lb, WIDTH), lambda c: (0, c, ZC_CG // WIDTH)),
            pl.BlockSpec((bsz, lb, SMALL), lambda c: (0, c, ZC_SM // SMALL)),
            _layer_spec(layer, SMALL, kw),
            _layer_spec(layer, 1, kw),
            _layer_spec(layer, 1, WIDTH),
        ],
        out_specs=pl.BlockSpec((bsz, lb, WIDTH), lambda c: (0, c, 0)),
        out_shape=jax.ShapeDtypeStruct((bsz, t, WIDTH), BF16),
        scratch_shapes=[pltpu.VMEM((bsz, WIDTH, kw), F32)],
        compiler_params=pltpu.CompilerParams(
            dimension_semantics=("arbitrary",), vmem_limit_bytes=VMEM_LIMIT),
        name="gla",
    )(z, z, z, z, z, a_up, a_bias, norm_g)


def _rwkv_kernel(r_ref, k_ref, v_ref, t_ref, mu_ref, mut_ref,
                 w0_ref, wup_ref, a0_ref, aup_ref, gup_ref, kk_ref, ka_ref, rk_ref, ng_ref,
                 out_ref, pr_ref, pk_ref, pv_ref, pt_ref, st_ref, y_ref):
    nb, ch = r_ref.shape[0], r_ref.shape[1]
    lb = nb * ch
    gl = RWKV_GROUP_LANES
    gh = RWKV_GROUP_HEADS

    @pl.when(pl.program_id(0) == 0)
    def _():
        for ref in (pr_ref, pk_ref, pv_ref, pt_ref, st_ref):
            ref[...] = jnp.zeros_like(ref)

    def shift_mix(x_ref, prev_ref, mu):
        mixed = []
        for bi in range(nb):
            x = x_ref[bi].astype(F32)
            xs = _shift_rows(prev_ref.at[bi], x, 1)
            prev_ref[bi] = x[ch - prev_ref.shape[1]:]
            mixed.append(x + mu * (xs - x))
        return jnp.concatenate(mixed, axis=0)

    r = shift_mix(r_ref, pr_ref, mu_ref[0:1, :])
    k = shift_mix(k_ref, pk_ref, mu_ref[1:2, :])
    v = shift_mix(v_ref, pv_ref, mu_ref[2:3, :])
    tl = shift_mix(t_ref, pt_ref, mut_ref[...])

    ones_bd = _tri(gl, RWKV_HEAD) + _tri(gl, RWKV_HEAD, upper=True, strict=True)

    def head_sum(x):
        return jnp.concatenate([_mm(x[:, i * gl:(i + 1) * gl], ones_bd, pa=2) for i in range(WIDTH // gl)], axis=1)

    w_log = _log_sigmoid(w0_ref[...] + _mm(jnp.tanh(tl), wup_ref[...])) - 0.5
    log_w = -jnp.exp(w_log)
    a = _sigmoid(a0_ref[...] + _mm(tl, aup_ref[...]))
    g = _mm(_sigmoid(tl), gup_ref[...])
    kk = k * kk_ref[...]
    kk = kk / jnp.maximum(jnp.sqrt(head_sum(kk * kk)), 1e-12)
    k_rep = k * (1.0 + (a - 1.0) * ka_ref[...])
    av = -kk
    bv = kk * a
    bonus = head_sum(r * k_rep * rk_ref[...]) * v
    gc_all = _mm(_tri(lb, ch), log_w, pb=3)

    lane_head = _iota((1, gl), 1) // RWKV_HEAD
    masks = [jnp.where(lane_head == h, 1.0, 0.0).astype(BF16) for h in range(gh)]

    def stack(x):
        xb = x.astype(BF16)
        return jnp.concatenate([xb * m for m in masks], axis=0)

    def stack_t(x):
        return jnp.concatenate([x * m.astype(F32) for m in masks], axis=0).T

    s_lane = _iota((ch, gl), 1) % ch
    t_row = _iota((ch, gl), 0)
    strict = t_row > s_lane
    incl = t_row >= s_lane
    eye = jnp.where(t_row == s_lane, 1.0, 0.0)

    pairs = [(gi, bi) for gi in range(WIDTH // gl) for bi in range(nb)]
    pre = {}
    for gi, bi in pairs:
        ls = slice(gi * gl, (gi + 1) * gl)
        rs = slice(bi * ch, (bi + 1) * ch)
        gc = gc_all[rs, ls]
        g_last = gc[ch - 1:ch, :]
        e_inv = jnp.exp(-gc)
        e_end = jnp.exp(g_last - gc)
        r_t = (r[rs, ls] * jnp.exp(gc)).astype(BF16)
        a_t = (av[rs, ls] * jnp.exp(gc - log_w[rs, ls])).astype(BF16)
        scores = _mm(jnp.concatenate([a_t, r_t], axis=0),
                     jnp.concatenate([stack(bv[rs, ls] * e_inv), stack(k_rep[rs, ls] * e_inv)], axis=0), _NT)
        pre[gi, bi] = dict(
            r_t=r_t, a_bd=stack(a_t), v_bd=stack(v[rs, ls]), dec=jnp.exp(g_last),
            b_e=stack(bv[rs, ls] * e_end), k_e=stack(k_rep[rs, ls] * e_end),
            a_ab=jnp.where(strict, scores[:ch, :gl], 0.0),
            a_kv=jnp.concatenate([jnp.where(strict, scores[:ch, gl:], 0.0),
                                  jnp.where(incl, scores[ch:, gl:], 0.0)], axis=0).astype(BF16),
            a_rb=jnp.where(incl, scores[ch:, :gl], 0.0).astype(BF16))
    tinv = {key: eye + pre[key]["a_ab"] for key in pairs}
    power = {key: pre[key]["a_ab"].astype(BF16) for key in pairs}
    for key in pairs:
        power[key] = _mm(power[key], stack(power[key])).astype(BF16)
    step = 2
    while step < ch:
        for key in pairs:
            if 2 * step < ch:
                both = _mm(jnp.concatenate([power[key], tinv[key].astype(BF16)], axis=0), stack(power[key]))
                power[key] = both[:ch].astype(BF16)
                tinv[key] = tinv[key] + both[ch:]
            else:
                tinv[key] = tinv[key] + _mm(tinv[key], stack(power[key]))
        step *= 2

    from_v = {key: _mm(pre[key]["a_kv"], pre[key]["v_bd"]) for key in pairs}
    t_bf = {key: tinv[key].astype(BF16) for key in pairs}
    a_hat = {key: _mm(t_bf[key], pre[key]["a_bd"]).astype(BF16) for key in pairs}
    u0 = {key: _mm(t_bf[key], stack(from_v[key][:ch])) for key in pairs}
    grow = {key: _mm(pre[key]["v_bd"].astype(F32).T, pre[key]["k_e"]) for key in pairs}
    n_groups = WIDTH // gl
    state = {(gi, bi): st_ref[bi * n_groups + gi] for gi, bi in pairs}
    from_state = {key: _mm(jnp.concatenate([a_hat[key], pre[key]["r_t"]], axis=0), state[key], _NT) for key in pairs}
    u = {key: from_state[key][:ch] + u0[key] for key in pairs}
    for gi, bi in pairs:
        key = (gi, bi)
        y_ref[bi * ch:(bi + 1) * ch, gi * gl:(gi + 1) * gl] = (
            from_state[key][ch:] + _mm(pre[key]["a_rb"], stack(u[key])) + from_v[key][ch:])
    for gi, bi in pairs:
        key = (gi, bi)
        st_ref[bi * n_groups + gi] = (pre[key]["dec"] * state[key] + _mm(stack_t(u[key]), pre[key]["b_e"])
                                      + grow[key])

    y = y_ref[...]
    inv_n = 1.0 / RWKV_HEAD
    mu = head_sum(y) * inv_n
    yc = y - mu
    var = head_sum(yc * yc) * inv_n
    yn = yc * lax.rsqrt(var + RWKV_GN_EPS) * ng_ref[...]
    out_ref[...] = ((yn + bonus) * g).astype(out_ref.dtype).reshape(nb, ch, WIDTH)


def _rwkv(z, layer, mu_rkv, mu_t, w0, w_up, a0, a_up, g_up, k_k, k_a, r_k, norm_g):
    bsz, t, _ = z.shape
    ch = RWKV_CHUNK
    gl = RWKV_GROUP_LANES
    vec = _layer_spec(layer, 1, WIDTH)
    mat = _layer_spec(layer, RWKV_TAIL, WIDTH)
    return pl.pallas_call(
        _rwkv_kernel,
        grid=(t // ch,),
        in_specs=[
            pl.BlockSpec((bsz, ch, WIDTH), lambda c: (0, c, ZC_DR // WIDTH)),
            pl.BlockSpec((bsz, ch, WIDTH), lambda c: (0, c, ZC_DK // WIDTH)),
            pl.BlockSpec((bsz, ch, WIDTH), lambda c: (0, c, ZC_DV // WIDTH)),
            pl.BlockSpec((bsz, ch, RWKV_TAIL), lambda c: (0, c, ZC_DT // RWKV_TAIL)),
            _layer_spec(layer, 3, WIDTH), _layer_spec(layer, 1, RWKV_TAIL),
            vec, mat, vec, mat, mat, vec, vec, vec, vec,
        ],
        out_specs=pl.BlockSpec((bsz, ch, WIDTH), lambda c: (0, c, 0)),
        out_shape=jax.ShapeDtypeStruct((bsz, t, WIDTH), BF16),
        scratch_shapes=[
            pltpu.VMEM((bsz, SUBLANES, WIDTH), F32), pltpu.VMEM((bsz, SUBLANES, WIDTH), F32),
            pltpu.VMEM((bsz, SUBLANES, WIDTH), F32), pltpu.VMEM((bsz, SUBLANES, RWKV_TAIL), F32),
            pltpu.VMEM((bsz * (WIDTH // gl), gl, gl), F32),
            pltpu.VMEM((bsz * ch, WIDTH), F32),
        ],
        compiler_params=pltpu.CompilerParams(
            dimension_semantics=("arbitrary",), vmem_limit_bytes=VMEM_LIMIT),
        name="rwkv",
    )(z, z, z, z, mu_rkv, mu_t, w0, w_up, a0, a_up, g_up, k_k, k_a, r_k, norm_g)


def _merge_kernel(x_ref, ya_ref, zb_ref, yc_ref, yd_ref, g0_ref, g1_ref, g2_ref, g3_ref, gb_ref,
                  pa_ref, w1_ref, w2_ref, pc_ref, pd_ref, wo_ref, out_ref):
    d = D_MODEL

    def gate(g_ref, i):
        return _sigmoid(g_ref[...].astype(F32) + gb_ref[:, i * d:(i + 1) * d])

    zb = zb_ref[...]
    merged = gate(g0_ref, 0) * _mm(ya_ref[...], pa_ref[...])
    merged = merged + gate(g1_ref, 1) * (_mm(zb, w1_ref[...]) * _sigmoid(_mm(zb, w2_ref[...])))
    merged = merged + gate(g2_ref, 2) * _mm(yc_ref[...], pc_ref[...])
    merged = merged + gate(g3_ref, 3) * _mm(yd_ref[...], pd_ref[...])
    out_ref[...] = x_ref[...] + _mm(merged, wo_ref[...])


def _merge(x, z, ya, zb, yc, yd, layer, gate_bias, pa, w1, w2, pc, pd, wo, *, tm=512):
    n, d = x.shape
    tile = lambda w: pl.BlockSpec((tm, w), lambda i: (i, 0))
    gate = lambda b: pl.BlockSpec((tm, d), lambda i: (i, ZC_GATE // d + b))
    resident = lambda rows: pl.BlockSpec((None, rows, d), lambda i: (layer, 0, 0), pipeline_mode=pl.Buffered(1))
    proj = resident(WIDTH)
    return pl.pallas_call(
        _merge_kernel,
        grid=(n // tm,),
        in_specs=[tile(d), tile(WIDTH), tile(WIDTH), tile(WIDTH), tile(WIDTH),
                  gate(0), gate(1), gate(2), gate(3), _layer_spec(layer, 1, N_BRANCH * d),
                  proj, proj, proj, proj, proj, resident(d)],
        out_specs=tile(d),
        out_shape=jax.ShapeDtypeStruct((n, d), F32),
        compiler_params=pltpu.CompilerParams(
            dimension_semantics=("parallel",), vmem_limit_bytes=VMEM_LIMIT),
        name="merge",
    )(x, ya, zb, yc, yd, z, z, z, z, gate_bias, pa, w1, w2, pc, pd, wo)


def _xattn_kernel(x_ref, g_ref, wq_ref, kv_ref, wo_ref, out_ref):
    d = D_MODEL
    hd = XATTN_HEAD_DIM
    x = x_ref[...]
    q = _mm(_rms(x, g_ref[...]), wq_ref[...])
    kv = kv_ref[...]
    outs = []
    for h in range(XATTN_HEADS):
        sl = slice(h * hd, (h + 1) * hd)
        s = _mm(q[:, sl], kv[:, sl], _NT) * hd ** -0.5
        s = s - jnp.max(s, axis=-1, keepdims=True)
        p = jnp.exp(s)
        p = p / jnp.sum(p, axis=-1, keepdims=True)
        outs.append(_mm(p, kv[:, d + h * hd:d + (h + 1) * hd]))
    o = jnp.concatenate(outs, axis=1)
    out_ref[...] = x + _mm(o, wo_ref[...])


def _xattn(x, bsz, layer, gain, wq, kv, wo, *, tm=512):
    n, d = x.shape
    nblk = n // bsz // tm
    m_len = kv.shape[0] // bsz
    return pl.pallas_call(
        _xattn_kernel,
        grid=(bsz, nblk),
        in_specs=[
            pl.BlockSpec((tm, d), lambda b, c: (b * nblk + c, 0)),
            _layer_spec(layer, 1, d),
            _layer_spec(layer, d, d),
            pl.BlockSpec((m_len, 2 * d), lambda b, c: (b, 0)),
            _layer_spec(layer, d, d),
        ],
        out_specs=pl.BlockSpec((tm, d), lambda b, c: (b * nblk + c, 0)),
        out_shape=jax.ShapeDtypeStruct((n, d), F32),
        compiler_params=pltpu.CompilerParams(
            dimension_semantics=("parallel", "parallel"), vmem_limit_bytes=VMEM_LIMIT),
        name="xattn",
    )(x, gain, wq, kv, wo)


def _w_in_segments():
    o = np.cumsum((0, WIDTH, WIDTH, MLSTM_HEADS, MLSTM_HEADS, WIDTH, GLA_KEY_WIDTH, GLA_KEY_WIDTH, WIDTH, WIDTH,
                   GLA_GATE_RANK, 3 * WIDTH + RWKV_TAIL, N_BRANCH * D_MODEL))
    a_u, a_o, a_i, a_f, b_u, c_q, c_k, c_v, c_g, c_a, d_z, gate = range(12)
    seg = lambda i: (int(o[i]), int(o[i + 1] - o[i]))
    d_parts = [(int(o[d_z]) + i * WIDTH, WIDTH) for i in range(3)] + [(int(o[d_z]) + 3 * WIDTH, RWKV_TAIL)]
    return ([seg(gate), seg(a_u), seg(a_o), seg(b_u), seg(c_v), seg(c_g)] + d_parts[:3]
            + [seg(c_q), seg(c_k), d_parts[3], seg(a_i), seg(a_f), seg(c_a)])


def _regroup_kernel(w_ref, o_ref):
    lanes = w_ref.shape[1]
    segments = _w_in_segments()
    big = [s for s in segments if s[1] % SMALL == 0]
    dst = 0
    for src, width in big:
        o_ref[dst:dst + width, :] = w_ref[src:src + width, :].astype(o_ref.dtype)
        dst += width
    small = [w_ref[src:src + width, :] for src, width in segments if width % SMALL]
    used = sum(s.shape[0] for s in small)
    small.append(jnp.zeros((o_ref.shape[0] - dst - used, lanes), F32))
    o_ref[dst:, :] = jnp.concatenate(small, axis=0).astype(o_ref.dtype)


def _regroup_w_in(w_t, *, lanes=256):
    depth, n_in, d = w_t.shape
    return pl.pallas_call(
        _regroup_kernel,
        grid=(depth, d // lanes),
        in_specs=[pl.BlockSpec((None, n_in, lanes), lambda l, i: (l, 0, i))],
        out_specs=pl.BlockSpec((None, Z_COLS, lanes), lambda l, i: (l, 0, i)),
        out_shape=jax.ShapeDtypeStruct((depth, Z_COLS, d), BF16),
        compiler_params=pltpu.CompilerParams(
            dimension_semantics=("parallel", "parallel"), vmem_limit_bytes=VMEM_LIMIT),
        name="regroup_w_in",
    )(w_t)


def _s5_tables(a_re, a_im, log_step, b_re, b_im, c_re, c_im, d_skip, bsz):
    depth = a_re.shape[0]
    step = jnp.exp(log_step)[..., None]
    lam_re = jnp.minimum(a_re, -S5_MIN_NEG)
    lam_im = a_im
    mag = jnp.exp(lam_re * step)
    bar_re = mag * jnp.cos(lam_im * step)
    bar_im = mag * jnp.sin(lam_im * step)
    denom = lam_re * lam_re + lam_im * lam_im
    coef_re = ((bar_re - 1.0) * lam_re + bar_im * lam_im) / denom
    coef_im = (bar_im * lam_re - (bar_re - 1.0) * lam_im) / denom
    bb_re = coef_re[..., None] * b_re - coef_im[..., None] * b_im
    bb_im = coef_re[..., None] * b_im + coef_im[..., None] * b_re
    nb = S5_HALF // S5_BLOCK_CH
    gpb = S5_BLOCK_GROUPS
    eye = jnp.eye(gpb, dtype=F32)

    def in_block(bb):
        bb = bb.reshape(depth, 2, nb, gpb, S5_STATE, S5_GROUP)
        return jnp.einsum('lhjgpc,gk->lhjgckp', bb, eye).reshape(depth, 2, nb, S5_BLOCK_CH, S5_BLOCK_STATES)

    def out_block(cc):
        cc = cc.reshape(depth, 2, nb, gpb, S5_GROUP, S5_STATE)
        return jnp.einsum('lhjgcp,gk->lhjgpkc', cc, eye).reshape(depth, 2, nb, S5_BLOCK_STATES, S5_BLOCK_CH)

    w_in = jnp.concatenate([in_block(bb_re), in_block(bb_im)], axis=-1)
    w_in = w_in.transpose(0, 2, 1, 3, 4).reshape(depth, nb, 2 * S5_BLOCK_CH, 2 * S5_BLOCK_STATES)
    w_out = jnp.concatenate([out_block(c_re), -out_block(c_im)], axis=-2)
    w_out = w_out.transpose(0, 2, 3, 1, 4).reshape(depth, nb, 2 * S5_BLOCK_STATES, 2 * S5_BLOCK_CH)

    def rows(t, width):
        t = t.reshape(depth, 2, 1, width)
        return jnp.broadcast_to(t, (depth, 2, bsz, width)).reshape(depth, 2 * bsz, width)

    nst = nb * S5_BLOCK_STATES
    lam_re_rows = rows(bar_re.reshape(depth, 2 * nst), nst)
    lam_im_rows = rows(bar_im.reshape(depth, 2 * nst), nst)
    d_tab = jnp.tile(rows(d_skip, S5_HALF), (1, S5_STEPS, 1))
    return w_in.astype(BF16), w_out.astype(BF16), lam_re_rows, lam_im_rows, d_tab


def _pad_rows(w, start, total):
    return jnp.pad(w, ((0, 0), (start, total - start - w.shape[1]), (0, 0)))


def kernel(x, mem, ffn1_norm, ffn1_w_gate, ffn1_w_up, ffn1_w_down, mix_norm, w_in, gate_bias, mlstm_conv, mlstm_wq, mlstm_wk, mlstm_wv, mlstm_b_i, mlstm_b_f, mlstm_norm, mlstm_proj, s5_a_re, s5_a_im, s5_log_step, s5_b_re, s5_b_im, s5_c_re, s5_c_im, s5_d, s5_glu_w1, s5_glu_w2, gla_a_up, gla_a_bias, gla_norm, gla_proj, rwkv_mu, rwkv_w0, rwkv_w_up, rwkv_a0, rwkv_a_up, rwkv_g_up, rwkv_k_k, rwkv_k_a, rwkv_r_k, rwkv_norm, rwkv_proj, w_out, xattn_norm, mem_norm, xattn_wq, xattn_wk, xattn_wv, xattn_wo, ffn2_norm, ffn2_w_gate, ffn2_w_up, ffn2_w_down, final_norm):
    bsz, t, d = x.shape
    depth = w_in.shape[0]
    assert d == D_MODEL and t % TIME_BLOCK == 0 and 2 * bsz == SUBLANES
    xs = x.reshape(bsz * t, d)
    mems = mem.reshape(bsz * mem.shape[1], d)
    bf = lambda w: w.astype(BF16)
    vec = lambda v: v.reshape(depth, 1, -1)
    fin = final_norm.reshape(1, d)

    w_in_r = _regroup_w_in(jnp.swapaxes(w_in, 1, 2))
    gate_b = jnp.pad(jnp.concatenate([mlstm_b_i, mlstm_b_f], axis=1), ((0, 0), (0, SMALL - 2 * MLSTM_HEADS)))
    s5_w_in, s5_w_out, s5_lre, s5_lim, s5_dtab = _s5_tables(
        s5_a_re, s5_a_im, s5_log_step, s5_b_re, s5_b_im, s5_c_re, s5_c_im, s5_d, bsz)
    gla_up = _pad_rows(gla_a_up, 2 * MLSTM_HEADS, SMALL)
    mu_rkv = rwkv_mu[:, :3 * WIDTH].reshape(depth, 3, WIDTH)
    mu_t = vec(rwkv_mu[:, 3 * WIDTH:])
    rw_wup = bf(_pad_rows(rwkv_w_up, 0, RWKV_TAIL))
    rw_aup = bf(_pad_rows(rwkv_a_up, RWKV_DECAY_RANK, RWKV_TAIL))
    rw_gup = bf(_pad_rows(rwkv_g_up, RWKV_DECAY_RANK + RWKV_ICLR_RANK, RWKV_TAIL))
    w_kv = bf(jnp.concatenate([xattn_wk, xattn_wv], axis=2))
    wq_a, wk_a, wv_a = bf(mlstm_wq), bf(mlstm_wk), bf(mlstm_wv)
    p_a, p_b1, p_b2, p_c, p_d, p_o = (bf(mlstm_proj), bf(s5_glu_w1), bf(s5_glu_w2), bf(gla_proj), bf(rwkv_proj),
                                      bf(w_out))
    x_wq, x_wo = bf(xattn_wq), bf(xattn_wo)
    f1_g, f1_u, f1_d = bf(ffn1_w_gate), bf(ffn1_w_up), bf(ffn1_w_down)
    f2_g, f2_u, f2_d = bf(ffn2_w_gate), bf(ffn2_w_up), bf(ffn2_w_down)

    for l in range(depth):
        xs = _ffn(xs, vec(ffn1_norm), f1_g, f1_u, f1_d, fin, layer=l, final=False)

        z = _in_proj(xs, vec(mix_norm), w_in_r, layer=l)

        z3 = z.reshape(bsz, t, Z_COLS)
        y_a = _mlstm(z3, l, mlstm_conv, wq_a, wk_a, wv_a, vec(gate_b), gate_b.reshape(depth, SMALL, 1),
                     vec(mlstm_norm)).reshape(bsz * t, WIDTH)

        z_b = _s5(z3, bsz, l, s5_w_in, s5_w_out, s5_lre, s5_lim, s5_dtab).reshape(bsz * t, WIDTH)

        y_c = _gla(z3, l, gla_up, vec(gla_a_bias), vec(gla_norm)).reshape(bsz * t, WIDTH)

        y_d = _rwkv(z3, l, mu_rkv, mu_t, vec(rwkv_w0), rw_wup, vec(rwkv_a0), rw_aup, rw_gup,
                    vec(rwkv_k_k), vec(rwkv_k_a), vec(rwkv_r_k), vec(rwkv_norm)).reshape(bsz * t, WIDTH)

        xs = _merge(xs, z, y_a, z_b, y_c, y_d, l, vec(gate_bias), p_a, p_b1, p_b2, p_c, p_d, p_o)

        kv = _norm_matmul(mems, vec(mem_norm), w_kv, layer=l, tm=mems.shape[0], tn=512, name="mem_kv",
                          out_dtype=BF16)
        xs = _xattn(xs, bsz, l, vec(xattn_norm), x_wq, kv, x_wo)

        xs = _ffn(xs, vec(ffn2_norm), f2_g, f2_u, f2_d, fin, layer=l,
                  final=(l == depth - 1))
    return xs.reshape(bsz, t, d)
```

```python
import functools
import math

import jax
import jax.numpy as jnp
import numpy as np
from jax import lax
from jax.experimental import pallas as pl
from jax.experimental.pallas import tpu as pltpu

F32 = jnp.float32
BF16 = jnp.bfloat16

D_MODEL = 1024
D_FF = 11 * D_MODEL // 4
NORM_EPS = 1e-6
FFN_HALF = 0.5
N_BRANCH = 4
WIDTH = D_MODEL // 2

MLSTM_HEADS = 4
MLSTM_HEAD_DIM = WIDTH // MLSTM_HEADS
MLSTM_CONV = 4
MLSTM_CHUNK = 256

S5_GROUP = 16
S5_GROUPS = WIDTH // S5_GROUP
S5_STATE = 64
S5_MIN_NEG = 1e-4
S5_HALF = WIDTH // 2
S5_BLOCK_GROUPS = 8
S5_BLOCK_CH = S5_BLOCK_GROUPS * S5_GROUP
S5_BLOCK_STATES = S5_BLOCK_GROUPS * S5_STATE
S5_STEPS = 64
SUBLANES = 8

GLA_HEADS = 4
GLA_KEY_WIDTH = WIDTH // 2
GLA_HEAD_K = GLA_KEY_WIDTH // GLA_HEADS
GLA_HEAD_V = WIDTH // GLA_HEADS
GLA_GATE_RANK = 16
GLA_GATE_TAU = 16.0
GLA_CHUNK = 64

RWKV_HEAD = 64
RWKV_HEADS = WIDTH // RWKV_HEAD
RWKV_DECAY_RANK = 64
RWKV_ICLR_RANK = 64
RWKV_GATE_RANK = 128
RWKV_TAIL = RWKV_DECAY_RANK + RWKV_ICLR_RANK + RWKV_GATE_RANK
RWKV_GN_EPS = 64e-5
RWKV_CHUNK = 64
RWKV_GROUP_LANES = 256
RWKV_GROUP_HEADS = RWKV_GROUP_LANES // RWKV_HEAD

XATTN_HEADS = 4
XATTN_HEAD_DIM = D_MODEL // XATTN_HEADS

ZC_GATE = 0
ZC_AU = 4096
ZC_AO = 4608
ZC_BU = 5120
ZC_CV = 5632
ZC_CG = 6144
ZC_DR = 6656
ZC_DK = 7168
ZC_DV = 7680
ZC_CQ = 8192
ZC_CK = 8448
ZC_DT = 8704
ZC_SM = 8960
Z_COLS = 9216
SMALL = 128

TIME_BLOCK = 256
VMEM_LIMIT = 56 * 1024 * 1024

_NN = (((1,), (0,)), ((), ()))
_NT = (((1,), (1,)), ((), ()))


def _split(x, n):
    parts = []
    rest = x
    for i in range(n):
        p = rest.astype(BF16)
        parts.append(p)
        if i + 1 < n:
            rest = rest - p.astype(F32)
    return parts


def _mm(a, b, dn=_NN, pa=1, pb=1):
    ap, bp = _split(a, pa), _split(b, pb)
    order = max(pa, pb)
    out = None
    for i, x in enumerate(ap):
        for j, y in enumerate(bp):
            if i + j < order:
                t = lax.dot_general(x, y, dn, preferred_element_type=F32)
                out = t if out is None else out + t
    return out


def _rms(x, g):
    return x * lax.rsqrt(jnp.mean(x * x, axis=-1, keepdims=True) + NORM_EPS) * g


def _sigmoid(x):
    return 0.5 * jnp.tanh(0.5 * x) + 0.5


def _log_sigmoid(x):
    return jnp.minimum(x, 0.0) - jnp.log(1.0 + jnp.exp(-jnp.abs(x)))


def _iota(shape, dim):
    return lax.broadcasted_iota(jnp.int32, shape, dim)


def _tri(n, block, upper=False, strict=False):
    r, c = _iota((n, n), 0), _iota((n, n), 1)
    same = (r // block) == (c // block)
    if upper:
        keep = (r < c) if strict else (r <= c)
    else:
        keep = (r > c) if strict else (r >= c)
    return jnp.where(same & keep, 1.0, 0.0).astype(BF16)


def _shift_rows(prev_ref, x, k):
    ext = jnp.concatenate([prev_ref[...], x], axis=0)
    return pltpu.roll(ext, k, 0)[prev_ref.shape[0]:]


def _layer_spec(layer, *shape):
    return pl.BlockSpec((None,) + shape, lambda *_: (layer,) + (0,) * len(shape))


def _ffn_kernel(x_ref, g_ref, wg_ref, wu_ref, wd_ref, fg_ref, o_ref, *, final, tf):
    x = x_ref[...]
    xn = _rms(x, g_ref[...]).astype(BF16)
    acc = None
    for j in range(wg_ref.shape[1] // tf):
        cols = slice(j * tf, (j + 1) * tf)
        gate = _mm(xn, wg_ref[:, cols])
        up = _mm(xn, wu_ref[:, cols])
        part = _mm(gate * _sigmoid(gate) * up, wd_ref[cols, :])
        acc = part if acc is None else acc + part
    y = x + FFN_HALF * acc
    if final:
        y = _rms(y, fg_ref[...])
    o_ref[...] = y


def _ffn(x, gain, wg, wu, wd, final_gain, *, layer, final, tm=512, tf=256):
    n, d = x.shape
    f = wg.shape[2]
    resident = lambda *shape: pl.BlockSpec((None,) + shape, lambda i: (layer, 0, 0), pipeline_mode=pl.Buffered(1))
    return pl.pallas_call(
        functools.partial(_ffn_kernel, final=final, tf=tf),
        grid=(n // tm,),
        in_specs=[
            pl.BlockSpec((tm, d), lambda i: (i, 0)),
            _layer_spec(layer, 1, d),
            resident(d, f), resident(d, f), resident(f, d),
            pl.BlockSpec((1, d), lambda i: (0, 0)),
        ],
        out_specs=pl.BlockSpec((tm, d), lambda i: (i, 0)),
        out_shape=jax.ShapeDtypeStruct((n, d), F32),
        compiler_params=pltpu.CompilerParams(
            dimension_semantics=("parallel",), vmem_limit_bytes=VMEM_LIMIT),
        name="ffn",
    )(x, gain, wg, wu, wd, final_gain)


def _in_proj_kernel(x_ref, g_ref, w_ref, o_ref, *, tn):
    xn = _rms(x_ref[...], g_ref[...]).astype(BF16)
    for j in range(w_ref.shape[0] // tn):
        cols = slice(j * tn, (j + 1) * tn)
        o_ref[:, cols] = _mm(xn, w_ref[cols, :], _NT).astype(o_ref.dtype)


def _in_proj(x, gain, w, *, layer, tm=512, tn=1536):
    n, d = x.shape
    m = w.shape[1]
    return pl.pallas_call(
        functools.partial(_in_proj_kernel, tn=tn),
        grid=(n // tm,),
        in_specs=[
            pl.BlockSpec((tm, d), lambda i: (i, 0)),
            _layer_spec(layer, 1, d),
            pl.BlockSpec((None, m, d), lambda i: (layer, 0, 0), pipeline_mode=pl.Buffered(1)),
        ],
        out_specs=pl.BlockSpec((tm, m), lambda i: (i, 0)),
        out_shape=jax.ShapeDtypeStruct((n, m), BF16),
        compiler_params=pltpu.CompilerParams(
            dimension_semantics=("parallel",), vmem_limit_bytes=VMEM_LIMIT),
        name="in_proj",
    )(x, gain, w)


def _norm_matmul_kernel(x_ref, g_ref, w_ref, o_ref, xn_ref):
    @pl.when(pl.program_id(1) == 0)
    def _():
        xn_ref[...] = _rms(x_ref[...], g_ref[...]).astype(BF16)

    o_ref[...] = _mm(xn_ref[...], w_ref[...]).astype(o_ref.dtype)


def _norm_matmul(x, gain, w, *, layer, tm, tn, name, out_dtype):
    n, d = x.shape
    m = w.shape[2]
    return pl.pallas_call(
        _norm_matmul_kernel,
        grid=(n // tm, m // tn),
        in_specs=[
            pl.BlockSpec((tm, d), lambda i, j: (i, 0)),
            _layer_spec(layer, 1, d),
            pl.BlockSpec((None, d, tn), lambda i, j: (layer, 0, j)),
        ],
        out_specs=pl.BlockSpec((tm, tn), lambda i, j: (i, j)),
        out_shape=jax.ShapeDtypeStruct((n, m), out_dtype),
        scratch_shapes=[pltpu.VMEM((tm, d), BF16)],
        compiler_params=pltpu.CompilerParams(
            dimension_semantics=("parallel", "arbitrary"), vmem_limit_bytes=VMEM_LIMIT),
        name=name,
    )(x, gain, w)


def _mlstm_kernel(u_ref, o_ref, sm_ref, conv_ref, wq_ref, wk_ref, wv_ref, brow_ref, bcol_ref, ng_ref,
                  out_ref, uprev_ref, c_ref, m_ref):
    nb, lb = u_ref.shape[0], u_ref.shape[1]
    e = MLSTM_HEAD_DIM
    nh = MLSTM_HEADS

    @pl.when(pl.program_id(0) == 0)
    def _():
        uprev_ref[...] = jnp.zeros_like(uprev_ref)
        c_ref[...] = jnp.zeros_like(c_ref)
        m_ref[...] = jnp.zeros_like(m_ref)

    u, uc, pre, pre_t, b_cols, b_rows = [], [], [], [], [], []
    lower, upper = _tri(lb, lb), _tri(lb, lb, upper=True)
    for bi in range(nb):
        ub = u_ref[bi].astype(F32)
        conv = None
        for j in range(MLSTM_CONV):
            k = MLSTM_CONV - 1 - j
            term = conv_ref[j:j + 1, :] * (_shift_rows(uprev_ref.at[bi], ub, k) if k else ub)
            conv = term if conv is None else conv + term
        uprev_ref[bi] = ub[lb - uprev_ref.shape[1]:]
        u.append(ub)
        uc.append(conv * _sigmoid(conv))
        sm = sm_ref[bi].astype(F32)
        pre.append(sm + brow_ref[...])
        pre_t.append(sm.T[0:8, :] + bcol_ref[0:8, :])
        b_cols.append(_mm(lower, _log_sigmoid(pre[bi]), pb=3))
        b_rows.append(_mm(_log_sigmoid(pre_t[bi]), upper, pa=3))

    causal = _iota((lb, lb), 0) >= _iota((lb, lb), 1)
    streams = [(bi, h) for bi in range(nb) for h in range(nh)]
    sls = [slice(h * e, (h + 1) * e) for h in range(nh)]
    ones = jnp.ones((lb, e), BF16)
    q = {s: _mm(uc[s[0]][:, sls[s[1]]], wq_ref[s[1]]).astype(BF16) for s in streams}
    k = {s: _mm(uc[s[0]][:, sls[s[1]]], wk_ref[s[1]]) * e ** -0.5 for s in streams}
    v = {s: jnp.concatenate([_mm(u[s[0]][:, sls[s[1]]], wv_ref[s[1]]).astype(BF16), ones], axis=1) for s in streams}
    bc = {(bi, h): b_cols[bi][:, nh + h:nh + h + 1] for bi, h in streams}
    m_st = {(bi, h): m_ref[bi * SUBLANES + h:bi * SUBLANES + h + 1, 0:1] for bi, h in streams}
    dmat = {(bi, h): jnp.where(causal, bc[bi, h] - b_rows[bi][nh + h:nh + h + 1, :] + pre_t[bi][h:h + 1, :], -jnp.inf)
            for bi, h in streams}
    m_inter = {s: bc[s] + m_st[s] for s in streams}
    m_row = {s: jnp.maximum(m_inter[s], jnp.max(dmat[s], axis=-1, keepdims=True)) for s in streams}
    qk = {s: _mm(q[s], k[s], _NT) for s in streams}
    w = {s: jnp.exp(dmat[s] - m_row[s]) * qk[s] for s in streams}
    from_state = {(bi, h): _mm(q[bi, h], c_ref[bi * nh + h]) for bi, h in streams}
    from_chunk = {s: _mm(w[s], v[s]) for s in streams}
    for bi, h in streams:
        s = (bi, h)
        tot = jnp.exp(m_inter[s] - m_row[s]) * from_state[s] + from_chunk[s]
        hh = tot[:, :e] / jnp.maximum(jnp.abs(tot[:, e:e + 1]), jnp.exp(-m_row[s]))
        hn = hh * lax.rsqrt(jnp.mean(hh * hh, axis=-1, keepdims=True) + NORM_EPS) * ng_ref[:, sls[h]]
        out_ref[bi, :, sls[h]] = (hn * _sigmoid(o_ref[bi, :, sls[h]].astype(F32))).astype(out_ref.dtype)
    for bi, h in streams:
        s = (bi, h)
        b_last = bc[s][lb - 1:lb, :]
        e_col = b_last - bc[s] + pre[bi][:, h:h + 1]
        m_new = jnp.maximum(b_last + m_st[s], jnp.max(e_col, axis=0, keepdims=True))
        decay = jnp.exp(b_last + m_st[s] - m_new)
        wk = jnp.exp(e_col - m_new) * k[s]
        c_ref[bi * nh + h] = decay * c_ref[bi * nh + h] + _mm(wk.T, v[s])
        m_ref[bi * SUBLANES + h:bi * SUBLANES + h + 1, :] = jnp.broadcast_to(m_new, (1, m_ref.shape[1]))


def _mlstm(z, layer, conv_w, wq, wk, wv, b_row, b_col, norm_g):
    bsz, t, _ = z.shape
    lb = MLSTM_CHUNK
    hd = MLSTM_HEAD_DIM
    return pl.pallas_call(
        _mlstm_kernel,
        grid=(t // lb,),
        in_specs=[
            pl.BlockSpec((bsz, lb, WIDTH), lambda c: (0, c, ZC_AU // WIDTH)),
            pl.BlockSpec((bsz, lb, WIDTH), lambda c: (0, c, ZC_AO // WIDTH)),
            pl.BlockSpec((bsz, lb, SMALL), lambda c: (0, c, ZC_SM // SMALL)),
            _layer_spec(layer, MLSTM_CONV, WIDTH),
            _layer_spec(layer, MLSTM_HEADS, hd, hd),
            _layer_spec(layer, MLSTM_HEADS, hd, hd),
            _layer_spec(layer, MLSTM_HEADS, hd, hd),
            _layer_spec(layer, 1, SMALL),
            _layer_spec(layer, SMALL, 1),
            _layer_spec(layer, 1, WIDTH),
        ],
        out_specs=pl.BlockSpec((bsz, lb, WIDTH), lambda c: (0, c, 0)),
        out_shape=jax.ShapeDtypeStruct((bsz, t, WIDTH), BF16),
        scratch_shapes=[
            pltpu.VMEM((bsz, SUBLANES, WIDTH), F32),
            pltpu.VMEM((bsz * MLSTM_HEADS, hd, 2 * hd), F32),
            pltpu.VMEM((bsz * SUBLANES, hd), F32),
        ],
        compiler_params=pltpu.CompilerParams(
            dimension_semantics=("arbitrary",), vmem_limit_bytes=VMEM_LIMIT),
        name="mlstm",
    )(z, z, z, conv_w, wq, wk, wv, b_row, b_col, norm_g)


def _s5_kernel(u_ref, pin_ref, pout_ref, win_ref, wout_ref, lre_ref, lim_ref, d_ref, out_ref, x_ref, c_ref, *, bsz):
    steps = u_ref.shape[1]
    rows = steps * SUBLANES
    ns = S5_BLOCK_STATES
    cw = S5_BLOCK_CH
    nblk = S5_HALF // cw

    @pl.when(pl.program_id(0) == 0)
    def _():
        c_ref[...] = jnp.zeros_like(c_ref)

    u_bt = u_ref[...].reshape(bsz * steps, WIDTH)
    u = _mm(pin_ref[0], u_bt[:, :S5_HALF]) + _mm(pin_ref[1], u_bt[:, S5_HALF:])
    upper = (_iota((rows, 1), 0) % SUBLANES) >= bsz
    for j in range(nblk):
        uj = u[:, j * cw:(j + 1) * cw]
        uext = jnp.concatenate([jnp.where(upper, 0.0, uj), jnp.where(upper, uj, 0.0)], axis=1)
        x_ref[:, j * 2 * ns:(j + 1) * 2 * ns] = _mm(uext, win_ref[j])

    lre, lim = lre_ref[...], lim_ref[...]

    ys = []
    for j in range(nblk):
        lr, li = lre[:, j * ns:(j + 1) * ns], lim[:, j * ns:(j + 1) * ns]
        re_cols = slice(j * 2 * ns, j * 2 * ns + ns)
        im_cols = slice(j * 2 * ns + ns, (j + 1) * 2 * ns)
        sr, si = c_ref[:, re_cols], c_ref[:, im_cols]
        for t in range(steps):
            rows_t = slice(t * SUBLANES, (t + 1) * SUBLANES)
            sr, si = (lr * sr - li * si + x_ref[rows_t, re_cols],
                      lr * si + li * sr + x_ref[rows_t, im_cols])
            x_ref[rows_t, re_cols] = sr
            x_ref[rows_t, im_cols] = si
        c_ref[:, re_cols] = sr
        c_ref[:, im_cols] = si
        res = _mm(x_ref[:, j * 2 * ns:(j + 1) * 2 * ns], wout_ref[j])
        cols = slice(j * cw, (j + 1) * cw)
        y = jnp.where(upper, res[:, cw:], res[:, :cw]) + d_ref[:, cols] * u[:, cols]
        ys.append(0.5 * y * (1.0 + jnp.tanh(math.sqrt(2.0 / math.pi) * (y + 0.044715 * y * y * y))))
    y = jnp.concatenate(ys, axis=1).astype(BF16)
    out = jnp.concatenate([_mm(pout_ref[0], y), _mm(pout_ref[1], y)], axis=1)
    out_ref[...] = out.astype(out_ref.dtype).reshape(bsz, steps, WIDTH)


def _s5_permutations(bsz, steps):
    rows = steps * 2 * bsz
    r = np.arange(rows)
    p = np.zeros((2, rows, bsz * steps), np.float32)
    p[(r % (2 * bsz)) // bsz, r, (r % bsz) * steps + r // (2 * bsz)] = 1.0
    return jnp.asarray(p, BF16), jnp.asarray(p.transpose(0, 2, 1), BF16)


def _s5(z, bsz, layer, w_in, w_out, lam_re, lam_im, d_tab):
    t = z.shape[1]
    steps = S5_STEPS
    rows = steps * SUBLANES
    nblk = S5_HALF // S5_BLOCK_CH
    nst = nblk * S5_BLOCK_STATES
    p_in, p_out = _s5_permutations(bsz, steps)
    whole = lambda *shape: pl.BlockSpec(shape, lambda i: (0,) * len(shape))
    return pl.pallas_call(
        functools.partial(_s5_kernel, bsz=bsz),
        grid=(t // steps,),
        in_specs=[
            pl.BlockSpec((bsz, steps, WIDTH), lambda i: (0, i, ZC_BU // WIDTH)),
            whole(2, rows, bsz * steps),
            whole(2, bsz * steps, rows),
            _layer_spec(layer, nblk, 2 * S5_BLOCK_CH, 2 * S5_BLOCK_STATES),
            _layer_spec(layer, nblk, 2 * S5_BLOCK_STATES, 2 * S5_BLOCK_CH),
            _layer_spec(layer, SUBLANES, nst),
            _layer_spec(layer, SUBLANES, nst),
            _layer_spec(layer, rows, S5_HALF),
        ],
        out_specs=pl.BlockSpec((bsz, steps, WIDTH), lambda i: (0, i, 0)),
        out_shape=jax.ShapeDtypeStruct((bsz, t, WIDTH), BF16),
        scratch_shapes=[pltpu.VMEM((rows, 2 * nst), F32), pltpu.VMEM((SUBLANES, 2 * nst), F32)],
        compiler_params=pltpu.CompilerParams(
            dimension_semantics=("arbitrary",), vmem_limit_bytes=VMEM_LIMIT),
        name="s5",
    )(z, p_in, p_out, w_in, w_out, lam_re, lam_im, d_tab)


def _gla_kernel(q_ref, k_ref, v_ref, g_ref, sm_ref, aup_ref, ab_ref, ng_ref, out_ref, st_ref):
    nb, lb = q_ref.shape[0], q_ref.shape[1]
    ch = GLA_CHUNK
    kw = GLA_KEY_WIDTH

    @pl.when(pl.program_id(0) == 0)
    def _():
        st_ref[...] = jnp.zeros_like(st_ref)

    heads = range(GLA_HEADS)
    key_head = _iota((1, kw), 1) // GLA_HEAD_K
    val_head = _iota((1, WIDTH), 1) // GLA_HEAD_V
    key_masks = [jnp.where(key_head == h, 1.0, 0.0).astype(BF16) for h in heads]
    val_masks = [jnp.where(val_head == h, 1.0, 0.0).astype(BF16) for h in heads]

    def stack(x, masks):
        return jnp.concatenate([x * m for m in masks], axis=0)

    causal = _iota((ch, GLA_HEADS * ch), 0) >= _iota((ch, GLA_HEADS * ch), 1) % ch
    own_head = (_iota((WIDTH, kw), 0) // GLA_HEAD_V) == (_iota((WIDTH, kw), 1) // GLA_HEAD_K)
    lower = _tri(lb, ch)
    mid = ch // 2 - 1
    rss = [slice(c * ch, (c + 1) * ch) for c in range(lb // ch)]
    units = [(bi, c) for bi in range(nb) for c in range(lb // ch)]

    b_all = []
    for bi in range(nb):
        x = _mm(sm_ref[bi], aup_ref[...], pb=2) + ab_ref[...]
        b_all.append(_mm(lower, _log_sigmoid(x) * (1.0 / GLA_GATE_TAU), pb=3))
    q_in, q_at, k_at, k_st, dec = {}, {}, {}, {}, {}
    for bi, c in units:
        b = b_all[bi][rss[c]]
        b_mid, b_last = b[mid:mid + 1, :], b[ch - 1:ch, :]
        q = q_ref[bi, rss[c], :].astype(F32) * GLA_HEAD_K ** -0.5
        k = k_ref[bi, rss[c], :].astype(F32)
        q_in[bi, c] = (q * jnp.exp(b)).astype(BF16)
        q_at[bi, c] = (q * jnp.exp(b - b_mid)).astype(BF16)
        k_at[bi, c] = (k * jnp.exp(b_mid - b)).astype(BF16)
        k_st[bi, c] = (k * jnp.exp(b_last - b)).astype(BF16)
        dec[bi, c] = jnp.exp(b_last)
    v = {(bi, c): v_ref[bi, rss[c], :] for bi, c in units}
    attn = {u: jnp.where(causal, _mm(q_at[u], stack(k_at[u], key_masks), _NT), 0.0) for u in units}
    intra = {u: _mm(attn[u], stack(v[u], val_masks)) for u in units}
    grow = {u: jnp.where(own_head, _mm(v[u].astype(F32).T, k_st[u]), 0.0) for u in units}
    state = {}
    for bi in range(nb):
        st = st_ref[bi]
        for c in range(lb // ch):
            state[bi, c] = st
            st = dec[bi, c] * st + grow[bi, c]
        st_ref[bi] = st
    for bi, c in units:
        o = _mm(q_in[bi, c], state[bi, c], _NT) + intra[bi, c]
        for h in heads:
            vs = slice(h * GLA_HEAD_V, (h + 1) * GLA_HEAD_V)
            oh = o[:, vs]
            on = oh * lax.rsqrt(jnp.mean(oh * oh, axis=-1, keepdims=True) + NORM_EPS) * ng_ref[:, vs]
            g = g_ref[bi, rss[c], vs].astype(F32)
            out_ref[bi, rss[c], vs] = (on * (g * _sigmoid(g))).astype(out_ref.dtype)


def _gla(z, layer, a_up, a_bias, norm_g):
    bsz, t, _ = z.shape
    lb = TIME_BLOCK
    kw = GLA_KEY_WIDTH
    return pl.pallas_call(
        _gla_kernel,
        grid=(t // lb,),
        in_specs=[
            pl.BlockSpec((bsz, lb, kw), lambda c: (0, c, ZC_CQ // kw)),
            pl.BlockSpec((bsz, lb, kw), lambda c: (0, c, ZC_CK // kw)),
            pl.BlockSpec((bsz, lb, WIDTH), lambda c: (0, c, ZC_CV // WIDTH)),
            pl.BlockSpec((bsz, lb, WIDTH), lambda c: (0, c, ZC_CG // WIDTH)),
            pl.BlockSpec((bsz, lb, SMALL), lambda c: (0, c, ZC_SM // SMALL)),
            _layer_spec(layer, SMALL, kw),
            _layer_spec(layer, 1, kw),
            _layer_spec(layer, 1, WIDTH),
        ],
        out_specs=pl.BlockSpec((bsz, lb, WIDTH), lambda c: (0, c, 0)),
        out_shape=jax.ShapeDtypeStruct((bsz, t, WIDTH), BF16),
        scratch_shapes=[pltpu.VMEM((bsz, WIDTH, kw), F32)],
        compiler_params=pltpu.CompilerParams(
            dimension_semantics=("arbitrary",), vmem_limit_bytes=VMEM_LIMIT),
        name="gla",
    )(z, z, z, z, z, a_up, a_bias, norm_g)


def _rwkv_kernel(r_ref, k_ref, v_ref, t_ref, mu_ref, mut_ref,
                 w0_ref, wup_ref, a0_ref, aup_ref, gup_ref, kk_ref, ka_ref, rk_ref, ng_ref,
                 out_ref, pr_ref, pk_ref, pv_ref, pt_ref, st_ref, y_ref):
    nb, ch = r_ref.shape[0], r_ref.shape[1]
    lb = nb * ch
    gl = RWKV_GROUP_LANES
    gh = RWKV_GROUP_HEADS

    @pl.when(pl.program_id(0) == 0)
    def _():
        for ref in (pr_ref, pk_ref, pv_ref, pt_ref, st_ref):
            ref[...] = jnp.zeros_like(ref)

    def shift_mix(x_ref, prev_ref, mu):
        mixed = []
        for bi in range(nb):
            x = x_ref[bi].astype(F32)
            xs = _shift_rows(prev_ref.at[bi], x, 1)
            prev_ref[bi] = x[ch - prev_ref.shape[1]:]
            mixed.append(x + mu * (xs - x))
        return jnp.concatenate(mixed, axis=0)

    r = shift_mix(r_ref, pr_ref, mu_ref[0:1, :])
    k = shift_mix(k_ref, pk_ref, mu_ref[1:2, :])
    v = shift_mix(v_ref, pv_ref, mu_ref[2:3, :])
    tl = shift_mix(t_ref, pt_ref, mut_ref[...])

    ones_bd = _tri(gl, RWKV_HEAD) + _tri(gl, RWKV_HEAD, upper=True, strict=True)

    def head_sum(x):
        return jnp.concatenate([_mm(x[:, i * gl:(i + 1) * gl], ones_bd, pa=2) for i in range(WIDTH // gl)], axis=1)

    w_log = _log_sigmoid(w0_ref[...] + _mm(jnp.tanh(tl), wup_ref[...])) - 0.5
    log_w = -jnp.exp(w_log)
    a = _sigmoid(a0_ref[...] + _mm(tl, aup_ref[...]))
    g = _mm(_sigmoid(tl), gup_ref[...])
    kk = k * kk_ref[...]
    kk = kk / jnp.maximum(jnp.sqrt(head_sum(kk * kk)), 1e-12)
    k_rep = k * (1.0 + (a - 1.0) * ka_ref[...])
    av = -kk
    bv = kk * a
    bonus = head_sum(r * k_rep * rk_ref[...]) * v
    gc_all = _mm(_tri(lb, ch), log_w, pb=3)

    lane_head = _iota((1, gl), 1) // RWKV_HEAD
    masks = [jnp.where(lane_head == h, 1.0, 0.0).astype(BF16) for h in range(gh)]

    def stack(x):
        xb = x.astype(BF16)
        return jnp.concatenate([xb * m for m in masks], axis=0)

    def stack_t(x):
        return jnp.concatenate([x * m.astype(F32) for m in masks], axis=0).T

    s_lane = _iota((ch, gl), 1) % ch
    t_row = _iota((ch, gl), 0)
    strict = t_row > s_lane
    incl = t_row >= s_lane
    eye = jnp.where(t_row == s_lane, 1.0, 0.0)

    pairs = [(gi, bi) for gi in range(WIDTH // gl) for bi in range(nb)]
    pre = {}
    for gi, bi in pairs:
        ls = slice(gi * gl, (gi + 1) * gl)
        rs = slice(bi * ch, (bi + 1) * ch)
        gc = gc_all[rs, ls]
        g_last = gc[ch - 1:ch, :]
        e_inv = jnp.exp(-gc)
        e_end = jnp.exp(g_last - gc)
        r_t = (r[rs, ls] * jnp.exp(gc)).astype(BF16)
        a_t = (av[rs, ls] * jnp.exp(gc - log_w[rs, ls])).astype(BF16)
        scores = _mm(jnp.concatenate([a_t, r_t], axis=0),
                     jnp.concatenate([stack(bv[rs, ls] * e_inv), stack(k_rep[rs, ls] * e_inv)], axis=0), _NT)
        pre[gi, bi] = dict(
            r_t=r_t, a_bd=stack(a_t), v_bd=stack(v[rs, ls]), dec=jnp.exp(g_last),
            b_e=stack(bv[rs, ls] * e_end), k_e=stack(k_rep[rs, ls] * e_end),
            a_ab=jnp.where(strict, scores[:ch, :gl], 0.0),
            a_kv=jnp.concatenate([jnp.where(strict, scores[:ch, gl:], 0.0),
                                  jnp.where(incl, scores[ch:, gl:], 0.0)], axis=0).astype(BF16),
            a_rb=jnp.where(incl, scores[ch:, :gl], 0.0).astype(BF16))
    tinv = {key: eye + pre[key]["a_ab"] for key in pairs}
    power = {key: pre[key]["a_ab"].astype(BF16) for key in pairs}
    for key in pairs:
        power[key] = _mm(power[key], stack(power[key])).astype(BF16)
    step = 2
    while step < ch:
        for key in pairs:
            if 2 * step < ch:
                both = _mm(jnp.concatenate([power[key], tinv[key].astype(BF16)], axis=0), stack(power[key]))
                power[key] = both[:ch].astype(BF16)
                tinv[key] = tinv[key] + both[ch:]
            else:
                tinv[key] = tinv[key] + _mm(tinv[key], stack(power[key]))
        step *= 2

    from_v = {key: _mm(pre[key]["a_kv"], pre[key]["v_bd"]) for key in pairs}
    t_bf = {key: tinv[key].astype(BF16) for key in pairs}
    a_hat = {key: _mm(t_bf[key], pre[key]["a_bd"]).astype(BF16) for key in pairs}
    u0 = {key: _mm(t_bf[key], stack(from_v[key][:ch])) for key in pairs}
    grow = {key: _mm(pre[key]["v_bd"].astype(F32).T, pre[key]["k_e"]) for key in pairs}
    n_groups = WIDTH // gl
    state = {(gi, bi): st_ref[bi * n_groups + gi] for gi, bi in pairs}
    from_state = {key: _mm(jnp.concatenate([a_hat[key], pre[key]["r_t"]], axis=0), state[key], _NT) for key in pairs}
    u = {key: from_state[key][:ch] + u0[key] for key in pairs}
    for gi, bi in pairs:
        key = (gi, bi)
        y_ref[bi * ch:(bi + 1) * ch, gi * gl:(gi + 1) * gl] = (
            from_state[key][ch:] + _mm(pre[key]["a_rb"], stack(u[key])) + from_v[key][ch:])
    for gi, bi in pairs:
        key = (gi, bi)
        st_ref[bi * n_groups + gi] = (pre[key]["dec"] * state[key] + _mm(stack_t(u[key]), pre[key]["b_e"])
                                      + grow[key])

    y = y_ref[...]
    inv_n = 1.0 / RWKV_HEAD
    mu = head_sum(y) * inv_n
    yc = y - mu
    var = head_sum(yc * yc) * inv_n
    yn = yc * lax.rsqrt(var + RWKV_GN_EPS) * ng_ref[...]
    out_ref[...] = ((yn + bonus) * g).astype(out_ref.dtype).reshape(nb, ch, WIDTH)


def _rwkv(z, layer, mu_rkv, mu_t, w0, w_up, a0, a_up, g_up, k_k, k_a, r_k, norm_g):
    bsz, t, _ = z.shape
    ch = RWKV_CHUNK
    gl = RWKV_GROUP_LANES
    vec = _layer_spec(layer, 1, WIDTH)
    mat = _layer_spec(layer, RWKV_TAIL, WIDTH)
    return pl.pallas_call(
        _rwkv_kernel,
        grid=(t // ch,),
        in_specs=[
            pl.BlockSpec((bsz, ch, WIDTH), lambda c: (0, c, ZC_DR // WIDTH)),
            pl.BlockSpec((bsz, ch, WIDTH), lambda c: (0, c, ZC_DK // WIDTH)),
            pl.BlockSpec((bsz, ch, WIDTH), lambda c: (0, c, ZC_DV // WIDTH)),
            pl.BlockSpec((bsz, ch, RWKV_TAIL), lambda c: (0, c, ZC_DT // RWKV_TAIL)),
            _layer_spec(layer, 3, WIDTH), _layer_spec(layer, 1, RWKV_TAIL),
            vec, mat, vec, mat, mat, vec, vec, vec, vec,
        ],
        out_specs=pl.BlockSpec((bsz, ch, WIDTH), lambda c: (0, c, 0)),
        out_shape=jax.ShapeDtypeStruct((bsz, t, WIDTH), BF16),
        scratch_shapes=[
            pltpu.VMEM((bsz, SUBLANES, WIDTH), F32), pltpu.VMEM((bsz, SUBLANES, WIDTH), F32),
            pltpu.VMEM((bsz, SUBLANES, WIDTH), F32), pltpu.VMEM((bsz, SUBLANES, RWKV_TAIL), F32),
            pltpu.VMEM((bsz * (WIDTH // gl), gl, gl), F32),
            pltpu.VMEM((bsz * ch, WIDTH), F32),
        ],
        compiler_params=pltpu.CompilerParams(
            dimension_semantics=("arbitrary",), vmem_limit_bytes=VMEM_LIMIT),
        name="rwkv",
    )(z, z, z, z, mu_rkv, mu_t, w0, w_up, a0, a_up, g_up, k_k, k_a, r_k, norm_g)


def _merge_kernel(x_ref, ya_ref, zb_ref, yc_ref, yd_ref, g0_ref, g1_ref, g2_ref, g3_ref, gb_ref,
                  pa_ref, w1_ref, w2_ref, pc_ref, pd_ref, wo_ref, out_ref):
    d = D_MODEL

    def gate(g_ref, i):
        return _sigmoid(g_ref[...].astype(F32) + gb_ref[:, i * d:(i + 1) * d])

    zb = zb_ref[...]
    merged = gate(g0_ref, 0) * _mm(ya_ref[...], pa_ref[...])
    merged = merged + gate(g1_ref, 1) * (_mm(zb, w1_ref[...]) * _sigmoid(_mm(zb, w2_ref[...])))
    merged = merged + gate(g2_ref, 2) * _mm(yc_ref[...], pc_ref[...])
    merged = merged + gate(g3_ref, 3) * _mm(yd_ref[...], pd_ref[...])
    out_ref[...] = x_ref[...] + _mm(merged, wo_ref[...])


def _merge(x, z, ya, zb, yc, yd, layer, gate_bias, pa, w1, w2, pc, pd, wo, *, tm=512):
    n, d = x.shape
    tile = lambda w: pl.BlockSpec((tm, w), lambda i: (i, 0))
    gate = lambda b: pl.BlockSpec((tm, d), lambda i: (i, ZC_GATE // d + b))
    resident = lambda rows: pl.BlockSpec((None, rows, d), lambda i: (layer, 0, 0), pipeline_mode=pl.Buffered(1))
    proj = resident(WIDTH)
    return pl.pallas_call(
        _merge_kernel,
        grid=(n // tm,),
        in_specs=[tile(d), tile(WIDTH), tile(WIDTH), tile(WIDTH), tile(WIDTH),
                  gate(0), gate(1), gate(2), gate(3), _layer_spec(layer, 1, N_BRANCH * d),
                  proj, proj, proj, proj, proj, resident(d)],
        out_specs=tile(d),
        out_shape=jax.ShapeDtypeStruct((n, d), F32),
        compiler_params=pltpu.CompilerParams(
            dimension_semantics=("parallel",), vmem_limit_bytes=VMEM_LIMIT),
        name="merge",
    )(x, ya, zb, yc, yd, z, z, z, z, gate_bias, pa, w1, w2, pc, pd, wo)


def _xattn_kernel(x_ref, g_ref, wq_ref, kv_ref, wo_ref, out_ref):
    d = D_MODEL
    hd = XATTN_HEAD_DIM
    x = x_ref[...]
    q = _mm(_rms(x, g_ref[...]), wq_ref[...])
    kv = kv_ref[...]
    heads = range(XATTN_HEADS)
    sls = [slice(h * hd, (h + 1) * hd) for h in heads]
    qb = q.astype(BF16)
    s = [_mm(qb[:, sls[h]], kv[:, sls[h]], _NT) * hd ** -0.5 for h in heads]
    p = [jnp.exp(s[h] - jnp.max(s[h], axis=-1, keepdims=True)) for h in heads]
    p = [p[h] / jnp.sum(p[h], axis=-1, keepdims=True) for h in heads]
    o = jnp.concatenate([_mm(p[h], kv[:, d + h * hd:d + (h + 1) * hd]) for h in heads], axis=1)
    out_ref[...] = x + _mm(o, wo_ref[...])


def _xattn(x, bsz, layer, gain, wq, kv, wo, *, tm=1024):
    n, d = x.shape
    nblk = n // bsz // tm
    m_len = kv.shape[0] // bsz
    return pl.pallas_call(
        _xattn_kernel,
        grid=(bsz, nblk),
        in_specs=[
            pl.BlockSpec((tm, d), lambda b, c: (b * nblk + c, 0)),
            _layer_spec(layer, 1, d),
            _layer_spec(layer, d, d),
            pl.BlockSpec((m_len, 2 * d), lambda b, c: (b, 0)),
            _layer_spec(layer, d, d),
        ],
        out_specs=pl.BlockSpec((tm, d), lambda b, c: (b * nblk + c, 0)),
        out_shape=jax.ShapeDtypeStruct((n, d), F32),
        compiler_params=pltpu.CompilerParams(
            dimension_semantics=("parallel", "parallel"), vmem_limit_bytes=VMEM_LIMIT),
        name="xattn",
    )(x, gain, wq, kv, wo)


def _w_in_segments():
    o = np.cumsum((0, WIDTH, WIDTH, MLSTM_HEADS, MLSTM_HEADS, WIDTH, GLA_KEY_WIDTH, GLA_KEY_WIDTH, WIDTH, WIDTH,
                   GLA_GATE_RANK, 3 * WIDTH + RWKV_TAIL, N_BRANCH * D_MODEL))
    a_u, a_o, a_i, a_f, b_u, c_q, c_k, c_v, c_g, c_a, d_z, gate = range(12)
    seg = lambda i: (int(o[i]), int(o[i + 1] - o[i]))
    d_parts = [(int(o[d_z]) + i * WIDTH, WIDTH) for i in range(3)] + [(int(o[d_z]) + 3 * WIDTH, RWKV_TAIL)]
    return ([seg(gate), seg(a_u), seg(a_o), seg(b_u), seg(c_v), seg(c_g)] + d_parts[:3]
            + [seg(c_q), seg(c_k), d_parts[3], seg(a_i), seg(a_f), seg(c_a)])


def _regroup_kernel(w_ref, o_ref):
    lanes = w_ref.shape[1]
    segments = _w_in_segments()
    big = [s for s in segments if s[1] % SMALL == 0]
    dst = 0
    for src, width in big:
        o_ref[dst:dst + width, :] = w_ref[src:src + width, :].astype(o_ref.dtype)
        dst += width
    small = [w_ref[src:src + width, :] for src, width in segments if width % SMALL]
    used = sum(s.shape[0] for s in small)
    small.append(jnp.zeros((o_ref.shape[0] - dst - used, lanes), F32))
    o_ref[dst:, :] = jnp.concatenate(small, axis=0).astype(o_ref.dtype)


def _regroup_w_in(w_t, *, lanes=256):
    depth, n_in, d = w_t.shape
    return pl.pallas_call(
        _regroup_kernel,
        grid=(depth, d // lanes),
        in_specs=[pl.BlockSpec((None, n_in, lanes), lambda l, i: (l, 0, i))],
        out_specs=pl.BlockSpec((None, Z_COLS, lanes), lambda l, i: (l, 0, i)),
        out_shape=jax.ShapeDtypeStruct((depth, Z_COLS, d), BF16),
        compiler_params=pltpu.CompilerParams(
            dimension_semantics=("parallel", "parallel"), vmem_limit_bytes=VMEM_LIMIT),
        name="regroup_w_in",
    )(w_t)


def _s5_tables(a_re, a_im, log_step, b_re, b_im, c_re, c_im, d_skip, bsz):
    depth = a_re.shape[0]
    step = jnp.exp(log_step)[..., None]
    lam_re = jnp.minimum(a_re, -S5_MIN_NEG)
    lam_im = a_im
    mag = jnp.exp(lam_re * step)
    bar_re = mag * jnp.cos(lam_im * step)
    bar_im = mag * jnp.sin(lam_im * step)
    denom = lam_re * lam_re + lam_im * lam_im
    coef_re = ((bar_re - 1.0) * lam_re + bar_im * lam_im) / denom
    coef_im = (bar_im * lam_re - (bar_re - 1.0) * lam_im) / denom
    bb_re = coef_re[..., None] * b_re - coef_im[..., None] * b_im
    bb_im = coef_re[..., None] * b_im + coef_im[..., None] * b_re
    nb = S5_HALF // S5_BLOCK_CH
    gpb = S5_BLOCK_GROUPS
    eye = jnp.eye(gpb, dtype=F32)

    def in_block(bb):
        bb = bb.reshape(depth, 2, nb, gpb, S5_STATE, S5_GROUP)
        return jnp.einsum('lhjgpc,gk->lhjgckp', bb, eye).reshape(depth, 2, nb, S5_BLOCK_CH, S5_BLOCK_STATES)

    def out_block(cc):
        cc = cc.reshape(depth, 2, nb, gpb, S5_GROUP, S5_STATE)
        return jnp.einsum('lhjgcp,gk->lhjgpkc', cc, eye).reshape(depth, 2, nb, S5_BLOCK_STATES, S5_BLOCK_CH)

    w_in = jnp.concatenate([in_block(bb_re), in_block(bb_im)], axis=-1)
    w_in = w_in.transpose(0, 2, 1, 3, 4).reshape(depth, nb, 2 * S5_BLOCK_CH, 2 * S5_BLOCK_STATES)
    w_out = jnp.concatenate([out_block(c_re), -out_block(c_im)], axis=-2)
    w_out = w_out.transpose(0, 2, 3, 1, 4).reshape(depth, nb, 2 * S5_BLOCK_STATES, 2 * S5_BLOCK_CH)

    def rows(t, width):
        t = t.reshape(depth, 2, 1, width)
        return jnp.broadcast_to(t, (depth, 2, bsz, width)).reshape(depth, 2 * bsz, width)

    nst = nb * S5_BLOCK_STATES
    lam_re_rows = rows(bar_re.reshape(depth, 2 * nst), nst)
    lam_im_rows = rows(bar_im.reshape(depth, 2 * nst), nst)
    d_tab = jnp.tile(rows(d_skip, S5_HALF), (1, S5_STEPS, 1))
    return w_in.astype(BF16), w_out.astype(BF16), lam_re_rows, lam_im_rows, d_tab


def _pad_rows(w, start, total):
    return jnp.pad(w, ((0, 0), (start, total - start - w.shape[1]), (0, 0)))


def kernel(x, mem, ffn1_norm, ffn1_w_gate, ffn1_w_up, ffn1_w_down, mix_norm, w_in, gate_bias, mlstm_conv, mlstm_wq, mlstm_wk, mlstm_wv, mlstm_b_i, mlstm_b_f, mlstm_norm, mlstm_proj, s5_a_re, s5_a_im, s5_log_step, s5_b_re, s5_b_im, s5_c_re, s5_c_im, s5_d, s5_glu_w1, s5_glu_w2, gla_a_up, gla_a_bias, gla_norm, gla_proj, rwkv_mu, rwkv_w0, rwkv_w_up, rwkv_a0, rwkv_a_up, rwkv_g_up, rwkv_k_k, rwkv_k_a, rwkv_r_k, rwkv_norm, rwkv_proj, w_out, xattn_norm, mem_norm, xattn_wq, xattn_wk, xattn_wv, xattn_wo, ffn2_norm, ffn2_w_gate, ffn2_w_up, ffn2_w_down, final_norm):
    bsz, t, d = x.shape
    depth = w_in.shape[0]
    assert d == D_MODEL and t % TIME_BLOCK == 0 and 2 * bsz == SUBLANES
    xs = x.reshape(bsz * t, d)
    mems = mem.reshape(bsz * mem.shape[1], d)
    bf = lambda w: w.astype(BF16)
    vec = lambda v: v.reshape(depth, 1, -1)
    fin = final_norm.reshape(1, d)

    w_in_r = _regroup_w_in(jnp.swapaxes(w_in, 1, 2))
    gate_b = jnp.pad(jnp.concatenate([mlstm_b_i, mlstm_b_f], axis=1), ((0, 0), (0, SMALL - 2 * MLSTM_HEADS)))
    s5_w_in, s5_w_out, s5_lre, s5_lim, s5_dtab = _s5_tables(
        s5_a_re, s5_a_im, s5_log_step, s5_b_re, s5_b_im, s5_c_re, s5_c_im, s5_d, bsz)
    gla_up = _pad_rows(gla_a_up, 2 * MLSTM_HEADS, SMALL)
    mu_rkv = rwkv_mu[:, :3 * WIDTH].reshape(depth, 3, WIDTH)
    mu_t = vec(rwkv_mu[:, 3 * WIDTH:])
    rw_wup = bf(_pad_rows(rwkv_w_up, 0, RWKV_TAIL))
    rw_aup = bf(_pad_rows(rwkv_a_up, RWKV_DECAY_RANK, RWKV_TAIL))
    rw_gup = bf(_pad_rows(rwkv_g_up, RWKV_DECAY_RANK + RWKV_ICLR_RANK, RWKV_TAIL))
    w_kv = bf(jnp.concatenate([xattn_wk, xattn_wv], axis=2))
    wq_a, wk_a, wv_a = bf(mlstm_wq), bf(mlstm_wk), bf(mlstm_wv)
    p_a, p_b1, p_b2, p_c, p_d, p_o = (bf(mlstm_proj), bf(s5_glu_w1), bf(s5_glu_w2), bf(gla_proj), bf(rwkv_proj),
                                      bf(w_out))
    x_wq, x_wo = bf(xattn_wq), bf(xattn_wo)
    f1_g, f1_u, f1_d = bf(ffn1_w_gate), bf(ffn1_w_up), bf(ffn1_w_down)
    f2_g, f2_u, f2_d = bf(ffn2_w_gate), bf(ffn2_w_up), bf(ffn2_w_down)

    for l in range(depth):
        xs = _ffn(xs, vec(ffn1_norm), f1_g, f1_u, f1_d, fin, layer=l, final=False)

        z = _in_proj(xs, vec(mix_norm), w_in_r, layer=l)

        z3 = z.reshape(bsz, t, Z_COLS)
        y_a = _mlstm(z3, l, mlstm_conv, wq_a, wk_a, wv_a, vec(gate_b), gate_b.reshape(depth, SMALL, 1),
                     vec(mlstm_norm)).reshape(bsz * t, WIDTH)

        z_b = _s5(z3, bsz, l, s5_w_in, s5_w_out, s5_lre, s5_lim, s5_dtab).reshape(bsz * t, WIDTH)

        y_c = _gla(z3, l, gla_up, vec(gla_a_bias), vec(gla_norm)).reshape(bsz * t, WIDTH)

        y_d = _rwkv(z3, l, mu_rkv, mu_t, vec(rwkv_w0), rw_wup, vec(rwkv_a0), rw_aup, rw_gup,
                    vec(rwkv_k_k), vec(rwkv_k_a), vec(rwkv_r_k), vec(rwkv_norm)).reshape(bsz * t, WIDTH)

        xs = _merge(xs, z, y_a, z_b, y_c, y_d, l, vec(gate_bias), p_a, p_b1, p_b2, p_c, p_d, p_o)

        kv = _norm_matmul(mems, vec(mem_norm), w_kv, layer=l, tm=mems.shape[0], tn=512, name="mem_kv",
                          out_dtype=BF16)
        xs = _xattn(xs, bsz, l, vec(xattn_norm), x_wq, kv, x_wo)

        xs = _ffn(xs, vec(ffn2_norm), f2_g, f2_u, f2_d, fin, layer=l,
                  final=(l == depth - 1))
    return xs.reshape(bsz, t, d)
```

```python
import functools
import math

import jax
import jax.numpy as jnp
import numpy as np
from jax import lax
from jax.experimental import pallas as pl
from jax.experimental.pallas import tpu as pltpu

F32 = jnp.float32
BF16 = jnp.bfloat16

D_MODEL = 1024
D_FF = 11 * D_MODEL // 4
NORM_EPS = 1e-6
FFN_HALF = 0.5
N_BRANCH = 4
WIDTH = D_MODEL // 2

MLSTM_HEADS = 4
MLSTM_HEAD_DIM = WIDTH // MLSTM_HEADS
MLSTM_CONV = 4
MLSTM_CHUNK = 256

S5_GROUP = 16
S5_GROUPS = WIDTH // S5_GROUP
S5_STATE = 64
S5_MIN_NEG = 1e-4
S5_HALF = WIDTH // 2
S5_BLOCK_GROUPS = 8
S5_BLOCK_CH = S5_BLOCK_GROUPS * S5_GROUP
S5_BLOCK_STATES = S5_BLOCK_GROUPS * S5_STATE
S5_STEPS = 64
SUBLANES = 8

GLA_HEADS = 4
GLA_KEY_WIDTH = WIDTH // 2
GLA_HEAD_K = GLA_KEY_WIDTH // GLA_HEADS
GLA_HEAD_V = WIDTH // GLA_HEADS
GLA_GATE_RANK = 16
GLA_GATE_TAU = 16.0
GLA_CHUNK = 64

RWKV_HEAD = 64
RWKV_HEADS = WIDTH // RWKV_HEAD
RWKV_DECAY_RANK = 64
RWKV_ICLR_RANK = 64
RWKV_GATE_RANK = 128
RWKV_TAIL = RWKV_DECAY_RANK + RWKV_ICLR_RANK + RWKV_GATE_RANK
RWKV_GN_EPS = 64e-5
RWKV_CHUNK = 64
RWKV_GROUP_LANES = 256
RWKV_GROUP_HEADS = RWKV_GROUP_LANES // RWKV_HEAD

XATTN_HEADS = 4
XATTN_HEAD_DIM = D_MODEL // XATTN_HEADS

ZC_GATE = 0
ZC_AU = 4096
ZC_AO = 4608
ZC_BU = 5120
ZC_CV = 5632
ZC_CG = 6144
ZC_DR = 6656
ZC_DK = 7168
ZC_DV = 7680
ZC_CQ = 8192
ZC_CK = 8448
ZC_DT = 8704
ZC_SM = 8960
Z_COLS = 9216
SMALL = 128

TIME_BLOCK = 256
VMEM_LIMIT = 56 * 1024 * 1024

_NN = (((1,), (0,)), ((), ()))
_NT = (((1,), (1,)), ((), ()))


def _split(x, n):
    parts = []
    rest = x
    for i in range(n):
        p = rest.astype(BF16)
        parts.append(p)
        if i + 1 < n:
            rest = rest - p.astype(F32)
    return parts


def _mm(a, b, dn=_NN, pa=1, pb=1):
    ap, bp = _split(a, pa), _split(b, pb)
    order = max(pa, pb)
    out = None
    for i, x in enumerate(ap):
        for j, y in enumerate(bp):
            if i + j < order:
                t = lax.dot_general(x, y, dn, preferred_element_type=F32)
                out = t if out is None else out + t
    return out


def _rms(x, g):
    return x * lax.rsqrt(jnp.mean(x * x, axis=-1, keepdims=True) + NORM_EPS) * g


def _sigmoid(x):
    return 0.5 * jnp.tanh(0.5 * x) + 0.5


def _log_sigmoid(x):
    return jnp.minimum(x, 0.0) - jnp.log(1.0 + jnp.exp(-jnp.abs(x)))


def _iota(shape, dim):
    return lax.broadcasted_iota(jnp.int32, shape, dim)


def _tri(n, block, upper=False, strict=False):
    r, c = _iota((n, n), 0), _iota((n, n), 1)
    same = (r // block) == (c // block)
    if upper:
        keep = (r < c) if strict else (r <= c)
    else:
        keep = (r > c) if strict else (r >= c)
    return jnp.where(same & keep, 1.0, 0.0).astype(BF16)


def _shift_rows(prev_ref, x, k):
    ext = jnp.concatenate([prev_ref[...], x], axis=0)
    return pltpu.roll(ext, k, 0)[prev_ref.shape[0]:]


def _layer_spec(layer, *shape):
    return pl.BlockSpec((None,) + shape, lambda *_: (layer,) + (0,) * len(shape))


def _ffn_kernel(x_ref, g_ref, wg_ref, wu_ref, wd_ref, fg_ref, o_ref, *, final, tf):
    x = x_ref[...]
    xn = _rms(x, g_ref[...]).astype(BF16)
    acc = None
    for j in range(wg_ref.shape[1] // tf):
        cols = slice(j * tf, (j + 1) * tf)
        gate = _mm(xn, wg_ref[:, cols])
        up = _mm(xn, wu_ref[:, cols])
        part = _mm(gate * _sigmoid(gate) * up, wd_ref[cols, :])
        acc = part if acc is None else acc + part
    y = x + FFN_HALF * acc
    if final:
        y = _rms(y, fg_ref[...])
    o_ref[...] = y


def _ffn(x, gain, wg, wu, wd, final_gain, *, layer, final, tm=512, tf=256):
    n, d = x.shape
    f = wg.shape[2]
    resident = lambda *shape: pl.BlockSpec((None,) + shape, lambda i: (layer, 0, 0), pipeline_mode=pl.Buffered(1))
    return pl.pallas_call(
        functools.partial(_ffn_kernel, final=final, tf=tf),
        grid=(n // tm,),
        in_specs=[
            pl.BlockSpec((tm, d), lambda i: (i, 0)),
            _layer_spec(layer, 1, d),
            resident(d, f), resident(d, f), resident(f, d),
            pl.BlockSpec((1, d), lambda i: (0, 0)),
        ],
        out_specs=pl.BlockSpec((tm, d), lambda i: (i, 0)),
        out_shape=jax.ShapeDtypeStruct((n, d), F32),
        compiler_params=pltpu.CompilerParams(
            dimension_semantics=("parallel",), vmem_limit_bytes=VMEM_LIMIT),
        name="ffn",
    )(x, gain, wg, wu, wd, final_gain)


def _in_proj_kernel(x_ref, g_ref, w_ref, o_ref, *, tn):
    xn = _rms(x_ref[...], g_ref[...]).astype(BF16)
    for j in range(w_ref.shape[0] // tn):
        cols = slice(j * tn, (j + 1) * tn)
        o_ref[:, cols] = _mm(xn, w_ref[cols, :], _NT).astype(o_ref.dtype)


def _in_proj(x, gain, w, *, layer, tm=512, tn=1536):
    n, d = x.shape
    m = w.shape[1]
    return pl.pallas_call(
        functools.partial(_in_proj_kernel, tn=tn),
        grid=(n // tm,),
        in_specs=[
            pl.BlockSpec((tm, d), lambda i: (i, 0)),
            _layer_spec(layer, 1, d),
            pl.BlockSpec((None, m, d), lambda i: (layer, 0, 0), pipeline_mode=pl.Buffered(1)),
        ],
        out_specs=pl.BlockSpec((tm, m), lambda i: (i, 0)),
        out_shape=jax.ShapeDtypeStruct((n, m), BF16),
        compiler_params=pltpu.CompilerParams(
            dimension_semantics=("parallel",), vmem_limit_bytes=VMEM_LIMIT),
        name="in_proj",
    )(x, gain, w)


def _norm_matmul_kernel(x_ref, g_ref, w_ref, o_ref, xn_ref):
    @pl.when(pl.program_id(1) == 0)
    def _():
        xn_ref[...] = _rms(x_ref[...], g_ref[...]).astype(BF16)

    o_ref[...] = _mm(xn_ref[...], w_ref[...]).astype(o_ref.dtype)


def _norm_matmul(x, gain, w, *, layer, tm, tn, name, out_dtype):
    n, d = x.shape
    m = w.shape[2]
    return pl.pallas_call(
        _norm_matmul_kernel,
        grid=(n // tm, m // tn),
        in_specs=[
            pl.BlockSpec((tm, d), lambda i, j: (i, 0)),
            _layer_spec(layer, 1, d),
            pl.BlockSpec((None, d, tn), lambda i, j: (layer, 0, j)),
        ],
        out_specs=pl.BlockSpec((tm, tn), lambda i, j: (i, j)),
        out_shape=jax.ShapeDtypeStruct((n, m), out_dtype),
        scratch_shapes=[pltpu.VMEM((tm, d), BF16)],
        compiler_params=pltpu.CompilerParams(
            dimension_semantics=("parallel", "arbitrary"), vmem_limit_bytes=VMEM_LIMIT),
        name=name,
    )(x, gain, w)


def _mlstm_kernel(u_ref, o_ref, sm_ref, conv_ref, wq_ref, wk_ref, wv_ref, brow_ref, bcol_ref, ng_ref,
                  out_ref, uprev_ref, c_ref, m_ref):
    nb, lb = u_ref.shape[0], u_ref.shape[1]
    e = MLSTM_HEAD_DIM
    nh = MLSTM_HEADS

    @pl.when(pl.program_id(0) == 0)
    def _():
        uprev_ref[...] = jnp.zeros_like(uprev_ref)
        c_ref[...] = jnp.zeros_like(c_ref)
        m_ref[...] = jnp.zeros_like(m_ref)

    u, uc, pre, pre_t, b_cols, b_rows = [], [], [], [], [], []
    lower, upper = _tri(lb, lb), _tri(lb, lb, upper=True)
    for bi in range(nb):
        ub = u_ref[bi].astype(F32)
        conv = None
        for j in range(MLSTM_CONV):
            k = MLSTM_CONV - 1 - j
            term = conv_ref[j:j + 1, :] * (_shift_rows(uprev_ref.at[bi], ub, k) if k else ub)
            conv = term if conv is None else conv + term
        uprev_ref[bi] = ub[lb - uprev_ref.shape[1]:]
        u.append(ub)
        uc.append(conv * _sigmoid(conv))
        sm = sm_ref[bi].astype(F32)
        pre.append(sm + brow_ref[...])
        pre_t.append(sm.T[0:8, :] + bcol_ref[0:8, :])
        b_cols.append(_mm(lower, _log_sigmoid(pre[bi]), pb=3))
        b_rows.append(_mm(_log_sigmoid(pre_t[bi]), upper, pa=3))

    causal = _iota((lb, lb), 0) >= _iota((lb, lb), 1)
    streams = [(bi, h) for bi in range(nb) for h in range(nh)]
    sls = [slice(h * e, (h + 1) * e) for h in range(nh)]
    ones = jnp.ones((lb, e), BF16)
    q = {s: _mm(uc[s[0]][:, sls[s[1]]], wq_ref[s[1]]).astype(BF16) for s in streams}
    k = {s: _mm(uc[s[0]][:, sls[s[1]]], wk_ref[s[1]]) * e ** -0.5 for s in streams}
    v = {s: jnp.concatenate([_mm(u[s[0]][:, sls[s[1]]], wv_ref[s[1]]).astype(BF16), ones], axis=1) for s in streams}
    bc = {(bi, h): b_cols[bi][:, nh + h:nh + h + 1] for bi, h in streams}
    m_st = {(bi, h): m_ref[bi * SUBLANES + h:bi * SUBLANES + h + 1, 0:1] for bi, h in streams}
    dmat = {(bi, h): jnp.where(causal, bc[bi, h] - b_rows[bi][nh + h:nh + h + 1, :] + pre_t[bi][h:h + 1, :], -jnp.inf)
            for bi, h in streams}
    m_inter = {s: bc[s] + m_st[s] for s in streams}
    m_row = {s: jnp.maximum(m_inter[s], jnp.max(dmat[s], axis=-1, keepdims=True)) for s in streams}
    qk = {s: _mm(q[s], k[s], _NT) for s in streams}
    w = {s: jnp.exp(dmat[s] - m_row[s]) * qk[s] for s in streams}
    from_state = {(bi, h): _mm(q[bi, h], c_ref[bi * nh + h]) for bi, h in streams}
    from_chunk = {s: _mm(w[s], v[s]) for s in streams}
    for bi, h in streams:
        s = (bi, h)
        tot = jnp.exp(m_inter[s] - m_row[s]) * from_state[s] + from_chunk[s]
        hh = tot[:, :e] / jnp.maximum(jnp.abs(tot[:, e:e + 1]), jnp.exp(-m_row[s]))
        hn = hh * lax.rsqrt(jnp.mean(hh * hh, axis=-1, keepdims=True) + NORM_EPS) * ng_ref[:, sls[h]]
        out_ref[bi, :, sls[h]] = (hn * _sigmoid(o_ref[bi, :, sls[h]].astype(F32))).astype(out_ref.dtype)
    for bi, h in streams:
        s = (bi, h)
        b_last = bc[s][lb - 1:lb, :]
        e_col = b_last - bc[s] + pre[bi][:, h:h + 1]
        m_new = jnp.maximum(b_last + m_st[s], jnp.max(e_col, axis=0, keepdims=True))
        decay = jnp.exp(b_last + m_st[s] - m_new)
        wk = jnp.exp(e_col - m_new) * k[s]
        c_ref[bi * nh + h] = decay * c_ref[bi * nh + h] + _mm(wk.T, v[s])
        m_ref[bi * SUBLANES + h:bi * SUBLANES + h + 1, :] = jnp.broadcast_to(m_new, (1, m_ref.shape[1]))


def _mlstm(z, layer, conv_w, wq, wk, wv, b_row, b_col, norm_g):
    bsz, t, _ = z.shape
    lb = MLSTM_CHUNK
    hd = MLSTM_HEAD_DIM
    return pl.pallas_call(
        _mlstm_kernel,
        grid=(t // lb,),
        in_specs=[
            pl.BlockSpec((bsz, lb, WIDTH), lambda c: (0, c, ZC_AU // WIDTH)),
            pl.BlockSpec((bsz, lb, WIDTH), lambda c: (0, c, ZC_AO // WIDTH)),
            pl.BlockSpec((bsz, lb, SMALL), lambda c: (0, c, ZC_SM // SMALL)),
            _layer_spec(layer, MLSTM_CONV, WIDTH),
            _layer_spec(layer, MLSTM_HEADS, hd, hd),
            _layer_spec(layer, MLSTM_HEADS, hd, hd),
            _layer_spec(layer, MLSTM_HEADS, hd, hd),
            _layer_spec(layer, 1, SMALL),
            _layer_spec(layer, SMALL, 1),
            _layer_spec(layer, 1, WIDTH),
        ],
        out_specs=pl.BlockSpec((bsz, lb, WIDTH), lambda c: (0, c, 0)),
        out_shape=jax.ShapeDtypeStruct((bsz, t, WIDTH), BF16),
        scratch_shapes=[
            pltpu.VMEM((bsz, SUBLANES, WIDTH), F32),
            pltpu.VMEM((bsz * MLSTM_HEADS, hd, 2 * hd), F32),
            pltpu.VMEM((bsz * SUBLANES, hd), F32),
        ],
        compiler_params=pltpu.CompilerParams(
            dimension_semantics=("arbitrary",), vmem_limit_bytes=VMEM_LIMIT),
        name="mlstm",
    )(z, z, z, conv_w, wq, wk, wv, b_row, b_col, norm_g)


def _s5_kernel(u_ref, pin_ref, pout_ref, win_ref, wout_ref, lre_ref, lim_ref, d_ref, out_ref, x_ref, c_ref, *, bsz):
    steps = u_ref.shape[1]
    rows = steps * SUBLANES
    ns = S5_BLOCK_STATES
    cw = S5_BLOCK_CH
    nblk = S5_HALF // cw

    @pl.when(pl.program_id(0) == 0)
    def _():
        c_ref[...] = jnp.zeros_like(c_ref)

    u_bt = u_ref[...].reshape(bsz * steps, WIDTH)
    u = _mm(pin_ref[0], u_bt[:, :S5_HALF]) + _mm(pin_ref[1], u_bt[:, S5_HALF:])
    upper = (_iota((rows, 1), 0) % SUBLANES) >= bsz
    for j in range(nblk):
        uj = u[:, j * cw:(j + 1) * cw]
        uext = jnp.concatenate([jnp.where(upper, 0.0, uj), jnp.where(upper, uj, 0.0)], axis=1)
        x_ref[:, j * 2 * ns:(j + 1) * 2 * ns] = _mm(uext, win_ref[j])

    lre, lim = lre_ref[...], lim_ref[...]

    ys = []
    for j in range(nblk):
        lr, li = lre[:, j * ns:(j + 1) * ns], lim[:, j * ns:(j + 1) * ns]
        re_cols = slice(j * 2 * ns, j * 2 * ns + ns)
        im_cols = slice(j * 2 * ns + ns, (j + 1) * 2 * ns)
        sr, si = c_ref[:, re_cols], c_ref[:, im_cols]
        for t in range(steps):
            rows_t = slice(t * SUBLANES, (t + 1) * SUBLANES)
            sr, si = (lr * sr - li * si + x_ref[rows_t, re_cols],
                      lr * si + li * sr + x_ref[rows_t, im_cols])
            x_ref[rows_t, re_cols] = sr
            x_ref[rows_t, im_cols] = si
        c_ref[:, re_cols] = sr
        c_ref[:, im_cols] = si
        res = _mm(x_ref[:, j * 2 * ns:(j + 1) * 2 * ns], wout_ref[j])
        cols = slice(j * cw, (j + 1) * cw)
        y = jnp.where(upper, res[:, cw:], res[:, :cw]) + d_ref[:, cols] * u[:, cols]
        ys.append(0.5 * y * (1.0 + jnp.tanh(math.sqrt(2.0 / math.pi) * (y + 0.044715 * y * y * y))))
    y = jnp.concatenate(ys, axis=1).astype(BF16)
    out = jnp.concatenate([_mm(pout_ref[0], y), _mm(pout_ref[1], y)], axis=1)
    out_ref[...] = out.astype(out_ref.dtype).reshape(bsz, steps, WIDTH)


def _s5_permutations(bsz, steps):
    rows = steps * 2 * bsz
    r = np.arange(rows)
    p = np.zeros((2, rows, bsz * steps), np.float32)
    p[(r % (2 * bsz)) // bsz, r, (r % bsz) * steps + r // (2 * bsz)] = 1.0
    return jnp.asarray(p, BF16), jnp.asarray(p.transpose(0, 2, 1), BF16)


def _s5(z, bsz, layer, w_in, w_out, lam_re, lam_im, d_tab):
    t = z.shape[1]
    steps = S5_STEPS
    rows = steps * SUBLANES
    nblk = S5_HALF // S5_BLOCK_CH
    nst = nblk * S5_BLOCK_STATES
    p_in, p_out = _s5_permutations(bsz, steps)
    whole = lambda *shape: pl.BlockSpec(shape, lambda i: (0,) * len(shape))
    return pl.pallas_call(
        functools.partial(_s5_kernel, bsz=bsz),
        grid=(t // steps,),
        in_specs=[
            pl.BlockSpec((bsz, steps, WIDTH), lambda i: (0, i, ZC_BU // WIDTH)),
            whole(2, rows, bsz * steps),
            whole(2, bsz * steps, rows),
            _layer_spec(layer, nblk, 2 * S5_BLOCK_CH, 2 * S5_BLOCK_STATES),
            _layer_spec(layer, nblk, 2 * S5_BLOCK_STATES, 2 * S5_BLOCK_CH),
            _layer_spec(layer, SUBLANES, nst),
            _layer_spec(layer, SUBLANES, nst),
            _layer_spec(layer, rows, S5_HALF),
        ],
        out_specs=pl.BlockSpec((bsz, steps, WIDTH), lambda i: (0, i, 0)),
        out_shape=jax.ShapeDtypeStruct((bsz, t, WIDTH), BF16),
        scratch_shapes=[pltpu.VMEM((rows, 2 * nst), F32), pltpu.VMEM((SUBLANES, 2 * nst), F32)],
        compiler_params=pltpu.CompilerParams(
            dimension_semantics=("arbitrary",), vmem_limit_bytes=VMEM_LIMIT),
        name="s5",
    )(z, p_in, p_out, w_in, w_out, lam_re, lam_im, d_tab)


def _gla_kernel(q_ref, k_ref, v_ref, g_ref, sm_ref, aup_ref, ab_ref, ng_ref, out_ref, st_ref):
    nb, lb = q_ref.shape[0], q_ref.shape[1]
    ch = GLA_CHUNK
    kw = GLA_KEY_WIDTH

    @pl.when(pl.program_id(0) == 0)
    def _():
        st_ref[...] = jnp.zeros_like(st_ref)

    heads = range(GLA_HEADS)
    key_head = _iota((1, kw), 1) // GLA_HEAD_K
    val_head = _iota((1, WIDTH), 1) // GLA_HEAD_V
    key_masks = [jnp.where(key_head == h, 1.0, 0.0).astype(BF16) for h in heads]
    val_masks = [jnp.where(val_head == h, 1.0, 0.0).astype(BF16) for h in heads]

    def stack(x, masks):
        return jnp.concatenate([x * m for m in masks], axis=0)

    causal = _iota((ch, GLA_HEADS * ch), 0) >= _iota((ch, GLA_HEADS * ch), 1) % ch
    own_head = (_iota((WIDTH, kw), 0) // GLA_HEAD_V) == (_iota((WIDTH, kw), 1) // GLA_HEAD_K)
    lower = _tri(lb, ch)
    mid = ch // 2 - 1
    rss = [slice(c * ch, (c + 1) * ch) for c in range(lb // ch)]
    units = [(bi, c) for bi in range(nb) for c in range(lb // ch)]

    b_all = []
    for bi in range(nb):
        x = _mm(sm_ref[bi], aup_ref[...], pb=2) + ab_ref[...]
        b_all.append(_mm(lower, _log_sigmoid(x) * (1.0 / GLA_GATE_TAU), pb=3))
    q_in, q_at, k_at, k_st, dec = {}, {}, {}, {}, {}
    for bi, c in units:
        b = b_all[bi][rss[c]]
        b_mid, b_last = b[mid:mid + 1, :], b[ch - 1:ch, :]
        q = q_ref[bi, rss[c], :].astype(F32) * GLA_HEAD_K ** -0.5
        k = k_ref[bi, rss[c], :].astype(F32)
        q_in[bi, c] = (q * jnp.exp(b)).astype(BF16)
        q_at[bi, c] = (q * jnp.exp(b - b_mid)).astype(BF16)
        k_at[bi, c] = (k * jnp.exp(b_mid - b)).astype(BF16)
        k_st[bi, c] = (k * jnp.exp(b_last - b)).astype(BF16)
        dec[bi, c] = jnp.exp(b_last)
    v = {(bi, c): v_ref[bi, rss[c], :] for bi, c in units}
    attn = {u: jnp.where(causal, _mm(q_at[u], stack(k_at[u], key_masks), _NT), 0.0) for u in units}
    intra = {u: _mm(attn[u], stack(v[u], val_masks)) for u in units}
    grow = {u: jnp.where(own_head, _mm(v[u].astype(F32).T, k_st[u]), 0.0) for u in units}
    state = {}
    for bi in range(nb):
        st = st_ref[bi]
        for c in range(lb // ch):
            state[bi, c] = st
            st = dec[bi, c] * st + grow[bi, c]
        st_ref[bi] = st
    for bi, c in units:
        o = _mm(q_in[bi, c], state[bi, c], _NT) + intra[bi, c]
        for h in heads:
            vs = slice(h * GLA_HEAD_V, (h + 1) * GLA_HEAD_V)
            oh = o[:, vs]
            on = oh * lax.rsqrt(jnp.mean(oh * oh, axis=-1, keepdims=True) + NORM_EPS) * ng_ref[:, vs]
            g = g_ref[bi, rss[c], vs].astype(F32)
            out_ref[bi, rss[c], vs] = (on * (g * _sigmoid(g))).astype(out_ref.dtype)


def _gla(z, layer, a_up, a_bias, norm_g):
    bsz, t, _ = z.shape
    lb = TIME_BLOCK
    kw = GLA_KEY_WIDTH
    return pl.pallas_call(
        _gla_kernel,
        grid=(t // lb,),
        in_specs=[
            pl.BlockSpec((bsz, lb, kw), lambda c: (0, c, ZC_CQ // kw)),
            pl.BlockSpec((bsz, lb, kw), lambda c: (0, c, ZC_CK // kw)),
            pl.BlockSpec((bsz, lb, WIDTH), lambda c: (0, c, ZC_CV // WIDTH)),
            pl.BlockSpec((bsz, lb, WIDTH), lambda c: (0, c, ZC_CG // WIDTH)),
            pl.BlockSpec((bsz, lb, SMALL), lambda c: (0, c, ZC_SM // SMALL)),
            _layer_spec(layer, SMALL, kw),
            _layer_spec(layer, 1, kw),
            _layer_spec(layer, 1, WIDTH),
        ],
        out_specs=pl.BlockSpec((bsz, lb, WIDTH), lambda c: (0, c, 0)),
        out_shape=jax.ShapeDtypeStruct((bsz, t, WIDTH), BF16),
        scratch_shapes=[pltpu.VMEM((bsz, WIDTH, kw), F32)],
        compiler_params=pltpu.CompilerParams(
            dimension_semantics=("arbitrary",), vmem_limit_bytes=VMEM_LIMIT),
        name="gla",
    )(z, z, z, z, z, a_up, a_bias, norm_g)


def _rwkv_kernel(r_ref, k_ref, v_ref, t_ref, mu_ref, mut_ref,
                 w0_ref, wup_ref, a0_ref, aup_ref, gup_ref, kk_ref, ka_ref, rk_ref, ng_ref,
                 out_ref, pr_ref, pk_ref, pv_ref, pt_ref, st_ref, y_ref):
    nb, ch = r_ref.shape[0], r_ref.shape[1]
    lb = nb * ch
    gl = RWKV_GROUP_LANES
    gh = RWKV_GROUP_HEADS

    @pl.when(pl.program_id(0) == 0)
    def _():
        for ref in (pr_ref, pk_ref, pv_ref, pt_ref, st_ref):
            ref[...] = jnp.zeros_like(ref)

    def shift_mix(x_ref, prev_ref, mu):
        mixed = []
        for bi in range(nb):
            x = x_ref[bi].astype(F32)
            xs = _shift_rows(prev_ref.at[bi], x, 1)
            prev_ref[bi] = x[ch - prev_ref.shape[1]:]
            mixed.append(x + mu * (xs - x))
        return jnp.concatenate(mixed, axis=0)

    r = shift_mix(r_ref, pr_ref, mu_ref[0:1, :])
    k = shift_mix(k_ref, pk_ref, mu_ref[1:2, :])
    v = shift_mix(v_ref, pv_ref, mu_ref[2:3, :])
    tl = shift_mix(t_ref, pt_ref, mut_ref[...])

    ones_bd = _tri(gl, RWKV_HEAD) + _tri(gl, RWKV_HEAD, upper=True, strict=True)

    def head_sum(x):
        return jnp.concatenate([_mm(x[:, i * gl:(i + 1) * gl], ones_bd, pa=2) for i in range(WIDTH // gl)], axis=1)

    w_log = _log_sigmoid(w0_ref[...] + _mm(jnp.tanh(tl), wup_ref[...])) - 0.5
    log_w = -jnp.exp(w_log)
    a = _sigmoid(a0_ref[...] + _mm(tl, aup_ref[...]))
    g = _mm(_sigmoid(tl), gup_ref[...])
    kk = k * kk_ref[...]
    kk = kk / jnp.maximum(jnp.sqrt(head_sum(kk * kk)), 1e-12)
    k_rep = k * (1.0 + (a - 1.0) * ka_ref[...])
    av = -kk
    bv = kk * a
    bonus = head_sum(r * k_rep * rk_ref[...]) * v
    gc_all = _mm(_tri(lb, ch), log_w, pb=3)

    lane_head = _iota((1, gl), 1) // RWKV_HEAD
    masks = [jnp.where(lane_head == h, 1.0, 0.0).astype(BF16) for h in range(gh)]

    def stack(x):
        xb = x.astype(BF16)
        return jnp.concatenate([xb * m for m in masks], axis=0)

    def stack_t(x):
        return jnp.concatenate([x * m.astype(F32) for m in masks], axis=0).T

    s_lane = _iota((ch, gl), 1) % ch
    t_row = _iota((ch, gl), 0)
    strict = t_row > s_lane
    incl = t_row >= s_lane
    eye = jnp.where(t_row == s_lane, 1.0, 0.0)

    pairs = [(gi, bi) for gi in range(WIDTH // gl) for bi in range(nb)]
    pre = {}
    for gi, bi in pairs:
        ls = slice(gi * gl, (gi + 1) * gl)
        rs = slice(bi * ch, (bi + 1) * ch)
        gc = gc_all[rs, ls]
        g_last = gc[ch - 1:ch, :]
        e_inv = jnp.exp(-gc)
        e_end = jnp.exp(g_last - gc)
        r_t = (r[rs, ls] * jnp.exp(gc)).astype(BF16)
        a_t = (av[rs, ls] * jnp.exp(gc - log_w[rs, ls])).astype(BF16)
        scores = _mm(jnp.concatenate([a_t, r_t], axis=0),
                     jnp.concatenate([stack(bv[rs, ls] * e_inv), stack(k_rep[rs, ls] * e_inv)], axis=0), _NT)
        pre[gi, bi] = dict(
            r_t=r_t, a_bd=stack(a_t), v_bd=stack(v[rs, ls]), dec=jnp.exp(g_last),
            b_e=stack(bv[rs, ls] * e_end), k_e=stack(k_rep[rs, ls] * e_end),
            a_ab=jnp.where(strict, scores[:ch, :gl], 0.0),
            a_kv=jnp.concatenate([jnp.where(strict, scores[:ch, gl:], 0.0),
                                  jnp.where(incl, scores[ch:, gl:], 0.0)], axis=0).astype(BF16),
            a_rb=jnp.where(incl, scores[ch:, :gl], 0.0).astype(BF16))
    tinv = {key: eye + pre[key]["a_ab"] for key in pairs}
    power = {key: pre[key]["a_ab"].astype(BF16) for key in pairs}
    for key in pairs:
        power[key] = _mm(power[key], stack(power[key])).astype(BF16)
    step = 2
    while step < ch:
        for key in pairs:
            if 2 * step < ch:
                both = _mm(jnp.concatenate([power[key], tinv[key].astype(BF16)], axis=0), stack(power[key]))
                power[key] = both[:ch].astype(BF16)
                tinv[key] = tinv[key] + both[ch:]
            else:
                tinv[key] = tinv[key] + _mm(tinv[key], stack(power[key]))
        step *= 2

    from_v = {key: _mm(pre[key]["a_kv"], pre[key]["v_bd"]) for key in pairs}
    t_bf = {key: tinv[key].astype(BF16) for key in pairs}
    a_hat = {key: _mm(t_bf[key], pre[key]["a_bd"]).astype(BF16) for key in pairs}
    u0 = {key: _mm(t_bf[key], stack(from_v[key][:ch])) for key in pairs}
    grow = {key: _mm(pre[key]["v_bd"].astype(F32).T, pre[key]["k_e"]) for key in pairs}
    n_groups = WIDTH // gl
    state = {(gi, bi): st_ref[bi * n_groups + gi] for gi, bi in pairs}
    from_state = {key: _mm(jnp.concatenate([a_hat[key], pre[key]["r_t"]], axis=0), state[key], _NT) for key in pairs}
    u = {key: from_state[key][:ch] + u0[key] for key in pairs}
    for gi, bi in pairs:
        key = (gi, bi)
        y_ref[bi * ch:(bi + 1) * ch, gi * gl:(gi + 1) * gl] = (
            from_state[key][ch:] + _mm(pre[key]["a_rb"], stack(u[key])) + from_v[key][ch:])
    for gi, bi in pairs:
        key = (gi, bi)
        st_ref[bi * n_groups + gi] = (pre[key]["dec"] * state[key] + _mm(stack_t(u[key]), pre[key]["b_e"])
                                      + grow[key])

    y = y_ref[...]
    inv_n = 1.0 / RWKV_HEAD
    mu = head_sum(y) * inv_n
    yc = y - mu
    var = head_sum(yc * yc) * inv_n
    yn = yc * lax.rsqrt(var + RWKV_GN_EPS) * ng_ref[...]
    out_ref[...] = ((yn + bonus) * g).astype(out_ref.dtype).reshape(nb, ch, WIDTH)


def _rwkv(z, layer, mu_rkv, mu_t, w0, w_up, a0, a_up, g_up, k_k, k_a, r_k, norm_g):
    bsz, t, _ = z.shape
    ch = RWKV_CHUNK
    gl = RWKV_GROUP_LANES
    vec = _layer_spec(layer, 1, WIDTH)
    mat = _layer_spec(layer, RWKV_TAIL, WIDTH)
    return pl.pallas_call(
        _rwkv_kernel,
        grid=(t // ch,),
        in_specs=[
            pl.BlockSpec((bsz, ch, WIDTH), lambda c: (0, c, ZC_DR // WIDTH)),
            pl.BlockSpec((bsz, ch, WIDTH), lambda c: (0, c, ZC_DK // WIDTH)),
            pl.BlockSpec((bsz, ch, WIDTH), lambda c: (0, c, ZC_DV // WIDTH)),
            pl.BlockSpec((bsz, ch, RWKV_TAIL), lambda c: (0, c, ZC_DT // RWKV_TAIL)),
            _layer_spec(layer, 3, WIDTH), _layer_spec(layer, 1, RWKV_TAIL),
            vec, mat, vec, mat, mat, vec, vec, vec, vec,
        ],
        out_specs=pl.BlockSpec((bsz, ch, WIDTH), lambda c: (0, c, 0)),
        out_shape=jax.ShapeDtypeStruct((bsz, t, WIDTH), BF16),
        scratch_shapes=[
            pltpu.VMEM((bsz, SUBLANES, WIDTH), F32), pltpu.VMEM((bsz, SUBLANES, WIDTH), F32),
            pltpu.VMEM((bsz, SUBLANES, WIDTH), F32), pltpu.VMEM((bsz, SUBLANES, RWKV_TAIL), F32),
            pltpu.VMEM((bsz * (WIDTH // gl), gl, gl), F32),
            pltpu.VMEM((bsz * ch, WIDTH), F32),
        ],
        compiler_params=pltpu.CompilerParams(
            dimension_semantics=("arbitrary",), vmem_limit_bytes=VMEM_LIMIT),
        name="rwkv",
    )(z, z, z, z, mu_rkv, mu_t, w0, w_up, a0, a_up, g_up, k_k, k_a, r_k, norm_g)


def _merge_kernel(x_ref, ya_ref, zb_ref, yc_ref, yd_ref, g0_ref, g1_ref, g2_ref, g3_ref, gb_ref,
                  pa_ref, w1_ref, w2_ref, pc_ref, pd_ref, wo_ref, out_ref):
    d = D_MODEL

    def gate(g_ref, i):
        return _sigmoid(g_ref[...].astype(F32) + gb_ref[:, i * d:(i + 1) * d])

    zb = zb_ref[...]
    merged = gate(g0_ref, 0) * _mm(ya_ref[...], pa_ref[...])
    merged = merged + gate(g1_ref, 1) * (_mm(zb, w1_ref[...]) * _sigmoid(_mm(zb, w2_ref[...])))
    merged = merged + gate(g2_ref, 2) * _mm(yc_ref[...], pc_ref[...])
    merged = merged + gate(g3_ref, 3) * _mm(yd_ref[...], pd_ref[...])
    out_ref[...] = x_ref[...] + _mm(merged, wo_ref[...])


def _merge(x, z, ya, zb, yc, yd, layer, gate_bias, pa, w1, w2, pc, pd, wo, *, tm=512):
    n, d = x.shape
    tile = lambda w: pl.BlockSpec((tm, w), lambda i: (i, 0))
    gate = lambda b: pl.BlockSpec((tm, d), lambda i: (i, ZC_GATE // d + b))
    resident = lambda rows: pl.BlockSpec((None, rows, d), lambda i: (layer, 0, 0), pipeline_mode=pl.Buffered(1))
    proj = resident(WIDTH)
    return pl.pallas_call(
        _merge_kernel,
        grid=(n // tm,),
        in_specs=[tile(d), tile(WIDTH), tile(WIDTH), tile(WIDTH), tile(WIDTH),
                  gate(0), gate(1), gate(2), gate(3), _layer_spec(layer, 1, N_BRANCH * d),
                  proj, proj, proj, proj, proj, resident(d)],
        out_specs=tile(d),
        out_shape=jax.ShapeDtypeStruct((n, d), F32),
        compiler_params=pltpu.CompilerParams(
            dimension_semantics=("parallel",), vmem_limit_bytes=VMEM_LIMIT),
        name="merge",
    )(x, ya, zb, yc, yd, z, z, z, z, gate_bias, pa, w1, w2, pc, pd, wo)


def _xattn_kernel(x_ref, g_ref, wq_ref, kv_ref, wo_ref, out_ref):
    d = D_MODEL
    hd = XATTN_HEAD_DIM
    x = x_ref[...]
    q = _mm(_rms(x, g_ref[...]), wq_ref[...])
    kv = kv_ref[...]
    heads = range(XATTN_HEADS)
    sls = [slice(h * hd, (h + 1) * hd) for h in heads]
    qb = q.astype(BF16)
    s = [_mm(qb[:, sls[h]], kv[:, sls[h]], _NT) * hd ** -0.5 for h in heads]
    p = [jnp.exp(s[h] - jnp.max(s[h], axis=-1, keepdims=True)) for h in heads]
    p = [p[h] / jnp.sum(p[h], axis=-1, keepdims=True) for h in heads]
    o = jnp.concatenate([_mm(p[h], kv[:, d + h * hd:d + (h + 1) * hd]) for h in heads], axis=1)
    out_ref[...] = x + _mm(o, wo_ref[...])


def _xattn(x, bsz, layer, gain, wq, kv, wo, *, tm=1024):
    n, d = x.shape
    nblk = n // bsz // tm
    m_len = kv.shape[0] // bsz
    return pl.pallas_call(
        _xattn_kernel,
        grid=(bsz, nblk),
        in_specs=[
            pl.BlockSpec((tm, d), lambda b, c: (b * nblk + c, 0)),
            _layer_spec(layer, 1, d),
            _layer_spec(layer, d, d),
            pl.BlockSpec((m_len, 2 * d), lambda b, c: (b, 0)),
            _layer_spec(layer, d, d),
        ],
        out_specs=pl.BlockSpec((tm, d), lambda b, c: (b * nblk + c, 0)),
        out_shape=jax.ShapeDtypeStruct((n, d), F32),
        compiler_params=pltpu.CompilerParams(
            dimension_semantics=("parallel", "parallel"), vmem_limit_bytes=VMEM_LIMIT),
        name="xattn",
    )(x, gain, wq, kv, wo)


def _w_in_segments():
    o = np.cumsum((0, WIDTH, WIDTH, MLSTM_HEADS, MLSTM_HEADS, WIDTH, GLA_KEY_WIDTH, GLA_KEY_WIDTH, WIDTH, WIDTH,
                   GLA_GATE_RANK, 3 * WIDTH + RWKV_TAIL, N_BRANCH * D_MODEL))
    a_u, a_o, a_i, a_f, b_u, c_q, c_k, c_v, c_g, c_a, d_z, gate = range(12)
    seg = lambda i: (int(o[i]), int(o[i + 1] - o[i]))
    d_parts = [(int(o[d_z]) + i * WIDTH, WIDTH) for i in range(3)] + [(int(o[d_z]) + 3 * WIDTH, RWKV_TAIL)]
    return ([seg(gate), seg(a_u), seg(a_o), seg(b_u), seg(c_v), seg(c_g)] + d_parts[:3]
            + [seg(c_q), seg(c_k), d_parts[3], seg(a_i), seg(a_f), seg(c_a)])


def _regroup_kernel(w_ref, o_ref):
    lanes = w_ref.shape[1]
    segments = _w_in_segments()
    big = [s for s in segments if s[1] % SMALL == 0]
    dst = 0
    for src, width in big:
        o_ref[dst:dst + width, :] = w_ref[src:src + width, :].astype(o_ref.dtype)
        dst += width
    small = [w_ref[src:src + width, :] for src, width in segments if width % SMALL]
    used = sum(s.shape[0] for s in small)
    small.append(jnp.zeros((o_ref.shape[0] - dst - used, lanes), F32))
    o_ref[dst:, :] = jnp.concatenate(small, axis=0).astype(o_ref.dtype)


def _regroup_w_in(w_t, *, lanes=256):
    depth, n_in, d = w_t.shape
    return pl.pallas_call(
        _regroup_kernel,
        grid=(depth, d // lanes),
        in_specs=[pl.BlockSpec((None, n_in, lanes), lambda l, i: (l, 0, i))],
        out_specs=pl.BlockSpec((None, Z_COLS, lanes), lambda l, i: (l, 0, i)),
        out_shape=jax.ShapeDtypeStruct((depth, Z_COLS, d), BF16),
        compiler_params=pltpu.CompilerParams(
            dimension_semantics=("parallel", "parallel"), vmem_limit_bytes=VMEM_LIMIT),
        name="regroup_w_in",
    )(w_t)


def _s5_tables(a_re, a_im, log_step, b_re, b_im, c_re, c_im, d_skip, bsz):
    depth = a_re.shape[0]
    step = jnp.exp(log_step)[..., None]
    lam_re = jnp.minimum(a_re, -S5_MIN_NEG)
    lam_im = a_im
    mag = jnp.exp(lam_re * step)
    bar_re = mag * jnp.cos(lam_im * step)
    bar_im = mag * jnp.sin(lam_im * step)
    denom = lam_re * lam_re + lam_im * lam_im
    coef_re = ((bar_re - 1.0) * lam_re + bar_im * lam_im) / denom
    coef_im = (bar_im * lam_re - (bar_re - 1.0) * lam_im) / denom
    bb_re = coef_re[..., None] * b_re - coef_im[..., None] * b_im
    bb_im = coef_re[..., None] * b_im + coef_im[..., None] * b_re
    nb = S5_HALF // S5_BLOCK_CH
    gpb = S5_BLOCK_GROUPS
    eye = jnp.eye(gpb, dtype=F32)

    def in_block(bb):
        bb = bb.reshape(depth, 2, nb, gpb, S5_STATE, S5_GROUP)
        return jnp.einsum('lhjgpc,gk->lhjgckp', bb, eye).reshape(depth, 2, nb, S5_BLOCK_CH, S5_BLOCK_STATES)

    def out_block(cc):
        cc = cc.reshape(depth, 2, nb, gpb, S5_GROUP, S5_STATE)
        return jnp.einsum('lhjgcp,gk->lhjgpkc', cc, eye).reshape(depth, 2, nb, S5_BLOCK_STATES, S5_BLOCK_CH)

    w_in = jnp.concatenate([in_block(bb_re), in_block(bb_im)], axis=-1)
    w_in = w_in.transpose(0, 2, 1, 3, 4).reshape(depth, nb, 2 * S5_BLOCK_CH, 2 * S5_BLOCK_STATES)
    w_out = jnp.concatenate([out_block(c_re), -out_block(c_im)], axis=-2)
    w_out = w_out.transpose(0, 2, 3, 1, 4).reshape(depth, nb, 2 * S5_BLOCK_STATES, 2 * S5_BLOCK_CH)

    def rows(t, width):
        t = t.reshape(depth, 2, 1, width)
        return jnp.broadcast_to(t, (depth, 2, bsz, width)).reshape(depth, 2 * bsz, width)

    nst = nb * S5_BLOCK_STATES
    lam_re_rows = rows(bar_re.reshape(depth, 2 * nst), nst)
    lam_im_rows = rows(bar_im.reshape(depth, 2 * nst), nst)
    d_tab = jnp.tile(rows(d_skip, S5_HALF), (1, S5_STEPS, 1))
    return w_in.astype(BF16), w_out.astype(BF16), lam_re_rows, lam_im_rows, d_tab


def _pad_rows(w, start, total):
    return jnp.pad(w, ((0, 0), (start, total - start - w.shape[1]), (0, 0)))


def kernel(x, mem, ffn1_norm, ffn1_w_gate, ffn1_w_up, ffn1_w_down, mix_norm, w_in, gate_bias, mlstm_conv, mlstm_wq, mlstm_wk, mlstm_wv, mlstm_b_i, mlstm_b_f, mlstm_norm, mlstm_proj, s5_a_re, s5_a_im, s5_log_step, s5_b_re, s5_b_im, s5_c_re, s5_c_im, s5_d, s5_glu_w1, s5_glu_w2, gla_a_up, gla_a_bias, gla_norm, gla_proj, rwkv_mu, rwkv_w0, rwkv_w_up, rwkv_a0, rwkv_a_up, rwkv_g_up, rwkv_k_k, rwkv_k_a, rwkv_r_k, rwkv_norm, rwkv_proj, w_out, xattn_norm, mem_norm, xattn_wq, xattn_wk, xattn_wv, xattn_wo, ffn2_norm, ffn2_w_gate, ffn2_w_up, ffn2_w_down, final_norm):
    bsz, t, d = x.shape
    depth = w_in.shape[0]
    assert d == D_MODEL and t % TIME_BLOCK == 0 and 2 * bsz == SUBLANES
    xs = x.reshape(bsz * t, d)
    mems = mem.reshape(bsz * mem.shape[1], d)
    bf = lambda w: w.astype(BF16)
    vec = lambda v: v.reshape(depth, 1, -1)
    fin = final_norm.reshape(1, d)

    w_in_r = _regroup_w_in(jnp.swapaxes(w_in, 1, 2))
    gate_b = jnp.pad(jnp.concatenate([mlstm_b_i, mlstm_b_f], axis=1), ((0, 0), (0, SMALL - 2 * MLSTM_HEADS)))
    s5_w_in, s5_w_out, s5_lre, s5_lim, s5_dtab = _s5_tables(
        s5_a_re, s5_a_im, s5_log_step, s5_b_re, s5_b_im, s5_c_re, s5_c_im, s5_d, bsz)
    gla_up = _pad_rows(gla_a_up, 2 * MLSTM_HEADS, SMALL)
    mu_rkv = rwkv_mu[:, :3 * WIDTH].reshape(depth, 3, WIDTH)
    mu_t = vec(rwkv_mu[:, 3 * WIDTH:])
    rw_wup = bf(_pad_rows(rwkv_w_up, 0, RWKV_TAIL))
    rw_aup = bf(_pad_rows(rwkv_a_up, RWKV_DECAY_RANK, RWKV_TAIL))
    rw_gup = bf(_pad_rows(rwkv_g_up, RWKV_DECAY_RANK + RWKV_ICLR_RANK, RWKV_TAIL))
    w_kv = bf(jnp.concatenate([xattn_wk, xattn_wv], axis=2))
    wq_a, wk_a, wv_a = bf(mlstm_wq), bf(mlstm_wk), bf(mlstm_wv)
    p_a, p_b1, p_b2, p_c, p_d, p_o = mlstm_proj, s5_glu_w1, s5_glu_w2, gla_proj, rwkv_proj, w_out
    x_wq, x_wo = xattn_wq, xattn_wo
    f1_g, f1_u, f1_d = ffn1_w_gate, ffn1_w_up, ffn1_w_down
    f2_g, f2_u, f2_d = ffn2_w_gate, ffn2_w_up, ffn2_w_down

    for l in range(depth):
        xs = _ffn(xs, vec(ffn1_norm), f1_g, f1_u, f1_d, fin, layer=l, final=False)

        z = _in_proj(xs, vec(mix_norm), w_in_r, layer=l)

        z3 = z.reshape(bsz, t, Z_COLS)
        y_a = _mlstm(z3, l, mlstm_conv, wq_a, wk_a, wv_a, vec(gate_b), gate_b.reshape(depth, SMALL, 1),
                     vec(mlstm_norm)).reshape(bsz * t, WIDTH)

        z_b = _s5(z3, bsz, l, s5_w_in, s5_w_out, s5_lre, s5_lim, s5_dtab).reshape(bsz * t, WIDTH)

        y_c = _gla(z3, l, gla_up, vec(gla_a_bias), vec(gla_norm)).reshape(bsz * t, WIDTH)

        y_d = _rwkv(z3, l, mu_rkv, mu_t, vec(rwkv_w0), rw_wup, vec(rwkv_a0), rw_aup, rw_gup,
                    vec(rwkv_k_k), vec(rwkv_k_a), vec(rwkv_r_k), vec(rwkv_norm)).reshape(bsz * t, WIDTH)

        xs = _merge(xs, z, y_a, z_b, y_c, y_d, l, vec(gate_bias), p_a, p_b1, p_b2, p_c, p_d, p_o)

        kv = _norm_matmul(mems, vec(mem_norm), w_kv, layer=l, tm=mems.shape[0], tn=512, name="mem_kv",
                          out_dtype=BF16)
        xs = _xattn(xs, bsz, l, vec(xattn_norm), x_wq, kv, x_wo)

        xs = _ffn(xs, vec(ffn2_norm), f2_g, f2_u, f2_d, fin, layer=l,
                  final=(l == depth - 1))
    return xs.reshape(bsz, t, d)
```

```python
import functools
import math

import jax
import jax.numpy as jnp
import numpy as np
from jax import lax
from jax.experimental import pallas as pl
from jax.experimental.pallas import tpu as pltpu

F32 = jnp.float32
BF16 = jnp.bfloat16

D_MODEL = 1024
D_FF = 11 * D_MODEL // 4
NORM_EPS = 1e-6
FFN_HALF = 0.5
N_BRANCH = 4
WIDTH = D_MODEL // 2

MLSTM_HEADS = 4
MLSTM_HEAD_DIM = WIDTH // MLSTM_HEADS
MLSTM_CONV = 4
MLSTM_CHUNK = 256

S5_GROUP = 16
S5_GROUPS = WIDTH // S5_GROUP
S5_STATE = 64
S5_MIN_NEG = 1e-4
S5_HALF = WIDTH // 2
S5_BLOCK_GROUPS = 8
S5_BLOCK_CH = S5_BLOCK_GROUPS * S5_GROUP
S5_BLOCK_STATES = S5_BLOCK_GROUPS * S5_STATE
S5_STEPS = 64
SUBLANES = 8

GLA_HEADS = 4
GLA_KEY_WIDTH = WIDTH // 2
GLA_HEAD_K = GLA_KEY_WIDTH // GLA_HEADS
GLA_HEAD_V = WIDTH // GLA_HEADS
GLA_GATE_RANK = 16
GLA_GATE_TAU = 16.0
GLA_CHUNK = 64

RWKV_HEAD = 64
RWKV_HEADS = WIDTH // RWKV_HEAD
RWKV_DECAY_RANK = 64
RWKV_ICLR_RANK = 64
RWKV_GATE_RANK = 128
RWKV_TAIL = RWKV_DECAY_RANK + RWKV_ICLR_RANK + RWKV_GATE_RANK
RWKV_GN_EPS = 64e-5
RWKV_CHUNK = 64
RWKV_GROUP_LANES = 256
RWKV_GROUP_HEADS = RWKV_GROUP_LANES // RWKV_HEAD

XATTN_HEADS = 4
XATTN_HEAD_DIM = D_MODEL // XATTN_HEADS

ZC_GATE = 0
ZC_AU = 4096
ZC_AO = 4608
ZC_BU = 5120
ZC_CV = 5632
ZC_CG = 6144
ZC_DR = 6656
ZC_DK = 7168
ZC_DV = 7680
ZC_CQ = 8192
ZC_CK = 8448
ZC_DT = 8704
ZC_SM = 8960
Z_COLS = 9216
SMALL = 128

TIME_BLOCK = 256
VMEM_LIMIT = 56 * 1024 * 1024

_NN = (((1,), (0,)), ((), ()))
_NT = (((1,), (1,)), ((), ()))


def _split(x, n):
    parts = []
    rest = x
    for i in range(n):
        p = rest.astype(BF16)
        parts.append(p)
        if i + 1 < n:
            rest = rest - p.astype(F32)
    return parts


def _mm(a, b, dn=_NN, pa=1, pb=1):
    ap, bp = _split(a, pa), _split(b, pb)
    order = max(pa, pb)
    out = None
    for i, x in enumerate(ap):
        for j, y in enumerate(bp):
            if i + j < order:
                t = lax.dot_general(x, y, dn, preferred_element_type=F32)
                out = t if out is None else out + t
    return out


def _rms(x, g):
    return x * lax.rsqrt(jnp.mean(x * x, axis=-1, keepdims=True) + NORM_EPS) * g


def _sigmoid(x):
    return 0.5 * jnp.tanh(0.5 * x) + 0.5


def _log_sigmoid(x):
    return jnp.minimum(x, 0.0) - jnp.log(1.0 + jnp.exp(-jnp.abs(x)))


def _iota(shape, dim):
    return lax.broadcasted_iota(jnp.int32, shape, dim)


def _tri(n, block, upper=False, strict=False):
    r, c = _iota((n, n), 0), _iota((n, n), 1)
    same = (r // block) == (c // block)
    if upper:
        keep = (r < c) if strict else (r <= c)
    else:
        keep = (r > c) if strict else (r >= c)
    return jnp.where(same & keep, 1.0, 0.0).astype(BF16)


def _shift_rows(prev_ref, x, k):
    ext = jnp.concatenate([prev_ref[...], x], axis=0)
    return pltpu.roll(ext, k, 0)[prev_ref.shape[0]:]


def _layer_spec(layer, *shape):
    return pl.BlockSpec((None,) + shape, lambda *_: (layer,) + (0,) * len(shape))


def _ffn_kernel(x_ref, g_ref, wg_ref, wu_ref, wd_ref, fg_ref, o_ref, *, final, tf):
    x = x_ref[...]
    xn = _rms(x, g_ref[...]).astype(BF16)
    acc = None
    for j in range(wg_ref.shape[1] // tf):
        cols = slice(j * tf, (j + 1) * tf)
        gate = _mm(xn, wg_ref[:, cols])
        up = _mm(xn, wu_ref[:, cols])
        part = _mm(gate * _sigmoid(gate) * up, wd_ref[cols, :])
        acc = part if acc is None else acc + part
    y = x + FFN_HALF * acc
    if final:
        y = _rms(y, fg_ref[...])
    o_ref[...] = y


def _ffn(x, gain, wg, wu, wd, final_gain, *, layer, final, tm=512, tf=256):
    n, d = x.shape
    f = wg.shape[2]
    resident = lambda *shape: pl.BlockSpec((None,) + shape, lambda i: (layer, 0, 0), pipeline_mode=pl.Buffered(1))
    return pl.pallas_call(
        functools.partial(_ffn_kernel, final=final, tf=tf),
        grid=(n // tm,),
        in_specs=[
            pl.BlockSpec((tm, d), lambda i: (i, 0)),
            _layer_spec(layer, 1, d),
            resident(d, f), resident(d, f), resident(f, d),
            pl.BlockSpec((1, d), lambda i: (0, 0)),
        ],
        out_specs=pl.BlockSpec((tm, d), lambda i: (i, 0)),
        out_shape=jax.ShapeDtypeStruct((n, d), F32),
        compiler_params=pltpu.CompilerParams(
            dimension_semantics=("parallel",), vmem_limit_bytes=VMEM_LIMIT),
        name="ffn",
    )(x, gain, wg, wu, wd, final_gain)


def _in_proj_kernel(x_ref, g_ref, w_ref, o_ref, *, tn):
    xn = _rms(x_ref[...], g_ref[...]).astype(BF16)
    for j in range(w_ref.shape[0] // tn):
        cols = slice(j * tn, (j + 1) * tn)
        o_ref[:, cols] = _mm(xn, w_ref[cols, :], _NT).astype(o_ref.dtype)


def _in_proj(x, gain, w, *, layer, tm=512, tn=1536):
    n, d = x.shape
    m = w.shape[1]
    return pl.pallas_call(
        functools.partial(_in_proj_kernel, tn=tn),
        grid=(n // tm,),
        in_specs=[
            pl.BlockSpec((tm, d), lambda i: (i, 0)),
            _layer_spec(layer, 1, d),
            pl.BlockSpec((None, m, d), lambda i: (layer, 0, 0), pipeline_mode=pl.Buffered(1)),
        ],
        out_specs=pl.BlockSpec((tm, m), lambda i: (i, 0)),
        out_shape=jax.ShapeDtypeStruct((n, m), BF16),
        compiler_params=pltpu.CompilerParams(
            dimension_semantics=("parallel",), vmem_limit_bytes=VMEM_LIMIT),
        name="in_proj",
    )(x, gain, w)


def _mlstm_kernel(u_ref, o_ref, sm_ref, conv_ref, wq_ref, wk_ref, wv_ref, brow_ref, bcol_ref, ng_ref,
                  out_ref, uprev_ref, c_ref, m_ref):
    nb, lb = u_ref.shape[0], u_ref.shape[1]
    e = MLSTM_HEAD_DIM
    nh = MLSTM_HEADS

    @pl.when(pl.program_id(0) == 0)
    def _():
        uprev_ref[...] = jnp.zeros_like(uprev_ref)
        c_ref[...] = jnp.zeros_like(c_ref)
        m_ref[...] = jnp.zeros_like(m_ref)

    u, uc, pre, pre_t, b_cols, b_rows = [], [], [], [], [], []
    lower, upper = _tri(lb, lb), _tri(lb, lb, upper=True)
    for bi in range(nb):
        ub = u_ref[bi].astype(F32)
        conv = None
        for j in range(MLSTM_CONV):
            k = MLSTM_CONV - 1 - j
            term = conv_ref[j:j + 1, :] * (_shift_rows(uprev_ref.at[bi], ub, k) if k else ub)
            conv = term if conv is None else conv + term
        uprev_ref[bi] = ub[lb - uprev_ref.shape[1]:]
        u.append(ub)
        uc.append(conv * _sigmoid(conv))
        sm = sm_ref[bi].astype(F32)
        pre.append(sm + brow_ref[...])
        pre_t.append(sm.T[0:8, :] + bcol_ref[0:8, :])
        b_cols.append(_mm(lower, _log_sigmoid(pre[bi]), pb=3))
        b_rows.append(_mm(_log_sigmoid(pre_t[bi]), upper, pa=3))

    causal = _iota((lb, lb), 0) >= _iota((lb, lb), 1)
    streams = [(bi, h) for bi in range(nb) for h in range(nh)]
    sls = [slice(h * e, (h + 1) * e) for h in range(nh)]
    ones = jnp.ones((lb, e), BF16)
    q = {s: _mm(uc[s[0]][:, sls[s[1]]], wq_ref[s[1]]).astype(BF16) for s in streams}
    k = {s: _mm(uc[s[0]][:, sls[s[1]]], wk_ref[s[1]]) * e ** -0.5 for s in streams}
    v = {s: jnp.concatenate([_mm(u[s[0]][:, sls[s[1]]], wv_ref[s[1]]).astype(BF16), ones], axis=1) for s in streams}
    bc = {(bi, h): b_cols[bi][:, nh + h:nh + h + 1] for bi, h in streams}
    m_st = {(bi, h): m_ref[bi * SUBLANES + h:bi * SUBLANES + h + 1, 0:1] for bi, h in streams}
    dmat = {(bi, h): jnp.where(causal, bc[bi, h] - b_rows[bi][nh + h:nh + h + 1, :] + pre_t[bi][h:h + 1, :], -jnp.inf)
            for bi, h in streams}
    m_inter = {s: bc[s] + m_st[s] for s in streams}
    m_row = {s: jnp.maximum(m_inter[s], jnp.max(dmat[s], axis=-1, keepdims=True)) for s in streams}
    qk = {s: _mm(q[s], k[s], _NT) for s in streams}
    w = {s: jnp.exp(dmat[s] - m_row[s]) * qk[s] for s in streams}
    from_state = {(bi, h): _mm(q[bi, h], c_ref[bi * nh + h]) for bi, h in streams}
    from_chunk = {s: _mm(w[s], v[s]) for s in streams}
    for bi, h in streams:
        s = (bi, h)
        tot = jnp.exp(m_inter[s] - m_row[s]) * from_state[s] + from_chunk[s]
        hh = tot[:, :e] / jnp.maximum(jnp.abs(tot[:, e:e + 1]), jnp.exp(-m_row[s]))
        hn = hh * lax.rsqrt(jnp.mean(hh * hh, axis=-1, keepdims=True) + NORM_EPS) * ng_ref[:, sls[h]]
        out_ref[bi, :, sls[h]] = (hn * _sigmoid(o_ref[bi, :, sls[h]].astype(F32))).astype(out_ref.dtype)
    for bi, h in streams:
        s = (bi, h)
        b_last = bc[s][lb - 1:lb, :]
        e_col = b_last - bc[s] + pre[bi][:, h:h + 1]
        m_new = jnp.maximum(b_last + m_st[s], jnp.max(e_col, axis=0, keepdims=True))
        decay = jnp.exp(b_last + m_st[s] - m_new)
        wk = jnp.exp(e_col - m_new) * k[s]
        c_ref[bi * nh + h] = decay * c_ref[bi * nh + h] + _mm(wk.T, v[s])
        m_ref[bi * SUBLANES + h:bi * SUBLANES + h + 1, :] = jnp.broadcast_to(m_new, (1, m_ref.shape[1]))


def _mlstm(z, layer, conv_w, wq, wk, wv, b_row, b_col, norm_g):
    bsz, t, _ = z.shape
    lb = MLSTM_CHUNK
    hd = MLSTM_HEAD_DIM
    return pl.pallas_call(
        _mlstm_kernel,
        grid=(t // lb,),
        in_specs=[
            pl.BlockSpec((bsz, lb, WIDTH), lambda c: (0, c, ZC_AU // WIDTH)),
            pl.BlockSpec((bsz, lb, WIDTH), lambda c: (0, c, ZC_AO // WIDTH)),
            pl.BlockSpec((bsz, lb, SMALL), lambda c: (0, c, ZC_SM // SMALL)),
            _layer_spec(layer, MLSTM_CONV, WIDTH),
            _layer_spec(layer, MLSTM_HEADS, hd, hd),
            _layer_spec(layer, MLSTM_HEADS, hd, hd),
            _layer_spec(layer, MLSTM_HEADS, hd, hd),
            _layer_spec(layer, 1, SMALL),
            _layer_spec(layer, SMALL, 1),
            _layer_spec(layer, 1, WIDTH),
        ],
        out_specs=pl.BlockSpec((bsz, lb, WIDTH), lambda c: (0, c, 0)),
        out_shape=jax.ShapeDtypeStruct((bsz, t, WIDTH), BF16),
        scratch_shapes=[
            pltpu.VMEM((bsz, SUBLANES, WIDTH), F32),
            pltpu.VMEM((bsz * MLSTM_HEADS, hd, 2 * hd), F32),
            pltpu.VMEM((bsz * SUBLANES, hd), F32),
        ],
        compiler_params=pltpu.CompilerParams(
            dimension_semantics=("arbitrary",), vmem_limit_bytes=VMEM_LIMIT),
        name="mlstm",
    )(z, z, z, conv_w, wq, wk, wv, b_row, b_col, norm_g)


def _s5_kernel(u_ref, pin_ref, pout_ref, win_ref, wout_ref, lre_ref, lim_ref, d_ref, out_ref, x_ref, c_ref, *, bsz):
    steps = u_ref.shape[1]
    rows = steps * SUBLANES
    ns = S5_BLOCK_STATES
    cw = S5_BLOCK_CH
    nblk = S5_HALF // cw

    @pl.when(pl.program_id(0) == 0)
    def _():
        c_ref[...] = jnp.zeros_like(c_ref)

    u_bt = u_ref[...].reshape(bsz * steps, WIDTH)
    u = _mm(pin_ref[0], u_bt[:, :S5_HALF]) + _mm(pin_ref[1], u_bt[:, S5_HALF:])
    upper = (_iota((rows, 1), 0) % SUBLANES) >= bsz
    for j in range(nblk):
        uj = u[:, j * cw:(j + 1) * cw]
        uext = jnp.concatenate([jnp.where(upper, 0.0, uj), jnp.where(upper, uj, 0.0)], axis=1)
        x_ref[:, j * 2 * ns:(j + 1) * 2 * ns] = _mm(uext, win_ref[j])

    lre, lim = lre_ref[...], lim_ref[...]

    ys = []
    for j in range(nblk):
        lr, li = lre[:, j * ns:(j + 1) * ns], lim[:, j * ns:(j + 1) * ns]
        re_cols = slice(j * 2 * ns, j * 2 * ns + ns)
        im_cols = slice(j * 2 * ns + ns, (j + 1) * 2 * ns)
        sr, si = c_ref[:, re_cols], c_ref[:, im_cols]
        for t in range(steps):
            rows_t = slice(t * SUBLANES, (t + 1) * SUBLANES)
            sr, si = (lr * sr - li * si + x_ref[rows_t, re_cols],
                      lr * si + li * sr + x_ref[rows_t, im_cols])
            x_ref[rows_t, re_cols] = sr
            x_ref[rows_t, im_cols] = si
        c_ref[:, re_cols] = sr
        c_ref[:, im_cols] = si
        res = _mm(x_ref[:, j * 2 * ns:(j + 1) * 2 * ns], wout_ref[j])
        cols = slice(j * cw, (j + 1) * cw)
        y = jnp.where(upper, res[:, cw:], res[:, :cw]) + d_ref[:, cols] * u[:, cols]
        ys.append(0.5 * y * (1.0 + jnp.tanh(math.sqrt(2.0 / math.pi) * (y + 0.044715 * y * y * y))))
    y = jnp.concatenate(ys, axis=1).astype(BF16)
    out = jnp.concatenate([_mm(pout_ref[0], y), _mm(pout_ref[1], y)], axis=1)
    out_ref[...] = out.astype(out_ref.dtype).reshape(bsz, steps, WIDTH)


def _s5_permutations(bsz, steps):
    rows = steps * 2 * bsz
    r = np.arange(rows)
    p = np.zeros((2, rows, bsz * steps), np.float32)
    p[(r % (2 * bsz)) // bsz, r, (r % bsz) * steps + r // (2 * bsz)] = 1.0
    return jnp.asarray(p, BF16), jnp.asarray(p.transpose(0, 2, 1), BF16)


def _s5(z, bsz, layer, w_in, w_out, lam_re, lam_im, d_tab):
    t = z.shape[1]
    steps = S5_STEPS
    rows = steps * SUBLANES
    nblk = S5_HALF // S5_BLOCK_CH
    nst = nblk * S5_BLOCK_STATES
    p_in, p_out = _s5_permutations(bsz, steps)
    whole = lambda *shape: pl.BlockSpec(shape, lambda i: (0,) * len(shape))
    return pl.pallas_call(
        functools.partial(_s5_kernel, bsz=bsz),
        grid=(t // steps,),
        in_specs=[
            pl.BlockSpec((bsz, steps, WIDTH), lambda i: (0, i, ZC_BU // WIDTH)),
            whole(2, rows, bsz * steps),
            whole(2, bsz * steps, rows),
            _layer_spec(layer, nblk, 2 * S5_BLOCK_CH, 2 * S5_BLOCK_STATES),
            _layer_spec(layer, nblk, 2 * S5_BLOCK_STATES, 2 * S5_BLOCK_CH),
            _layer_spec(layer, SUBLANES, nst),
            _layer_spec(layer, SUBLANES, nst),
            _layer_spec(layer, rows, S5_HALF),
        ],
        out_specs=pl.BlockSpec((bsz, steps, WIDTH), lambda i: (0, i, 0)),
        out_shape=jax.ShapeDtypeStruct((bsz, t, WIDTH), BF16),
        scratch_shapes=[pltpu.VMEM((rows, 2 * nst), F32), pltpu.VMEM((SUBLANES, 2 * nst), F32)],
        compiler_params=pltpu.CompilerParams(
            dimension_semantics=("arbitrary",), vmem_limit_bytes=VMEM_LIMIT),
        name="s5",
    )(z, p_in, p_out, w_in, w_out, lam_re, lam_im, d_tab)


def _gla_kernel(q_ref, k_ref, v_ref, g_ref, sm_ref, aup_ref, ab_ref, ng_ref, out_ref, st_ref):
    nb, lb = q_ref.shape[0], q_ref.shape[1]
    ch = GLA_CHUNK
    kw = GLA_KEY_WIDTH

    @pl.when(pl.program_id(0) == 0)
    def _():
        st_ref[...] = jnp.zeros_like(st_ref)

    heads = range(GLA_HEADS)
    key_head = _iota((1, kw), 1) // GLA_HEAD_K
    val_head = _iota((1, WIDTH), 1) // GLA_HEAD_V
    key_masks = [jnp.where(key_head == h, 1.0, 0.0).astype(BF16) for h in heads]
    val_masks = [jnp.where(val_head == h, 1.0, 0.0).astype(BF16) for h in heads]

    def stack(x, masks):
        return jnp.concatenate([x * m for m in masks], axis=0)

    causal = _iota((ch, GLA_HEADS * ch), 0) >= _iota((ch, GLA_HEADS * ch), 1) % ch
    own_head = (_iota((WIDTH, kw), 0) // GLA_HEAD_V) == (_iota((WIDTH, kw), 1) // GLA_HEAD_K)
    lower = _tri(lb, ch)
    mid = ch // 2 - 1
    rss = [slice(c * ch, (c + 1) * ch) for c in range(lb // ch)]
    units = [(bi, c) for bi in range(nb) for c in range(lb // ch)]

    b_all = []
    for bi in range(nb):
        x = _mm(sm_ref[bi], aup_ref[...], pb=2) + ab_ref[...]
        b_all.append(_mm(lower, _log_sigmoid(x) * (1.0 / GLA_GATE_TAU), pb=3))
    q_in, q_at, k_at, k_st, dec = {}, {}, {}, {}, {}
    for bi, c in units:
        b = b_all[bi][rss[c]]
        b_mid, b_last = b[mid:mid + 1, :], b[ch - 1:ch, :]
        q = q_ref[bi, rss[c], :].astype(F32) * GLA_HEAD_K ** -0.5
        k = k_ref[bi, rss[c], :].astype(F32)
        q_in[bi, c] = (q * jnp.exp(b)).astype(BF16)
        q_at[bi, c] = (q * jnp.exp(b - b_mid)).astype(BF16)
        k_at[bi, c] = (k * jnp.exp(b_mid - b)).astype(BF16)
        k_st[bi, c] = (k * jnp.exp(b_last - b)).astype(BF16)
        dec[bi, c] = jnp.exp(b_last)
    v = {(bi, c): v_ref[bi, rss[c], :] for bi, c in units}
    attn = {u: jnp.where(causal, _mm(q_at[u], stack(k_at[u], key_masks), _NT), 0.0) for u in units}
    intra = {u: _mm(attn[u], stack(v[u], val_masks)) for u in units}
    grow = {u: jnp.where(own_head, _mm(v[u].astype(F32).T, k_st[u]), 0.0) for u in units}
    state = {}
    for bi in range(nb):
        st = st_ref[bi]
        for c in range(lb // ch):
            state[bi, c] = st
            st = dec[bi, c] * st + grow[bi, c]
        st_ref[bi] = st
    for bi, c in units:
        o = _mm(q_in[bi, c], state[bi, c], _NT) + intra[bi, c]
        for h in heads:
            vs = slice(h * GLA_HEAD_V, (h + 1) * GLA_HEAD_V)
            oh = o[:, vs]
            on = oh * lax.rsqrt(jnp.mean(oh * oh, axis=-1, keepdims=True) + NORM_EPS) * ng_ref[:, vs]
            g = g_ref[bi, rss[c], vs].astype(F32)
            out_ref[bi, rss[c], vs] = (on * (g * _sigmoid(g))).astype(out_ref.dtype)


def _gla(z, layer, a_up, a_bias, norm_g):
    bsz, t, _ = z.shape
    lb = TIME_BLOCK
    kw = GLA_KEY_WIDTH
    return pl.pallas_call(
        _gla_kernel,
        grid=(t // lb,),
        in_specs=[
            pl.BlockSpec((bsz, lb, kw), lambda c: (0, c, ZC_CQ // kw)),
            pl.BlockSpec((bsz, lb, kw), lambda c: (0, c, ZC_CK // kw)),
            pl.BlockSpec((bsz, lb, WIDTH), lambda c: (0, c, ZC_CV // WIDTH)),
            pl.BlockSpec((bsz, lb, WIDTH), lambda c: (0, c, ZC_CG // WIDTH)),
            pl.BlockSpec((bsz, lb, SMALL), lambda c: (0, c, ZC_SM // SMALL)),
            _layer_spec(layer, SMALL, kw),
            _layer_spec(layer, 1, kw),
            _layer_spec(layer, 1, WIDTH),
        ],
        out_specs=pl.BlockSpec((bsz, lb, WIDTH), lambda c: (0, c, 0)),
        out_shape=jax.ShapeDtypeStruct((bsz, t, WIDTH), BF16),
        scratch_shapes=[pltpu.VMEM((bsz, WIDTH, kw), F32)],
        compiler_params=pltpu.CompilerParams(
            dimension_semantics=("arbitrary",), vmem_limit_bytes=VMEM_LIMIT),
        name="gla",
    )(z, z, z, z, z, a_up, a_bias, norm_g)


def _rwkv_kernel(r_ref, k_ref, v_ref, t_ref, mu_ref, mut_ref,
                 w0_ref, wup_ref, a0_ref, aup_ref, gup_ref, kk_ref, ka_ref, rk_ref, ng_ref,
                 out_ref, pr_ref, pk_ref, pv_ref, pt_ref, st_ref, y_ref):
    nb, ch = r_ref.shape[0], r_ref.shape[1]
    lb = nb * ch
    gl = RWKV_GROUP_LANES
    gh = RWKV_GROUP_HEADS

    @pl.when(pl.program_id(0) == 0)
    def _():
        for ref in (pr_ref, pk_ref, pv_ref, pt_ref, st_ref):
            ref[...] = jnp.zeros_like(ref)

    def shift_mix(x_ref, prev_ref, mu):
        mixed = []
        for bi in range(nb):
            x = x_ref[bi].astype(F32)
            xs = _shift_rows(prev_ref.at[bi], x, 1)
            prev_ref[bi] = x[ch - prev_ref.shape[1]:]
            mixed.append(x + mu * (xs - x))
        return jnp.concatenate(mixed, axis=0)

    r = shift_mix(r_ref, pr_ref, mu_ref[0:1, :])
    k = shift_mix(k_ref, pk_ref, mu_ref[1:2, :])
    v = shift_mix(v_ref, pv_ref, mu_ref[2:3, :])
    tl = shift_mix(t_ref, pt_ref, mut_ref[...])

    ones_bd = _tri(gl, RWKV_HEAD) + _tri(gl, RWKV_HEAD, upper=True, strict=True)

    def head_sum(x):
        return jnp.concatenate([_mm(x[:, i * gl:(i + 1) * gl], ones_bd, pa=2) for i in range(WIDTH // gl)], axis=1)

    w_log = _log_sigmoid(w0_ref[...] + _mm(jnp.tanh(tl), wup_ref[...])) - 0.5
    log_w = -jnp.exp(w_log)
    a = _sigmoid(a0_ref[...] + _mm(tl, aup_ref[...]))
    g = _mm(_sigmoid(tl), gup_ref[...])
    kk = k * kk_ref[...]
    kk = kk / jnp.maximum(jnp.sqrt(head_sum(kk * kk)), 1e-12)
    k_rep = k * (1.0 + (a - 1.0) * ka_ref[...])
    av = -kk
    bv = kk * a
    bonus = head_sum(r * k_rep * rk_ref[...]) * v
    gc_all = _mm(_tri(lb, ch), log_w, pb=3)

    lane_head = _iota((1, gl), 1) // RWKV_HEAD
    masks = [jnp.where(lane_head == h, 1.0, 0.0).astype(BF16) for h in range(gh)]

    def stack(x):
        xb = x.astype(BF16)
        return jnp.concatenate([xb * m for m in masks], axis=0)

    def stack_t(x):
        return jnp.concatenate([x * m.astype(F32) for m in masks], axis=0).T

    s_lane = _iota((ch, gl), 1) % ch
    t_row = _iota((ch, gl), 0)
    strict = t_row > s_lane
    incl = t_row >= s_lane
    eye = jnp.where(t_row == s_lane, 1.0, 0.0)

    pairs = [(gi, bi) for gi in range(WIDTH // gl) for bi in range(nb)]
    pre = {}
    for gi, bi in pairs:
        ls = slice(gi * gl, (gi + 1) * gl)
        rs = slice(bi * ch, (bi + 1) * ch)
        gc = gc_all[rs, ls]
        g_last = gc[ch - 1:ch, :]
        e_inv = jnp.exp(-gc)
        e_end = jnp.exp(g_last - gc)
        r_t = (r[rs, ls] * jnp.exp(gc)).astype(BF16)
        a_t = (av[rs, ls] * jnp.exp(gc - log_w[rs, ls])).astype(BF16)
        scores = _mm(jnp.concatenate([a_t, r_t], axis=0),
                     jnp.concatenate([stack(bv[rs, ls] * e_inv), stack(k_rep[rs, ls] * e_inv)], axis=0), _NT)
        pre[gi, bi] = dict(
            r_t=r_t, a_bd=stack(a_t), v_bd=stack(v[rs, ls]), dec=jnp.exp(g_last),
            b_e=stack(bv[rs, ls] * e_end), k_e=stack(k_rep[rs, ls] * e_end),
            a_ab=jnp.where(strict, scores[:ch, :gl], 0.0),
            a_kv=jnp.concatenate([jnp.where(strict, scores[:ch, gl:], 0.0),
                                  jnp.where(incl, scores[ch:, gl:], 0.0)], axis=0).astype(BF16),
            a_rb=jnp.where(incl, scores[ch:, :gl], 0.0).astype(BF16))
    tinv = {key: eye + pre[key]["a_ab"] for key in pairs}
    power = {key: pre[key]["a_ab"].astype(BF16) for key in pairs}
    for key in pairs:
        power[key] = _mm(power[key], stack(power[key])).astype(BF16)
    step = 2
    while step < ch:
        for key in pairs:
            if 2 * step < ch:
                both = _mm(jnp.concatenate([power[key], tinv[key].astype(BF16)], axis=0), stack(power[key]))
                power[key] = both[:ch].astype(BF16)
                tinv[key] = tinv[key] + both[ch:]
            else:
                tinv[key] = tinv[key] + _mm(tinv[key], stack(power[key]))
        step *= 2

    from_v = {key: _mm(pre[key]["a_kv"], pre[key]["v_bd"]) for key in pairs}
    t_bf = {key: tinv[key].astype(BF16) for key in pairs}
    a_hat = {key: _mm(t_bf[key], pre[key]["a_bd"]).astype(BF16) for key in pairs}
    u0 = {key: _mm(t_bf[key], stack(from_v[key][:ch])) for key in pairs}
    grow = {key: _mm(pre[key]["v_bd"].astype(F32).T, pre[key]["k_e"]) for key in pairs}
    n_groups = WIDTH // gl
    state = {(gi, bi): st_ref[bi * n_groups + gi] for gi, bi in pairs}
    from_state = {key: _mm(jnp.concatenate([a_hat[key], pre[key]["r_t"]], axis=0), state[key], _NT) for key in pairs}
    u = {key: from_state[key][:ch] + u0[key] for key in pairs}
    for gi, bi in pairs:
        key = (gi, bi)
        y_ref[bi * ch:(bi + 1) * ch, gi * gl:(gi + 1) * gl] = (
            from_state[key][ch:] + _mm(pre[key]["a_rb"], stack(u[key])) + from_v[key][ch:])
    for gi, bi in pairs:
        key = (gi, bi)
        st_ref[bi * n_groups + gi] = (pre[key]["dec"] * state[key] + _mm(stack_t(u[key]), pre[key]["b_e"])
                                      + grow[key])

    y = y_ref[...]
    inv_n = 1.0 / RWKV_HEAD
    mu = head_sum(y) * inv_n
    yc = y - mu
    var = head_sum(yc * yc) * inv_n
    yn = yc * lax.rsqrt(var + RWKV_GN_EPS) * ng_ref[...]
    out_ref[...] = ((yn + bonus) * g).astype(out_ref.dtype).reshape(nb, ch, WIDTH)


def _rwkv(z, layer, mu_rkv, mu_t, w0, w_up, a0, a_up, g_up, k_k, k_a, r_k, norm_g):
    bsz, t, _ = z.shape
    ch = RWKV_CHUNK
    gl = RWKV_GROUP_LANES
    vec = _layer_spec(layer, 1, WIDTH)
    mat = _layer_spec(layer, RWKV_TAIL, WIDTH)
    return pl.pallas_call(
        _rwkv_kernel,
        grid=(t // ch,),
        in_specs=[
            pl.BlockSpec((bsz, ch, WIDTH), lambda c: (0, c, ZC_DR // WIDTH)),
            pl.BlockSpec((bsz, ch, WIDTH), lambda c: (0, c, ZC_DK // WIDTH)),
            pl.BlockSpec((bsz, ch, WIDTH), lambda c: (0, c, ZC_DV // WIDTH)),
            pl.BlockSpec((bsz, ch, RWKV_TAIL), lambda c: (0, c, ZC_DT // RWKV_TAIL)),
            _layer_spec(layer, 3, WIDTH), _layer_spec(layer, 1, RWKV_TAIL),
            vec, mat, vec, mat, mat, vec, vec, vec, vec,
        ],
        out_specs=pl.BlockSpec((bsz, ch, WIDTH), lambda c: (0, c, 0)),
        out_shape=jax.ShapeDtypeStruct((bsz, t, WIDTH), BF16),
        scratch_shapes=[
            pltpu.VMEM((bsz, SUBLANES, WIDTH), F32), pltpu.VMEM((bsz, SUBLANES, WIDTH), F32),
            pltpu.VMEM((bsz, SUBLANES, WIDTH), F32), pltpu.VMEM((bsz, SUBLANES, RWKV_TAIL), F32),
            pltpu.VMEM((bsz * (WIDTH // gl), gl, gl), F32),
            pltpu.VMEM((bsz * ch, WIDTH), F32),
        ],
        compiler_params=pltpu.CompilerParams(
            dimension_semantics=("arbitrary",), vmem_limit_bytes=VMEM_LIMIT),
        name="rwkv",
    )(z, z, z, z, mu_rkv, mu_t, w0, w_up, a0, a_up, g_up, k_k, k_a, r_k, norm_g)


def _merge_kernel(x_ref, ya_ref, zb_ref, yc_ref, yd_ref, g0_ref, g1_ref, g2_ref, g3_ref, gb_ref,
                  pa_ref, w1_ref, w2_ref, pc_ref, pd_ref, wo_ref, out_ref):
    d = D_MODEL

    def gate(g_ref, i):
        return _sigmoid(g_ref[...].astype(F32) + gb_ref[:, i * d:(i + 1) * d])

    zb = zb_ref[...]
    merged = gate(g0_ref, 0) * _mm(ya_ref[...], pa_ref[...])
    merged = merged + gate(g1_ref, 1) * (_mm(zb, w1_ref[...]) * _sigmoid(_mm(zb, w2_ref[...])))
    merged = merged + gate(g2_ref, 2) * _mm(yc_ref[...], pc_ref[...])
    merged = merged + gate(g3_ref, 3) * _mm(yd_ref[...], pd_ref[...])
    out_ref[...] = x_ref[...] + _mm(merged, wo_ref[...])


def _merge(x, z, ya, zb, yc, yd, layer, gate_bias, pa, w1, w2, pc, pd, wo, *, tm=512):
    n, d = x.shape
    tile = lambda w: pl.BlockSpec((tm, w), lambda i: (i, 0))
    gate = lambda b: pl.BlockSpec((tm, d), lambda i: (i, ZC_GATE // d + b))
    resident = lambda rows: pl.BlockSpec((None, rows, d), lambda i: (layer, 0, 0), pipeline_mode=pl.Buffered(1))
    proj = resident(WIDTH)
    return pl.pallas_call(
        _merge_kernel,
        grid=(n // tm,),
        in_specs=[tile(d), tile(WIDTH), tile(WIDTH), tile(WIDTH), tile(WIDTH),
                  gate(0), gate(1), gate(2), gate(3), _layer_spec(layer, 1, N_BRANCH * d),
                  proj, proj, proj, proj, proj, resident(d)],
        out_specs=tile(d),
        out_shape=jax.ShapeDtypeStruct((n, d), F32),
        compiler_params=pltpu.CompilerParams(
            dimension_semantics=("parallel",), vmem_limit_bytes=VMEM_LIMIT),
        name="merge",
    )(x, ya, zb, yc, yd, z, z, z, z, gate_bias, pa, w1, w2, pc, pd, wo)


def _xattn_kernel(x_ref, g_ref, wq_ref, mem_ref, mg_ref, wk_ref, wv_ref, wo_ref, out_ref, kv_ref):
    d = D_MODEL
    hd = XATTN_HEAD_DIM

    @pl.when(pl.program_id(1) == 0)
    def _():
        mn = _rms(mem_ref[...], mg_ref[...]).astype(BF16)
        kv_ref[:, :d] = _mm(mn, wk_ref[...]).astype(BF16)
        kv_ref[:, d:] = _mm(mn, wv_ref[...]).astype(BF16)

    x = x_ref[...]
    q = _mm(_rms(x, g_ref[...]), wq_ref[...])
    kv = kv_ref[...]
    heads = range(XATTN_HEADS)
    sls = [slice(h * hd, (h + 1) * hd) for h in heads]
    qb = q.astype(BF16)
    s = [_mm(qb[:, sls[h]], kv[:, sls[h]], _NT) * hd ** -0.5 for h in heads]
    p = [jnp.exp(s[h] - jnp.max(s[h], axis=-1, keepdims=True)) for h in heads]
    p = [p[h] / jnp.sum(p[h], axis=-1, keepdims=True) for h in heads]
    o = jnp.concatenate([_mm(p[h], kv[:, d + h * hd:d + (h + 1) * hd]) for h in heads], axis=1)
    out_ref[...] = x + _mm(o, wo_ref[...])


def _xattn(x, mem, bsz, layer, gain, mem_gain, wq, wk, wv, wo, *, tm=1024):
    n, d = x.shape
    nblk = n // bsz // tm
    m_len = mem.shape[0] // bsz
    weight = pl.BlockSpec((None, d, d), lambda b, c: (layer, 0, 0), pipeline_mode=pl.Buffered(1))
    return pl.pallas_call(
        _xattn_kernel,
        grid=(bsz, nblk),
        in_specs=[
            pl.BlockSpec((tm, d), lambda b, c: (b * nblk + c, 0)),
            _layer_spec(layer, 1, d),
            weight,
            pl.BlockSpec((m_len, d), lambda b, c: (b, 0)),
            _layer_spec(layer, 1, d),
            weight, weight, weight,
        ],
        out_specs=pl.BlockSpec((tm, d), lambda b, c: (b * nblk + c, 0)),
        out_shape=jax.ShapeDtypeStruct((n, d), F32),
        scratch_shapes=[pltpu.VMEM((m_len, 2 * d), BF16)],
        compiler_params=pltpu.CompilerParams(
            dimension_semantics=("parallel", "arbitrary"), vmem_limit_bytes=VMEM_LIMIT),
        name="xattn",
    )(x, gain, wq, mem, mem_gain, wk, wv, wo)


def _w_in_segments():
    o = np.cumsum((0, WIDTH, WIDTH, MLSTM_HEADS, MLSTM_HEADS, WIDTH, GLA_KEY_WIDTH, GLA_KEY_WIDTH, WIDTH, WIDTH,
                   GLA_GATE_RANK, 3 * WIDTH + RWKV_TAIL, N_BRANCH * D_MODEL))
    a_u, a_o, a_i, a_f, b_u, c_q, c_k, c_v, c_g, c_a, d_z, gate = range(12)
    seg = lambda i: (int(o[i]), int(o[i + 1] - o[i]))
    d_parts = [(int(o[d_z]) + i * WIDTH, WIDTH) for i in range(3)] + [(int(o[d_z]) + 3 * WIDTH, RWKV_TAIL)]
    return ([seg(gate), seg(a_u), seg(a_o), seg(b_u), seg(c_v), seg(c_g)] + d_parts[:3]
            + [seg(c_q), seg(c_k), d_parts[3], seg(a_i), seg(a_f), seg(c_a)])


def _regroup_kernel(w_ref, o_ref):
    lanes = w_ref.shape[1]
    segments = _w_in_segments()
    big = [s for s in segments if s[1] % SMALL == 0]
    dst = 0
    for src, width in big:
        o_ref[dst:dst + width, :] = w_ref[src:src + width, :].astype(o_ref.dtype)
        dst += width
    small = [w_ref[src:src + width, :] for src, width in segments if width % SMALL]
    used = sum(s.shape[0] for s in small)
    small.append(jnp.zeros((o_ref.shape[0] - dst - used, lanes), F32))
    o_ref[dst:, :] = jnp.concatenate(small, axis=0).astype(o_ref.dtype)


def _regroup_w_in(w_t, *, lanes=256):
    depth, n_in, d = w_t.shape
    return pl.pallas_call(
        _regroup_kernel,
        grid=(depth, d // lanes),
        in_specs=[pl.BlockSpec((None, n_in, lanes), lambda l, i: (l, 0, i))],
        out_specs=pl.BlockSpec((None, Z_COLS, lanes), lambda l, i: (l, 0, i)),
        out_shape=jax.ShapeDtypeStruct((depth, Z_COLS, d), BF16),
        compiler_params=pltpu.CompilerParams(
            dimension_semantics=("parallel", "parallel"), vmem_limit_bytes=VMEM_LIMIT),
        name="regroup_w_in",
    )(w_t)


def _s5_tables(a_re, a_im, log_step, b_re, b_im, c_re, c_im, d_skip, bsz):
    depth = a_re.shape[0]
    step = jnp.exp(log_step)[..., None]
    lam_re = jnp.minimum(a_re, -S5_MIN_NEG)
    lam_im = a_im
    mag = jnp.exp(lam_re * step)
    bar_re = mag * jnp.cos(lam_im * step)
    bar_im = mag * jnp.sin(lam_im * step)
    denom = lam_re * lam_re + lam_im * lam_im
    coef_re = ((bar_re - 1.0) * lam_re + bar_im * lam_im) / denom
    coef_im = (bar_im * lam_re - (bar_re - 1.0) * lam_im) / denom
    bb_re = coef_re[..., None] * b_re - coef_im[..., None] * b_im
    bb_im = coef_re[..., None] * b_im + coef_im[..., None] * b_re
    nb = S5_HALF // S5_BLOCK_CH
    gpb = S5_BLOCK_GROUPS
    eye = jnp.eye(gpb, dtype=F32)

    def in_block(bb):
        bb = bb.reshape(depth, 2, nb, gpb, S5_STATE, S5_GROUP)
        return jnp.einsum('lhjgpc,gk->lhjgckp', bb, eye).reshape(depth, 2, nb, S5_BLOCK_CH, S5_BLOCK_STATES)

    def out_block(cc):
        cc = cc.reshape(depth, 2, nb, gpb, S5_GROUP, S5_STATE)
        return jnp.einsum('lhjgcp,gk->lhjgpkc', cc, eye).reshape(depth, 2, nb, S5_BLOCK_STATES, S5_BLOCK_CH)

    w_in = jnp.concatenate([in_block(bb_re), in_block(bb_im)], axis=-1)
    w_in = w_in.transpose(0, 2, 1, 3, 4).reshape(depth, nb, 2 * S5_BLOCK_CH, 2 * S5_BLOCK_STATES)
    w_out = jnp.concatenate([out_block(c_re), -out_block(c_im)], axis=-2)
    w_out = w_out.transpose(0, 2, 3, 1, 4).reshape(depth, nb, 2 * S5_BLOCK_STATES, 2 * S5_BLOCK_CH)

    def rows(t, width):
        t = t.reshape(depth, 2, 1, width)
        return jnp.broadcast_to(t, (depth, 2, bsz, width)).reshape(depth, 2 * bsz, width)

    nst = nb * S5_BLOCK_STATES
    lam_re_rows = rows(bar_re.reshape(depth, 2 * nst), nst)
    lam_im_rows = rows(bar_im.reshape(depth, 2 * nst), nst)
    d_tab = jnp.tile(rows(d_skip, S5_HALF), (1, S5_STEPS, 1))
    return w_in.astype(BF16), w_out.astype(BF16), lam_re_rows, lam_im_rows, d_tab


def _pad_rows(w, start, total):
    return jnp.pad(w, ((0, 0), (start, total - start - w.shape[1]), (0, 0)))


def kernel(x, mem, ffn1_norm, ffn1_w_gate, ffn1_w_up, ffn1_w_down, mix_norm, w_in, gate_bias, mlstm_conv, mlstm_wq, mlstm_wk, mlstm_wv, mlstm_b_i, mlstm_b_f, mlstm_norm, mlstm_proj, s5_a_re, s5_a_im, s5_log_step, s5_b_re, s5_b_im, s5_c_re, s5_c_im, s5_d, s5_glu_w1, s5_glu_w2, gla_a_up, gla_a_bias, gla_norm, gla_proj, rwkv_mu, rwkv_w0, rwkv_w_up, rwkv_a0, rwkv_a_up, rwkv_g_up, rwkv_k_k, rwkv_k_a, rwkv_r_k, rwkv_norm, rwkv_proj, w_out, xattn_norm, mem_norm, xattn_wq, xattn_wk, xattn_wv, xattn_wo, ffn2_norm, ffn2_w_gate, ffn2_w_up, ffn2_w_down, final_norm):
    bsz, t, d = x.shape
    depth = w_in.shape[0]
    assert d == D_MODEL and t % TIME_BLOCK == 0 and 2 * bsz == SUBLANES
    xs = x.reshape(bsz * t, d)
    mems = mem.reshape(bsz * mem.shape[1], d)
    bf = lambda w: w.astype(BF16)
    vec = lambda v: v.reshape(depth, 1, -1)
    fin = final_norm.reshape(1, d)

    w_in_r = _regroup_w_in(jnp.swapaxes(w_in, 1, 2))
    gate_b = jnp.pad(jnp.concatenate([mlstm_b_i, mlstm_b_f], axis=1), ((0, 0), (0, SMALL - 2 * MLSTM_HEADS)))
    s5_w_in, s5_w_out, s5_lre, s5_lim, s5_dtab = _s5_tables(
        s5_a_re, s5_a_im, s5_log_step, s5_b_re, s5_b_im, s5_c_re, s5_c_im, s5_d, bsz)
    gla_up = _pad_rows(gla_a_up, 2 * MLSTM_HEADS, SMALL)
    mu_rkv = rwkv_mu[:, :3 * WIDTH].reshape(depth, 3, WIDTH)
    mu_t = vec(rwkv_mu[:, 3 * WIDTH:])
    rw_wup = bf(_pad_rows(rwkv_w_up, 0, RWKV_TAIL))
    rw_aup = bf(_pad_rows(rwkv_a_up, RWKV_DECAY_RANK, RWKV_TAIL))
    rw_gup = bf(_pad_rows(rwkv_g_up, RWKV_DECAY_RANK + RWKV_ICLR_RANK, RWKV_TAIL))
    wq_a, wk_a, wv_a = bf(mlstm_wq), bf(mlstm_wk), bf(mlstm_wv)
    p_a, p_b1, p_b2, p_c, p_d, p_o = mlstm_proj, s5_glu_w1, s5_glu_w2, gla_proj, rwkv_proj, w_out
    x_wq, x_wo = xattn_wq, xattn_wo
    f1_g, f1_u, f1_d = ffn1_w_gate, ffn1_w_up, ffn1_w_down
    f2_g, f2_u, f2_d = ffn2_w_gate, ffn2_w_up, ffn2_w_down

    for l in range(depth):
        xs = _ffn(xs, vec(ffn1_norm), f1_g, f1_u, f1_d, fin, layer=l, final=False)

        z = _in_proj(xs, vec(mix_norm), w_in_r, layer=l)

        z3 = z.reshape(bsz, t, Z_COLS)
        y_a = _mlstm(z3, l, mlstm_conv, wq_a, wk_a, wv_a, vec(gate_b), gate_b.reshape(depth, SMALL, 1),
                     vec(mlstm_norm)).reshape(bsz * t, WIDTH)

        z_b = _s5(z3, bsz, l, s5_w_in, s5_w_out, s5_lre, s5_lim, s5_dtab).reshape(bsz * t, WIDTH)

        y_c = _gla(z3, l, gla_up, vec(gla_a_bias), vec(gla_norm)).reshape(bsz * t, WIDTH)

        y_d = _rwkv(z3, l, mu_rkv, mu_t, vec(rwkv_w0), rw_wup, vec(rwkv_a0), rw_aup, rw_gup,
                    vec(rwkv_k_k), vec(rwkv_k_a), vec(rwkv_r_k), vec(rwkv_norm)).reshape(bsz * t, WIDTH)

        xs = _merge(xs, z, y_a, z_b, y_c, y_d, l, vec(gate_bias), p_a, p_b1, p_b2, p_c, p_d, p_o)

        xs = _xattn(xs, mems, bsz, l, vec(xattn_norm), vec(mem_norm), x_wq, xattn_wk, xattn_wv, x_wo)

        xs = _ffn(xs, vec(ffn2_norm), f2_g, f2_u, f2_d, fin, layer=l,
                  final=(l == depth - 1))
    return xs.reshape(bsz, t, d)
```
